```python
import jax
import jax.numpy as jnp
from jax import lax
import numpy as np

D_MODEL = 1024
BATCH = 32
SEQ = 256
DEPTH = 2
DEC_BATCH = 2
DEC_SEQ = 1024
PAST_LEN = 256

GRID_W = 64
EPS = 1e-6
N_MOD = 6
SSD_HEADDIM = 64
SSD_HEADS = D_MODEL // SSD_HEADDIM
SSD_INNER = SSD_HEADS * SSD_HEADDIM
SSD_GROUPS = 2
SSD_STATE = 128
SSD_CONV_W = 3
SSD_CHUNK = 128
SSD_XBC = SSD_INNER + 2 * SSD_GROUPS * SSD_STATE
SC_WIDTH = D_MODEL
SC_CONV_W = 3
MIX0_IN = SSD_INNER + SSD_XBC + 2 * SSD_HEADS + 3 * SC_WIDTH
MIX0_OUT = SSD_INNER + SC_WIDTH
NA_HEAD_DIM = 64
NA_HEADS = D_MODEL // NA_HEAD_DIM
NA_INNER = NA_HEADS * NA_HEAD_DIM
NA_KH = 8
NA_KW = 16
Q_BLOCK = 128
D_FF = ((8 * D_MODEL // 3 + 255) // 256) * 256
FFN_CONV_W = 3
N_SSD_LAYERS = (DEPTH + 1) // 2
N_NA_LAYERS = DEPTH // 2

kernel_name = 'hybrid_ssd_conv_natten_prefix_dit_step'


def rmsnorm(x, g):
    xf = x.astype(jnp.float32)
    y = xf * lax.rsqrt(jnp.mean(xf * xf, axis=-1, keepdims=True) + EPS)
    return (y * g.astype(jnp.float32)).astype(x.dtype)


def dwconv(x, w):
    k, ch = w.shape
    pl = (k - 1) // 2
    return lax.conv_general_dilated(x, w[:, None, :].astype(x.dtype), window_strides=(1,),
                                    padding=[(pl, k - 1 - pl)], dimension_numbers=('NWC', 'WIO', 'NWC'),
                                    feature_group_count=ch)


def ssd_scan(x, dt, a, bm, cm, h0):
    b, l, h, p = x.shape
    nc = l // SSD_CHUNK

    def chunks(t):
        return t.reshape((b, nc, SSD_CHUNK) + t.shape[2:])

    xd = chunks(x * dt[..., None])
    bm, cm = chunks(bm), chunks(cm)
    a_cs = jnp.cumsum(jnp.moveaxis(chunks(dt * a), -1, 1), axis=-1)
    causal = jnp.tril(jnp.ones((SSD_CHUNK, SSD_CHUNK), dtype=bool))
    seg = a_cs[..., :, None] - a_cs[..., None, :]
    decay = jnp.where(causal, jnp.exp(jnp.where(causal, seg, 0.0)), 0.0)
    cb = jnp.einsum('bclhn,bcshn->bhcls', cm, bm) * decay
    y = jnp.einsum('bhcls,bcshp->bclhp', cb, xd)
    to_end = jnp.exp(a_cs[..., -1:] - a_cs)
    st = jnp.einsum('bclhn,bhcl,bclhp->bchpn', bm, to_end, xd)
    chunk_decay = jnp.exp(a_cs[..., -1])

    def step(hc, inp):
        s_c, d_c = inp
        return hc * d_c[..., None, None] + s_c, hc

    h_final, h_in = lax.scan(step, h0, (jnp.moveaxis(st, 1, 0), jnp.moveaxis(chunk_decay, -1, 0)))
    y = y + jnp.einsum('bclhn,cbhpn,bhcl->bclhp', cm, h_in, jnp.exp(a_cs))
    return y.reshape(b, l, h, p), h_final


def ssd_mixer(z, xbc, dt_raw, conv_w, conv_b, dt_bias, a_log, d_skip, norm_g, h0):
    b, l, _ = z.shape
    xbc = jax.nn.silu(dwconv(xbc, conv_w) + conv_b).astype(jnp.float32)
    xs, bm, cm = jnp.split(xbc, [SSD_INNER, SSD_INNER + SSD_GROUPS * SSD_STATE], axis=-1)
    xs = xs.reshape(b, l, SSD_HEADS, SSD_HEADDIM)
    rep = SSD_HEADS // SSD_GROUPS
    bm = jnp.repeat(bm.reshape(b, l, SSD_GROUPS, SSD_STATE), rep, axis=2)
    cm = jnp.repeat(cm.reshape(b, l, SSD_GROUPS, SSD_STATE), rep, axis=2)
    dt = jax.nn.softplus(dt_raw.astype(jnp.float32).reshape(b, l, 2, SSD_HEADS) + dt_bias.astype(jnp.float32))
    a = -jnp.exp(a_log.astype(jnp.float32))
    h0 = h0.astype(jnp.float32)
    y_f, s_f = ssd_scan(xs, dt[:, :, 0], a[0], bm, cm, h0[:, 0])

    def flip(t):
        return jnp.flip(t, axis=1)

    y_b, s_b = ssd_scan(flip(xs), flip(dt[:, :, 1]), a[1], flip(bm), flip(cm), h0[:, 1])
    y = y_f + flip(y_b) + xs * jnp.sum(d_skip.astype(jnp.float32), axis=0)[:, None]
    y = y.reshape(b, l, SSD_INNER) * jax.nn.silu(z.astype(jnp.float32))
    y = rmsnorm(y, norm_g).astype(z.dtype)
    return y, jnp.stack([s_f, s_b], axis=1)


def mixer_ssd_conv(h, P, e, h0):
    u = h @ P['ssd_w_in'][e]
    s1 = SSD_INNER
    s2 = s1 + SSD_XBC
    s3 = s2 + 2 * SSD_HEADS
    s4 = s3 + SC_WIDTH
    s5 = s4 + SC_WIDTH
    z, xbc, dt_raw, bg, cg, xin = jnp.split(u, [s1, s2, s3, s4, s5], axis=-1)
    y_a, states = ssd_mixer(z, xbc, dt_raw, P['ssd_conv_w'][e], P['ssd_conv_b'][e], P['ssd_dt_bias'][e],
                            P['ssd_a_log'][e], P['ssd_d'][e], P['ssd_norm_g'][e], h0)
    y_b = bg * dwconv(cg * xin, P['sc_conv_w'][e])
    return jnp.concatenate([y_a, y_b], axis=-1) @ P['mix0_w_out'][e], states


def split_heads(t, b, l):
    return t.reshape(b, l, NA_HEADS, NA_HEAD_DIM)


def dense_context_attention(q, k, v):
    b, s, h, d = q.shape
    qb = jnp.moveaxis(q.reshape(b, s // Q_BLOCK, Q_BLOCK, h, d), 1, 0)

    def one_block(qi):
        sc = jnp.einsum('bqhd,bkhd->bhqk', qi, k).astype(jnp.float32) * (d ** -0.5)
        p = jax.nn.softmax(sc, axis=-1).astype(v.dtype)
        return jnp.einsum('bhqk,bkhd->bqhd', p, v)

    o = lax.map(one_block, qb)
    return jnp.moveaxis(o, 0, 1).reshape(b, s, h * d)


def na_context(h, P, o):
    b, s, _ = h.shape
    q, k, v = [split_heads(t, b, s) for t in jnp.split(h @ P['na_w_qkv'][o], 3, axis=-1)]
    y = dense_context_attention(q, k, v)
    return y @ P['na_w_out'][o], k, v


def na_latent(h, P, o, k_ctx, v_ctx):
    b, l, _ = h.shape
    rows = l // GRID_W
    kh = min(NA_KH, rows)
    nb = GRID_W // NA_KW
    span = 2 * NA_KW
    q, k, v = [split_heads(t, b, l) for t in jnp.split(h @ P['na_w_qkv'][o], 3, axis=-1)]
    r = np.arange(rows)
    row_idx = np.clip(r - kh // 2, 0, rows - kh)[:, None] + np.arange(kh)
    qcol = np.arange(GRID_W).reshape(nb, NA_KW)
    col_idx = np.clip(np.arange(nb) * NA_KW - NA_KW // 2, 0, GRID_W - span)[:, None] + np.arange(span)
    col_start = np.clip(qcol - NA_KW // 2, 0, GRID_W - NA_KW)
    col_ok = (col_idx[:, None, :] >= col_start[:, :, None]) & (col_idx[:, None, :] < col_start[:, :, None] + NA_KW)
    dr = row_idx - r[:, None] + NA_KH - 1
    dc = np.clip(col_idx[:, None, :] - qcol[:, :, None] + NA_KW - 1, 0, 2 * NA_KW - 2)
    rpb = P['na_rpb'][o].astype(jnp.float32)
    bias = rpb[:, dr[:, None, None, :, None], dc[None, :, :, None, :]]
    bias = jnp.where(col_ok[None, None, :, :, None, :], bias, -jnp.inf)
    bias = jnp.moveaxis(bias, 0, 2).reshape(rows, nb, NA_HEADS, NA_KW, kh * span)
    kg = k.reshape(b, rows, GRID_W, NA_HEADS, NA_HEAD_DIM)
    vg = v.reshape(b, rows, GRID_W, NA_HEADS, NA_HEAD_DIM)
    ri = row_idx[:, None, :, None]
    ci = col_idx[None, :, None, :]
    n_loc = kh * span
    k_loc = kg[:, ri, ci].reshape(b, rows, nb, n_loc, NA_HEADS, NA_HEAD_DIM)
    v_loc = vg[:, ri, ci].reshape(b, rows, nb, n_loc, NA_HEADS, NA_HEAD_DIM)
    qb = q.reshape(b, rows, nb, NA_KW, NA_HEADS, NA_HEAD_DIM)
    scale = NA_HEAD_DIM ** -0.5
    s_loc = jnp.einsum('brjqhd,brjkhd->brjhqk', qb, k_loc).astype(jnp.float32) * scale + bias
    s_ctx = jnp.einsum('brjqhd,bshd->brjhqs', qb, k_ctx).astype(jnp.float32) * scale
    p = jax.nn.softmax(jnp.concatenate([s_loc, s_ctx], axis=-1), axis=-1).astype(v.dtype)
    out = (jnp.einsum('brjhqk,brjkhd->brjqhd', p[..., :n_loc], v_loc)
           + jnp.einsum('brjhqs,bshd->brjqhd', p[..., n_loc:], v_ctx))
    return out.reshape(b, l, NA_INNER) @ P['na_w_out'][o]


def conv_ffn(h, P, i):
    g = dwconv(h @ P['ffn_w_gate'][i], P['ffn_conv_w'][i])
    u = h @ P['ffn_w_up'][i]
    return (jax.nn.gelu(g) * u) @ P['ffn_w_down'][i]


def run_stream(x, cvec, P, is_ctx, state_ssd, cache_k, cache_v):
    new_ssd, new_k, new_v = [], [], []
    b = x.shape[0]
    for i in range(DEPTH):
        m = jax.nn.silu(cvec) @ P['ada_w'][i] + P['ada_b'][i]
        sh1, sc1, g1, sh2, sc2, g2 = jnp.split(m[:, None, :].astype(x.dtype), N_MOD, axis=-1)
        h = rmsnorm(x, P['norm_mix_g'][i]) * (1 + sc1) + sh1
        if i % 2 == 0:
            e = i // 2
            if is_ctx:
                h0 = jnp.zeros((b, 2, SSD_HEADS, SSD_HEADDIM, SSD_STATE), jnp.float32)
            else:
                h0 = state_ssd[:, e]
            y, st = mixer_ssd_conv(h, P, e, h0)
            new_ssd.append(st.astype(x.dtype))
        else:
            o = i // 2
            if is_ctx:
                y, k, v = na_context(h, P, o)
                new_k.append(k)
                new_v.append(v)
            else:
                y = na_latent(h, P, o, cache_k[:, o], cache_v[:, o])
        x = x + g1 * y
        h = rmsnorm(x, P['norm_ffn_g'][i]) * (1 + sc2) + sh2
        x = x + g2 * conv_ffn(h, P, i)
    return rmsnorm(x, P['final_norm_g']), new_ssd, new_k, new_v


def setup_inputs(seed: int = 0) -> dict:
    key = jax.random.key(seed)
    ks = iter(jax.random.split(key, 40))

    def nrm(shape, s):
        return jax.random.normal(next(ks), shape, jnp.float32) * s

    dt0 = jnp.exp(jax.random.uniform(next(ks), (N_SSD_LAYERS, 2, SSD_HEADS), jnp.float32,
                                     np.log(1e-3), np.log(1e-1)))
    return {
        'x_prompt': nrm((BATCH, SEQ, D_MODEL), 1.0),
        'x_sample': nrm((DEC_BATCH, DEC_SEQ, D_MODEL), 1.0),
        'state_ssd': nrm((DEC_BATCH, N_SSD_LAYERS, 2, SSD_HEADS, SSD_HEADDIM, SSD_STATE), 0.5),
        'cache_k': nrm((DEC_BATCH, N_NA_LAYERS, PAST_LEN, NA_HEADS, NA_HEAD_DIM), 1.0),
        'cache_v': nrm((DEC_BATCH, N_NA_LAYERS, PAST_LEN, NA_HEADS, NA_HEAD_DIM), 1.0),
        'c': nrm((DEC_BATCH, D_MODEL), 1.0),
        'c_ctx': nrm((D_MODEL,), 1.0),
        'ada_w': nrm((DEPTH, D_MODEL, N_MOD * D_MODEL), 0.5 * D_MODEL ** -0.5),
        'ada_b': nrm((DEPTH, N_MOD * D_MODEL), 0.01),
        'norm_mix_g': 1.0 + nrm((DEPTH, D_MODEL), 0.02),
        'norm_ffn_g': 1.0 + nrm((DEPTH, D_MODEL), 0.02),
        'ssd_w_in': nrm((N_SSD_LAYERS, D_MODEL, MIX0_IN), D_MODEL ** -0.5),
        'ssd_conv_w': nrm((N_SSD_LAYERS, SSD_CONV_W, SSD_XBC), SSD_CONV_W ** -0.5),
        'ssd_conv_b': nrm((N_SSD_LAYERS, SSD_XBC), 0.01),
        'ssd_dt_bias': dt0 + jnp.log(-jnp.expm1(-dt0)),
        'ssd_a_log': jnp.log(jax.random.uniform(next(ks), (N_SSD_LAYERS, 2, SSD_HEADS), jnp.float32, 1.0, 16.0)),
        'ssd_d': 0.5 + nrm((N_SSD_LAYERS, 2, SSD_HEADS), 0.02),
        'ssd_norm_g': 1.0 + nrm((N_SSD_LAYERS, SSD_INNER), 0.02),
        'sc_conv_w': nrm((N_SSD_LAYERS, SC_CONV_W, SC_WIDTH), SC_CONV_W ** -0.5),
        'mix0_w_out': nrm((N_SSD_LAYERS, MIX0_OUT, D_MODEL), MIX0_OUT ** -0.5),
        'na_w_qkv': nrm((N_NA_LAYERS, D_MODEL, 3 * NA_INNER), D_MODEL ** -0.5),
        'na_rpb': nrm((N_NA_LAYERS, NA_HEADS, 2 * NA_KH - 1, 2 * NA_KW - 1), 0.02),
        'na_w_out': nrm((N_NA_LAYERS, NA_INNER, D_MODEL), NA_INNER ** -0.5),
        'ffn_w_gate': nrm((DEPTH, D_MODEL, D_FF), D_MODEL ** -0.5),
        'ffn_w_up': nrm((DEPTH, D_MODEL, D_FF), D_MODEL ** -0.5),
        'ffn_conv_w': nrm((DEPTH, FFN_CONV_W, D_FF), FFN_CONV_W ** -0.5),
        'ffn_w_down': nrm((DEPTH, D_FF, D_MODEL), D_FF ** -0.5),
        'final_norm_g': 1.0 + nrm((D_MODEL,), 0.02),
    }


def reference(x_prompt, x_sample, state_ssd, cache_k, cache_v, c, c_ctx, ada_w, ada_b, norm_mix_g, norm_ffn_g,
              ssd_w_in, ssd_conv_w, ssd_conv_b, ssd_dt_bias, ssd_a_log, ssd_d, ssd_norm_g, sc_conv_w, mix0_w_out,
              na_w_qkv, na_rpb, na_w_out, ffn_w_gate, ffn_w_up, ffn_conv_w, ffn_w_down, final_norm_g):
    P = {'ada_w': ada_w, 'ada_b': ada_b, 'norm_mix_g': norm_mix_g, 'norm_ffn_g': norm_ffn_g,
         'ssd_w_in': ssd_w_in, 'ssd_conv_w': ssd_conv_w, 'ssd_conv_b': ssd_conv_b, 'ssd_dt_bias': ssd_dt_bias,
         'ssd_a_log': ssd_a_log, 'ssd_d': ssd_d, 'ssd_norm_g': ssd_norm_g, 'sc_conv_w': sc_conv_w,
         'mix0_w_out': mix0_w_out, 'na_w_qkv': na_w_qkv, 'na_rpb': na_rpb, 'na_w_out': na_w_out,
         'ffn_w_gate': ffn_w_gate, 'ffn_w_up': ffn_w_up, 'ffn_conv_w': ffn_conv_w, 'ffn_w_down': ffn_w_down,
         'final_norm_g': final_norm_g}
    y_prompt, ssd_list, k_list, v_list = run_stream(x_prompt, c_ctx[None, :], P, True, None, None, None)
    new_state_ssd = jnp.stack(ssd_list, axis=1)
    new_cache_k = jnp.stack(k_list, axis=1)
    new_cache_v = jnp.stack(v_list, axis=1)
    y_sample, _, _, _ = run_stream(x_sample, c, P, False, state_ssd, cache_k, cache_v)
    return (y_prompt, y_sample, new_state_ssd, new_cache_k, new_cache_v)
```

```python
import functools

import jax
import jax.numpy as jnp
import numpy as np
from jax import lax
from jax.experimental import pallas as pl
from jax.experimental.pallas import tpu as pltpu

F32 = jnp.float32
BF16 = jnp.bfloat16

D_MODEL = 1024
EPS = 1e-6
N_MOD = 6
GRID_W = 64
SSD_HEADDIM = 64
SSD_HEADS = 16
SSD_STATE = 128
SSD_GROUPS = 2
SSD_INNER = 1024
SSD_XBC = 1536
SSD_LC = 256
SC_WIDTH = 1024
NA_HEADS = 16
NA_HEAD_DIM = 64
NA_KH = 8
NA_KW = 16
D_FF = 2816

LANES = 128
HEAD_PAIRS = 8
U_COLS = 5 * 1024 + 512

VMEM_LIMIT = 56 * 1024 * 1024


def _cparams(sem):
    return pltpu.CompilerParams(dimension_semantics=sem, vmem_limit_bytes=VMEM_LIMIT)


def _silu(x):
    return x * jax.nn.sigmoid(x)


def _dot(a, b):
    return jnp.dot(a, b, preferred_element_type=F32)


def _dot_nt(a, b):
    return lax.dot_general(a, b, (((1,), (1,)), ((), ())), preferred_element_type=F32)


def _split3(v):
    hi = v.astype(BF16)
    r1 = v - hi.astype(F32)
    mid = r1.astype(BF16)
    lo = (r1 - mid.astype(F32)).astype(BF16)
    return hi, mid, lo


def _sel_dot_left(sel, v):
    hi, mid, lo = _split3(v)
    return _dot(sel, hi) + _dot(sel, mid) + _dot(sel, lo)


def _sel_dot_right(v, sel):
    hi, mid, lo = _split3(v)
    return _dot(hi, sel) + _dot(mid, sel) + _dot(lo, sel)


def _norm_mod(x, g, shift, scale):
    ms = jnp.mean(x * x, axis=-1, keepdims=True)
    y = x * lax.rsqrt(ms + EPS) * g
    return y * (1.0 + scale) + shift


def _mod_row(i, tm, row_base, rows_per_group):
    return row_base + (i * tm) // rows_per_group


def _ada_kernel(c_ref, w_ref, b_ref, o_ref):
    s = _silu(c_ref[...]).astype(BF16)
    o_ref[0] = _dot(s, w_ref[0].astype(BF16)) + b_ref[0]


def _ada(cvec8, ada_w, ada_b):
    depth, d, n = ada_w.shape
    tn = 1536
    return pl.pallas_call(
        _ada_kernel,
        grid=(depth, n // tn),
        in_specs=[
            pl.BlockSpec((8, d), lambda l, j: (0, 0)),
            pl.BlockSpec((1, d, tn), lambda l, j: (l, 0, j)),
            pl.BlockSpec((1, 1, tn), lambda l, j: (l, 0, j)),
        ],
        out_specs=pl.BlockSpec((1, 8, tn), lambda l, j: (l, 0, j)),
        out_shape=jax.ShapeDtypeStruct((depth, 8, n), F32),
        compiler_params=_cparams(("arbitrary", "arbitrary")),
        name="ada_mod",
    )(cvec8, ada_w, ada_b.reshape(depth, 1, n))


def _nm_matmul_kernel(*refs, n_groups, has_extra, tm, row_base, rows_per_group):
    x_ref, sh_ref, sc_ref, g_ref = refs[:4]
    w_refs = refs[4:4 + n_groups]
    pos = 4 + n_groups
    if has_extra:
        we_ref = refs[pos]
        pos += 1
    o_refs = refs[pos:pos + n_groups]
    pos += n_groups
    if has_extra:
        oe_ref = refs[pos]
        pos += 1
    h_ref = refs[pos]
    i, j = pl.program_id(0), pl.program_id(1)

    @pl.when(j == 0)
    def _():
        r = _mod_row(i, tm, row_base, rows_per_group)
        sh = sh_ref[0, pl.ds(r, 1), :]
        sc = sc_ref[0, pl.ds(r, 1), :]
        h = _norm_mod(x_ref[...], g_ref[...], sh, sc).astype(BF16)
        h_ref[...] = h
        if has_extra:
            oe_ref[...] = _dot(h, we_ref[...])

    h = h_ref[...]
    for w_ref, o_ref in zip(w_refs, o_refs):
        o_ref[...] = _dot(h, w_ref[...])


def _nm_matmul(x, mod, layer, shift_idx, scale_idx, g, w, *, n_groups, group_width, tn, row_base, rows_per_group,
               w_extra=None, tm=1024):
    m, d = x.shape
    nj = group_width // tn
    has_extra = w_extra is not None
    in_specs = [
        pl.BlockSpec((tm, d), lambda i, j: (i, 0)),
        pl.BlockSpec((1, 8, d), lambda i, j: (layer, 0, shift_idx)),
        pl.BlockSpec((1, 8, d), lambda i, j: (layer, 0, scale_idx)),
        pl.BlockSpec((1, d), lambda i, j: (0, 0)),
    ]
    args = [x, mod, mod, g.reshape(1, d)]
    for gi in range(n_groups):
        in_specs.append(pl.BlockSpec((d, tn), functools.partial(lambda i, j, gi: (0, gi * nj + j), gi=gi)))
        args.append(w)
    out_specs = [pl.BlockSpec((tm, tn), lambda i, j: (i, j)) for _ in range(n_groups)]
    out_shape = [jax.ShapeDtypeStruct((m, group_width), F32) for _ in range(n_groups)]
    if has_extra:
        ne = w_extra.shape[1]
        in_specs.append(pl.BlockSpec((d, ne), lambda i, j: (0, 0)))
        args.append(w_extra)
        out_specs.append(pl.BlockSpec((tm, ne), lambda i, j: (i, 0)))
        out_shape.append(jax.ShapeDtypeStruct((m, ne), F32))
    kern = functools.partial(_nm_matmul_kernel, n_groups=n_groups, has_extra=has_extra, tm=tm, row_base=row_base,
                             rows_per_group=rows_per_group)
    return pl.pallas_call(
        kern,
        grid=(m // tm, nj),
        in_specs=in_specs,
        out_specs=out_specs,
        out_shape=out_shape,
        scratch_shapes=[pltpu.VMEM((tm, d), BF16)],
        compiler_params=_cparams(("arbitrary", "arbitrary")),
        name="norm_mod_matmul",
    )(*args)


def _proj_res_kernel(*refs, n_parts, tm, row_base, rows_per_group):
    x_ref, gate_ref = refs[:2]
    a_refs = refs[2:2 + n_parts]
    w_refs = refs[2 + n_parts:2 + 2 * n_parts]
    o_ref = refs[2 + 2 * n_parts]
    i = pl.program_id(0)
    r = _mod_row(i, tm, row_base, rows_per_group)
    acc = _dot(a_refs[0][...], w_refs[0][...])
    for a_ref, w_ref in zip(a_refs[1:], w_refs[1:]):
        acc = acc + _dot(a_ref[...], w_ref[...])
    o_ref[...] = x_ref[...] + gate_ref[0, pl.ds(r, 1), :] * acc


def _proj_res(x, mod, layer, gate_idx, parts, w, *, row_base, rows_per_group, tm=1024):
    m, d = x.shape
    n_parts = len(parts)
    kp = w.shape[0] // n_parts
    in_specs = [
        pl.BlockSpec((tm, d), lambda i: (i, 0)),
        pl.BlockSpec((1, 8, d), lambda i: (layer, 0, gate_idx)),
    ]
    in_specs += [pl.BlockSpec((tm, kp), lambda i: (i, 0)) for _ in parts]
    in_specs += [pl.BlockSpec((kp, d), functools.partial(lambda i, k: (k, 0), k=k)) for k in range(n_parts)]
    kern = functools.partial(_proj_res_kernel, n_parts=n_parts, tm=tm, row_base=row_base, rows_per_group=rows_per_group)
    return pl.pallas_call(
        kern,
        grid=(m // tm,),
        in_specs=in_specs,
        out_specs=pl.BlockSpec((tm, d), lambda i: (i, 0)),
        out_shape=jax.ShapeDtypeStruct((m, d), F32),
        compiler_params=_cparams(("arbitrary",)),
        name="proj_residual",
    )(x, mod, *parts, *([w] * n_parts))


def _dwconv3(v, w, seq_len):
    rows = v.shape[0]
    pos = lax.broadcasted_iota(jnp.int32, v.shape, 0) % seq_len
    prev = jnp.where(pos == 0, 0.0, pltpu.roll(v, 1, 0))
    nxt = jnp.where(pos == seq_len - 1, 0.0, pltpu.roll(v, rows - 1, 0))
    return prev * w[0:1, :] + v * w[1:2, :] + nxt * w[2:3, :]


def _ffn_kernel(x_ref, sh_ref, sc_ref, gate_ref, g_ref, wg_ref, wu_ref, cw_ref, wd_ref, fg_ref, o_ref, h_ref, acc_ref,
                *, tm, row_base, rows_per_group, seq_len, final_norm):
    i, f = pl.program_id(0), pl.program_id(1)
    r = _mod_row(i, tm, row_base, rows_per_group)

    @pl.when(f == 0)
    def _():
        sh = sh_ref[0, pl.ds(r, 1), :]
        sc = sc_ref[0, pl.ds(r, 1), :]
        h_ref[...] = _norm_mod(x_ref[...], g_ref[...], sh, sc).astype(BF16)
        acc_ref[...] = jnp.zeros_like(acc_ref)

    h = h_ref[...]
    gpre = _dwconv3(_dot(h, wg_ref[...]), cw_ref[...], seq_len)
    up = _dot(h, wu_ref[...])
    act = (jax.nn.gelu(gpre, approximate=True) * up).astype(BF16)
    acc_ref[...] += _dot(act, wd_ref[...])

    @pl.when(f == pl.num_programs(1) - 1)
    def _():
        y = x_ref[...] + gate_ref[0, pl.ds(r, 1), :] * acc_ref[...]
        if final_norm:
            ms = jnp.mean(y * y, axis=-1, keepdims=True)
            y = y * lax.rsqrt(ms + EPS) * fg_ref[...]
        o_ref[...] = y


def _ffn(x, mod, layer, g, wg, wu, cw, wd, final_g, *, row_base, rows_per_group, seq_len, final_norm, tm=1024, tf=256):
    m, d = x.shape
    dff = wg.shape[1]
    kern = functools.partial(_ffn_kernel, tm=tm, row_base=row_base, rows_per_group=rows_per_group, seq_len=seq_len,
                             final_norm=final_norm)
    return pl.pallas_call(
        kern,
        grid=(m // tm, dff // tf),
        in_specs=[
            pl.BlockSpec((tm, d), lambda i, f: (i, 0)),
            pl.BlockSpec((1, 8, d), lambda i, f: (layer, 0, 3)),
            pl.BlockSpec((1, 8, d), lambda i, f: (layer, 0, 4)),
            pl.BlockSpec((1, 8, d), lambda i, f: (layer, 0, 5)),
            pl.BlockSpec((1, d), lambda i, f: (0, 0)),
            pl.BlockSpec((d, tf), lambda i, f: (0, f)),
            pl.BlockSpec((d, tf), lambda i, f: (0, f)),
            pl.BlockSpec((3, tf), lambda i, f: (0, f)),
            pl.BlockSpec((tf, d), lambda i, f: (f, 0)),
            pl.BlockSpec((1, d), lambda i, f: (0, 0)),
        ],
        out_specs=pl.BlockSpec((tm, d), lambda i, f: (i, 0)),
        out_shape=jax.ShapeDtypeStruct((m, d), F32),
        scratch_shapes=[pltpu.VMEM((tm, d), BF16), pltpu.VMEM((tm, d), F32)],
        compiler_params=_cparams(("arbitrary", "arbitrary")),
        name="conv_ffn",
    )(x, mod, mod, mod, g.reshape(1, d), wg, wu, cw, wd, final_g.reshape(1, d))


def _short_conv_kernel(bg_ref, cg_ref, xin_ref, w_ref, o_ref, *, seq_len):
    o_ref[...] = (bg_ref[...] * _dwconv3(cg_ref[...] * xin_ref[...], w_ref[...], seq_len)).astype(BF16)


def _short_conv(u, w, *, seq_len, tm=1024, tc=512):
    m = u.shape[0]
    nc = SC_WIDTH // tc
    kern = functools.partial(_short_conv_kernel, seq_len=seq_len)
    return pl.pallas_call(
        kern,
        grid=(m // tm, nc),
        in_specs=[
            pl.BlockSpec((tm, tc), lambda i, j: (i, 2 * nc + j)),
            pl.BlockSpec((tm, tc), lambda i, j: (i, 3 * nc + j)),
            pl.BlockSpec((tm, tc), lambda i, j: (i, 4 * nc + j)),
            pl.BlockSpec((3, tc), lambda i, j: (0, j)),
        ],
        out_specs=pl.BlockSpec((tm, tc), lambda i, j: (i, j)),
        out_shape=jax.ShapeDtypeStruct((m, SC_WIDTH), BF16),
        compiler_params=_cparams(("arbitrary", "arbitrary")),
        name="short_conv",
    )(u, u, u, w)


def _ssd_kernel(*refs, seq_len, has_h0, emit_state):
    z_ref, xs_ref, bc_ref, dt_ref, cw_ref, cb_ref, prm_ref, ng_ref = refs[:8]
    pos = 8
    if has_h0:
        h0_ref = refs[pos]
        pos += 1
    ya_ref = refs[pos]
    pos += 1
    if emit_state:
        st_ref = refs[pos]
        pos += 1
    yacc_ref, cbm_ref, bt_ref, cm_ref, acs_ref, acst_ref, dtv_ref, dtt_ref = refs[pos:pos + 8]

    lc = SSD_LC
    nchunk = seq_len // lc
    p = pl.program_id(1)
    lane = lax.broadcasted_iota(jnp.int32, (1, LANES), 1)
    first_half = lane < SSD_HEADDIM

    @pl.when(p == 0)
    def _():
        bc = _silu(_dwconv3(bc_ref[0], cw_ref[:, SSD_INNER:SSD_XBC], seq_len) + cb_ref[:, SSD_INNER:SSD_XBC])
        valid = lane < 2 * SSD_HEADS
        dt = jnp.where(valid, jax.nn.softplus(dt_ref[0] + prm_ref[0:1, :]), 0.0)
        a = -jnp.exp(prm_ref[1:2, :])
        dta = jnp.where(valid, dt * a, 0.0)
        dtv_ref[...] = dt
        tt = lax.broadcasted_iota(jnp.int32, (lc, lc), 0)
        ss = lax.broadcasted_iota(jnp.int32, (lc, lc), 1)
        tril = jnp.where(ss <= tt, 1.0, 0.0).astype(BF16)
        triu = jnp.where(ss >= tt, 1.0, 0.0).astype(BF16)
        for c in range(nchunk):
            rows = slice(c * lc, (c + 1) * lc)
            acs = jnp.where(lane < SSD_HEADS, _sel_dot_left(tril, dta[rows]), _sel_dot_left(triu, dta[rows]))
            acs_ref[c] = acs
            acst_ref[c] = acs.T
            dtt_ref[c] = dt[rows].T
            for g in range(SSD_GROUPS):
                bm = bc[rows, g * SSD_STATE:(g + 1) * SSD_STATE]
                cm = bc[rows, (SSD_GROUPS + g) * SSD_STATE:(SSD_GROUPS + g + 1) * SSD_STATE].astype(BF16)
                bt = bm.T.astype(BF16)
                bt_ref[c, g] = bt
                cm_ref[c, g] = cm
                cbm_ref[c, g] = _dot(cm, bt)

    col = pl.multiple_of(p * LANES, LANES)
    g = p // (HEAD_PAIRS // SSD_GROUPS)
    x2 = _silu(_dwconv3(xs_ref[0], cw_ref[:, pl.ds(col, LANES)], seq_len) + cb_ref[:, pl.ds(col, LANES)])
    kk = lax.broadcasted_iota(jnp.int32, (LANES, LANES), 0)
    nn = lax.broadcasted_iota(jnp.int32, (LANES, LANES), 1)
    head_of_lane = 2 * p + jnp.where(nn >= SSD_HEADDIM, 1, 0)
    ef = jnp.where(kk == head_of_lane, 1.0, 0.0).astype(BF16)
    eb = jnp.where(kk == head_of_lane + SSD_HEADS, 1.0, 0.0).astype(BF16)
    dt = dtv_ref[...]
    dt2f = _sel_dot_right(dt, ef)
    dt2b = _sel_dot_right(dt, eb)
    dsum = prm_ref[2:3, :] + prm_ref[3:4, :]
    dsum2 = _sel_dot_right(jnp.broadcast_to(dsum, (8, LANES)), ef)[0:1, :]

    tt = lax.broadcasted_iota(jnp.int32, (lc, lc), 0)
    ss = lax.broadcasted_iota(jnp.int32, (lc, lc), 1)
    lower = tt > ss
    upper = tt < ss

    if has_h0:
        hf = h0_ref[0, 0, 0].T
        hb = h0_ref[0, 1, 0].T
    else:
        hf = jnp.zeros((SSD_STATE, LANES), F32)
        hb = jnp.zeros((SSD_STATE, LANES), F32)

    ys = []
    pend = []
    for c in range(nchunk):
        rows = slice(c * lc, (c + 1) * lc)
        x2c = x2[rows]
        x2cb = x2c.astype(BF16)
        acs = acs_ref[c]
        acsf2 = _sel_dot_right(acs, ef)
        acsb2 = _sel_dot_right(acs, eb)
        cb = cbm_ref[c, g]
        outs = []
        for hh in range(2):
            h = 2 * p + hh
            cf = acsf2[:, hh * SSD_HEADDIM:hh * SSD_HEADDIM + 1]
            cbk = acsb2[:, hh * SSD_HEADDIM:hh * SSD_HEADDIM + 1]
            rf = acst_ref[c, pl.ds(h, 1), :]
            rb = acst_ref[c, pl.ds(h + SSD_HEADS, 1), :]
            df = dtt_ref[c, pl.ds(h, 1), :]
            db = dtt_ref[c, pl.ds(h + SSD_HEADS, 1), :]
            arg = jnp.where(lower, cf - rf, jnp.where(upper, cbk - rb, 0.0))
            wgt = jnp.where(lower, df, jnp.where(upper, db, df + db))
            gm = (cb * jnp.exp(arg) * wgt).astype(BF16)
            outs.append(_dot(gm, x2cb))
        y = jnp.where(first_half, outs[0], outs[1]) + x2c * dsum2
        last_f = acsf2[lc - 1:lc, :]
        first_b = acsb2[0:1, :]
        xdwf = (x2c * dt2f[rows] * jnp.exp(last_f - acsf2)).astype(BF16)
        xdwb = (x2c * dt2b[rows] * jnp.exp(first_b - acsb2)).astype(BF16)
        bt = bt_ref[c, g]
        stf = _dot(bt, xdwf)
        stb = _dot(bt, xdwb)
        if has_h0 or c > 0:
            y = y + _dot(cm_ref[c, g], hf.astype(BF16)) * jnp.exp(acsf2)
        hf = hf * jnp.exp(last_f) + stf
        ys.append(y)
        pend.append((stb, jnp.exp(first_b), acsb2))
    for c in reversed(range(nchunk)):
        stb, decay_b, acsb2 = pend[c]
        if has_h0 or c < nchunk - 1:
            ys[c] = ys[c] + _dot(cm_ref[c, g], hb.astype(BF16)) * jnp.exp(acsb2)
        hb = hb * decay_b + stb

    zg = _silu(z_ref[0])
    for c in range(nchunk):
        yacc_ref[c * lc:(c + 1) * lc, pl.ds(col, LANES)] = ys[c] * zg[c * lc:(c + 1) * lc]
    if emit_state:
        st_ref[0, 0, 0] = hf.T
        st_ref[0, 1, 0] = hb.T

    @pl.when(p == HEAD_PAIRS - 1)
    def _():
        y = yacc_ref[...]
        ms = jnp.mean(y * y, axis=-1, keepdims=True)
        ya_ref[0] = (y * lax.rsqrt(ms + EPS) * ng_ref[...]).astype(BF16)


def _ssd(u3, dt3, conv_w, conv_b, prm, norm_g, h0, *, emit_state):
    b, s, _ = u3.shape
    nchunk = s // SSD_LC
    has_h0 = h0 is not None
    in_specs = [
        pl.BlockSpec((1, s, LANES), lambda i, p: (i, 0, p)),
        pl.BlockSpec((1, s, LANES), lambda i, p: (i, 0, HEAD_PAIRS + p)),
        pl.BlockSpec((1, s, 512), lambda i, p: (i, 0, 10)),
        pl.BlockSpec((1, s, LANES), lambda i, p: (i, 0, 0)),
        pl.BlockSpec((3, SSD_XBC), lambda i, p: (0, 0)),
        pl.BlockSpec((1, SSD_XBC), lambda i, p: (0, 0)),
        pl.BlockSpec((8, LANES), lambda i, p: (0, 0)),
        pl.BlockSpec((1, SSD_INNER), lambda i, p: (0, 0)),
    ]
    args = [u3, u3, u3, dt3, conv_w, conv_b, prm, norm_g]
    if has_h0:
        in_specs.append(pl.BlockSpec((1, 2, 1, LANES, SSD_STATE), lambda i, p: (i, 0, p, 0, 0)))
        args.append(h0)
    out_specs = [pl.BlockSpec((1, s, SSD_INNER), lambda i, p: (i, 0, 0))]
    out_shape = [jax.ShapeDtypeStruct((b, s, SSD_INNER), BF16)]
    if emit_state:
        out_specs.append(pl.BlockSpec((1, 2, 1, LANES, SSD_STATE), lambda i, p: (i, 0, p, 0, 0)))
        out_shape.append(jax.ShapeDtypeStruct((b, 2, HEAD_PAIRS, LANES, SSD_STATE), F32))
    kern = functools.partial(_ssd_kernel, seq_len=s, has_h0=has_h0, emit_state=emit_state)
    return pl.pallas_call(
        kern,
        grid=(b, HEAD_PAIRS),
        in_specs=in_specs,
        out_specs=out_specs,
        out_shape=out_shape,
        scratch_shapes=[
            pltpu.VMEM((s, SSD_INNER), F32),
            pltpu.VMEM((nchunk, SSD_GROUPS, SSD_LC, SSD_LC), F32),
            pltpu.VMEM((nchunk, SSD_GROUPS, SSD_STATE, SSD_LC), BF16),
            pltpu.VMEM((nchunk, SSD_GROUPS, SSD_LC, SSD_STATE), BF16),
            pltpu.VMEM((nchunk, SSD_LC, LANES), F32),
            pltpu.VMEM((nchunk, LANES, SSD_LC), F32),
            pltpu.VMEM((s, LANES), F32),
            pltpu.VMEM((nchunk, LANES, SSD_LC), F32),
        ],
        compiler_params=_cparams(("arbitrary", "arbitrary")),
        name="ssd_pair",
    )(*args)


def _pair_masks():
    lane = lax.broadcasted_iota(jnp.int32, (1, LANES), 1)
    return lane < NA_HEAD_DIM


def _ctx_attn_kernel(q_ref, k_ref, v_ref, o_ref):
    first_half = _pair_masks()
    s = q_ref.shape[0]
    scale = NA_HEAD_DIM ** -0.5
    for p in range(HEAD_PAIRS):
        cols = slice(p * LANES, (p + 1) * LANES)
        q2 = q_ref[:, cols] * scale
        kb = k_ref[:, cols].astype(BF16)
        vb = v_ref[:, cols].astype(BF16)
        qs = jnp.concatenate([jnp.where(first_half, q2, 0.0), jnp.where(first_half, 0.0, q2)], axis=0).astype(BF16)
        sc = _dot_nt(qs, kb)
        e = jnp.exp(sc - jnp.max(sc, axis=-1, keepdims=True))
        pv = _dot(e.astype(BF16), vb) / jnp.sum(e, axis=-1, keepdims=True)
        o_ref[:, cols] = jnp.where(first_half, pv[:s], pv[s:]).astype(BF16)


def _ctx_attn(q, k, v, *, seq_len):
    m, d = q.shape
    spec = pl.BlockSpec((seq_len, d), lambda i: (i, 0))
    return pl.pallas_call(
        _ctx_attn_kernel,
        grid=(m // seq_len,),
        in_specs=[spec, spec, spec],
        out_specs=spec,
        out_shape=jax.ShapeDtypeStruct((m, d), BF16),
        compiler_params=_cparams(("arbitrary",)),
        name="ctx_attention",
    )(q, k, v)


def _lat_attn_kernel(q_ref, k_ref, v_ref, kc_ref, vc_ref, rpb_ref, o_ref, bias_ref, *, rows, q_tile):
    b = pl.program_id(1)
    first_half = _pair_masks()
    n_tok = rows * GRID_W
    neg_inf = -jnp.inf

    @pl.when(b == 0)
    def _():
        qc = lax.broadcasted_iota(jnp.int32, (GRID_W, LANES), 0)
        kc = lax.broadcasted_iota(jnp.int32, (GRID_W, LANES), 1)
        cs = jnp.clip(qc - NA_KW // 2, 0, GRID_W - NA_KW)
        col_ok = (kc >= cs) & (kc < cs + NA_KW)
        kh = min(NA_KH, rows)
        for hh in range(2):
            bias_ref[hh] = jnp.full((n_tok, n_tok), neg_inf, F32)
            for dr in range(2 * NA_KH - 1):
                v = jnp.broadcast_to(rpb_ref[hh, dr:dr + 1, :], (GRID_W, LANES))
                t = pltpu.roll(v, LANES - (NA_KW - 1), 1, stride=1, stride_axis=0)
                tile = jnp.where(col_ok, t, neg_inf)[:, :GRID_W]
                for r in range(rows):
                    start = min(max(r - kh // 2, 0), rows - kh)
                    i = dr - (NA_KH - 1) + r - start
                    if 0 <= i < kh:
                        kr = start + i
                        bias_ref[hh, r * GRID_W:(r + 1) * GRID_W, kr * GRID_W:(kr + 1) * GRID_W] = tile

    scale = NA_HEAD_DIM ** -0.5
    kb = k_ref[...].astype(BF16)
    vb = v_ref[...].astype(BF16)
    kcb = kc_ref[...].astype(BF16)
    vcb = vc_ref[...].astype(BF16)
    for qt in range(n_tok // q_tile):
        qrows = slice(qt * q_tile, (qt + 1) * q_tile)
        q2 = q_ref[qrows, :] * scale
        outs = []
        for hh in range(2):
            qm = (jnp.where(first_half, q2, 0.0) if hh == 0 else jnp.where(first_half, 0.0, q2)).astype(BF16)
            s_loc = _dot_nt(qm, kb) + bias_ref[hh, qrows, :]
            s_ctx = _dot_nt(qm, kcb)
            mx = jnp.maximum(jnp.max(s_loc, axis=-1, keepdims=True), jnp.max(s_ctx, axis=-1, keepdims=True))
            e_loc = jnp.exp(s_loc - mx)
            e_ctx = jnp.exp(s_ctx - mx)
            den = jnp.sum(e_loc, axis=-1, keepdims=True) + jnp.sum(e_ctx, axis=-1, keepdims=True)
            outs.append((_dot(e_loc.astype(BF16), vb) + _dot(e_ctx.astype(BF16), vcb)) / den)
        o_ref[qrows, :] = jnp.where(first_half, outs[0], outs[1]).astype(BF16)


def _lat_attn(q, k, v, kc, vc, rpb_pad, *, n_batch, n_tok, n_ctx, q_tile=256):
    d = q.shape[1]
    rows = n_tok // GRID_W
    tok_spec = pl.BlockSpec((n_tok, LANES), lambda p, b: (b, p))
    ctx_spec = pl.BlockSpec((n_ctx, LANES), lambda p, b: (b, p))
    kern = functools.partial(_lat_attn_kernel, rows=rows, q_tile=q_tile)
    return pl.pallas_call(
        kern,
        grid=(HEAD_PAIRS, n_batch),
        in_specs=[tok_spec, tok_spec, tok_spec, ctx_spec, ctx_spec,
                  pl.BlockSpec((2, 2 * NA_KH, LANES), lambda p, b: (p, 0, 0))],
        out_specs=tok_spec,
        out_shape=jax.ShapeDtypeStruct((n_batch * n_tok, d), BF16),
        scratch_shapes=[pltpu.VMEM((2, n_tok, n_tok), F32)],
        compiler_params=_cparams(("arbitrary", "arbitrary")),
        name="latent_attention",
    )(q, k, v, kc, vc, rpb_pad)


def _permute_w_in(w_in):
    s1 = SSD_INNER
    s2 = s1 + SSD_XBC
    s3 = s2 + 2 * SSD_HEADS
    z = w_in[:, :s1]
    xs = w_in[:, s1:s1 + SSD_INNER]
    bc = w_in[:, s1 + SSD_INNER:s2]
    dt = w_in[:, s2:s3]
    rest = w_in[:, s3:]
    main = jnp.concatenate([z, xs, rest, bc], axis=1).astype(BF16)
    dtw = jnp.pad(dt, ((0, 0), (0, LANES - 2 * SSD_HEADS))).astype(BF16)
    return main, dtw


def _pad_lanes(v):
    return jnp.pad(v, (0, LANES - v.shape[0]))


def _run_stream(x3, mod, row_base, W, *, is_ctx, state_ssd=None, cache_k=None, cache_v=None):
    b, s, d = x3.shape
    m = b * s
    x = x3.reshape(m, d)
    rpg = m if is_ctx else s
    kw = dict(row_base=row_base, rows_per_group=rpg)

    u, dtr = _nm_matmul(x, mod, 0, 0, 1, W['norm_mix_g'][0], W['w_in'], n_groups=1, group_width=U_COLS, tn=512,
                        w_extra=W['w_dt'], **kw)
    h0 = None
    if not is_ctx:
        h0 = state_ssd[:, 0].reshape(b, 2, HEAD_PAIRS, LANES, SSD_STATE)
    res = _ssd(u.reshape(b, s, U_COLS), dtr.reshape(b, s, LANES), W['ssd_conv_w'], W['ssd_conv_b'], W['ssd_prm'],
               W['ssd_norm_g'], h0, emit_state=is_ctx)
    ya = res[0].reshape(m, SSD_INNER)
    yb = _short_conv(u, W['sc_conv_w'], seq_len=s)
    x = _proj_res(x, mod, 0, 2, [ya, yb], W['mix0_w_out'], **kw)
    x = _ffn(x, mod, 0, W['norm_ffn_g'][0], W['ffn_w_gate'][0], W['ffn_w_up'][0], W['ffn_conv_w'][0], W['ffn_w_down'][0],
             W['final_norm_g'], seq_len=s, final_norm=False, **kw)

    q, k, v = _nm_matmul(x, mod, 1, 0, 1, W['norm_mix_g'][1], W['na_w_qkv'], n_groups=3, group_width=D_MODEL, tn=512, **kw)
    if is_ctx:
        o = _ctx_attn(q, k, v, seq_len=s)
    else:
        n_ctx = cache_k.shape[2]
        kc = cache_k[:, 0].reshape(b * n_ctx, d)
        vc = cache_v[:, 0].reshape(b * n_ctx, d)
        o = _lat_attn(q, k, v, kc, vc, W['rpb_pad'], n_batch=b, n_tok=s, n_ctx=n_ctx)
    x = _proj_res(x, mod, 1, 2, [o], W['na_w_out'], **kw)
    x = _ffn(x, mod, 1, W['norm_ffn_g'][1], W['ffn_w_gate'][1], W['ffn_w_up'][1], W['ffn_conv_w'][1], W['ffn_w_down'][1],
             W['final_norm_g'], seq_len=s, final_norm=True, **kw)
    y = x.reshape(b, s, d)
    if is_ctx:
        new_state = res[1].reshape(b, 1, 2, SSD_HEADS, SSD_HEADDIM, SSD_STATE)
        new_k = k.reshape(b, 1, s, NA_HEADS, NA_HEAD_DIM)
        new_v = v.reshape(b, 1, s, NA_HEADS, NA_HEAD_DIM)
        return y, new_state, new_k, new_v
    return y


def kernel(x_prompt, x_sample, state_ssd, cache_k, cache_v, c, c_ctx, ada_w, ada_b, norm_mix_g, norm_ffn_g, ssd_w_in,
           ssd_conv_w, ssd_conv_b, ssd_dt_bias, ssd_a_log, ssd_d, ssd_norm_g, sc_conv_w, mix0_w_out, na_w_qkv, na_rpb,
           na_w_out, ffn_w_gate, ffn_w_up, ffn_conv_w, ffn_w_down, final_norm_g):
    n_lat = x_sample.shape[0]
    cvec = jnp.concatenate([c_ctx[None, :], c, jnp.zeros((8 - 1 - n_lat, D_MODEL), F32)], axis=0)
    mod = _ada(cvec, ada_w, ada_b)

    w_in, w_dt = _permute_w_in(ssd_w_in[0])
    prm = jnp.stack([_pad_lanes(ssd_dt_bias[0].reshape(-1)), _pad_lanes(ssd_a_log[0].reshape(-1)),
                     _pad_lanes(ssd_d[0, 0]), _pad_lanes(ssd_d[0, 1])] + [jnp.zeros((LANES,), F32)] * 4, axis=0)
    rpb = na_rpb[0]
    rpb_pad = jnp.pad(rpb, ((0, 0), (0, 1), (0, LANES - rpb.shape[2])))
    W = {
        'norm_mix_g': norm_mix_g, 'norm_ffn_g': norm_ffn_g, 'final_norm_g': final_norm_g,
        'w_in': w_in, 'w_dt': w_dt,
        'ssd_conv_w': ssd_conv_w[0], 'ssd_conv_b': ssd_conv_b[0].reshape(1, SSD_XBC), 'ssd_prm': prm,
        'ssd_norm_g': ssd_norm_g[0].reshape(1, SSD_INNER), 'sc_conv_w': sc_conv_w[0],
        'mix0_w_out': mix0_w_out[0].astype(BF16),
        'na_w_qkv': na_w_qkv[0].astype(BF16), 'rpb_pad': rpb_pad, 'na_w_out': na_w_out[0].astype(BF16),
        'ffn_w_gate': ffn_w_gate.astype(BF16), 'ffn_w_up': ffn_w_up.astype(BF16), 'ffn_conv_w': ffn_conv_w,
        'ffn_w_down': ffn_w_down.astype(BF16),
    }
    y_prompt, new_state, new_k, new_v = _run_stream(x_prompt, mod, 0, W, is_ctx=True)
    y_sample = _run_stream(x_sample, mod, 1, W, is_ctx=False, state_ssd=state_ssd, cache_k=cache_k, cache_v=cache_v)
    return (y_prompt, y_sample, new_state, new_k, new_v)
```

```python
import functools

import jax
import jax.numpy as jnp
from jax import lax
from jax.experimental import pallas as pl
from jax.experimental.pallas import tpu as pltpu

F32 = jnp.float32
BF16 = jnp.bfloat16

D_MODEL = 1024
EPS = 1e-6
GRID_W = 64
SSD_HEADDIM = 64
SSD_HEADS = 16
SSD_STATE = 128
SSD_GROUPS = 2
SSD_INNER = 1024
SSD_XBC = 1536
SSD_LC = 256
SC_WIDTH = 1024
NA_HEADS = 16
NA_HEAD_DIM = 64
NA_KH = 8
NA_KW = 16

LANES = 128
SUBLANES = 8
HEAD_PAIRS = 8
U_COLS = 5 * 1024 + 512

VMEM_LIMIT = 56 * 1024 * 1024

_RESIDENT = pl.BlockSpec(memory_space=pltpu.VMEM)


def _cparams(sem):
    return pltpu.CompilerParams(dimension_semantics=sem, vmem_limit_bytes=VMEM_LIMIT)


def _silu(x):
    return x * jax.nn.sigmoid(x)


def _dot(a, b):
    return jnp.dot(a, b, preferred_element_type=F32)


def _dot_nt(a, b):
    return lax.dot_general(a, b, (((1,), (1,)), ((), ())), preferred_element_type=F32)


def _split3(v):
    hi = v.astype(BF16)
    r1 = v - hi.astype(F32)
    mid = r1.astype(BF16)
    lo = (r1 - mid.astype(F32)).astype(BF16)
    return hi, mid, lo


def _sel_dot_left(sel, v):
    hi, mid, lo = _split3(v)
    return _dot(sel, hi) + _dot(sel, mid) + _dot(sel, lo)


def _norm_mod(x, g, shift, scale):
    ms = jnp.mean(x * x, axis=-1, keepdims=True)
    y = x * lax.rsqrt(ms + EPS) * g
    return y * (1.0 + scale) + shift


def _mod_row(i, tm, row_base, rows_per_group):
    return row_base + (i * tm) // rows_per_group


def _ada_kernel(c_ref, w_ref, b_ref, o_ref):
    s = _silu(c_ref[...]).astype(BF16)
    o_ref[0] = _dot(s, w_ref[0].astype(BF16)) + b_ref[0]


def _ada(cvec8, ada_w, ada_b):
    depth, d, n = ada_w.shape
    tn = 1536
    return pl.pallas_call(
        _ada_kernel,
        grid=(depth, n // tn),
        in_specs=[
            pl.BlockSpec((8, d), lambda l, j: (0, 0)),
            pl.BlockSpec((1, d, tn), lambda l, j: (l, 0, j)),
            pl.BlockSpec((1, 1, tn), lambda l, j: (l, 0, j)),
        ],
        out_specs=pl.BlockSpec((1, 8, tn), lambda l, j: (l, 0, j)),
        out_shape=jax.ShapeDtypeStruct((depth, 8, n), F32),
        compiler_params=_cparams(("arbitrary", "arbitrary")),
        name="ada_mod",
    )(cvec8, ada_w, ada_b.reshape(depth, 1, n))


def _nm_matmul_kernel(*refs, n_groups, group_width, has_extra, tm, tn, row_base, rows_per_group):
    x_ref, sh_ref, sc_ref, g_ref, w_ref = refs[:5]
    pos = 5
    if has_extra:
        we_ref = refs[pos]
        pos += 1
    o_refs = refs[pos:pos + n_groups]
    pos += n_groups
    if has_extra:
        oe_ref = refs[pos]
    r = _mod_row(pl.program_id(0), tm, row_base, rows_per_group)
    sh = sh_ref[0, pl.ds(r, 1), :]
    sc = sc_ref[0, pl.ds(r, 1), :]
    h = _norm_mod(x_ref[...], g_ref[...], sh, sc).astype(BF16)
    if has_extra:
        oe_ref[...] = _dot(h, we_ref[...])
    for gi, o_ref in enumerate(o_refs):
        for j in range(group_width // tn):
            o_ref[:, j * tn:(j + 1) * tn] = _dot(h, w_ref[:, gi * group_width + j * tn:gi * group_width + (j + 1) * tn])


def _nm_matmul(x, mod, layer, shift_idx, scale_idx, g, w, *, n_groups, group_width, row_base, rows_per_group,
               w_extra=None, tm=512, tn=512):
    m, d = x.shape
    has_extra = w_extra is not None
    in_specs = [
        pl.BlockSpec((tm, d), lambda i: (i, 0)),
        pl.BlockSpec((1, 8, d), lambda i: (layer, 0, shift_idx)),
        pl.BlockSpec((1, 8, d), lambda i: (layer, 0, scale_idx)),
        pl.BlockSpec((1, d), lambda i: (0, 0)),
        _RESIDENT,
    ]
    args = [x, mod, mod, g.reshape(1, d), w]
    out_specs = [pl.BlockSpec((tm, group_width), lambda i: (i, 0)) for _ in range(n_groups)]
    out_shape = [jax.ShapeDtypeStruct((m, group_width), F32) for _ in range(n_groups)]
    if has_extra:
        ne = w_extra.shape[1]
        in_specs.append(_RESIDENT)
        args.append(w_extra)
        out_specs.append(pl.BlockSpec((tm, ne), lambda i: (i, 0)))
        out_shape.append(jax.ShapeDtypeStruct((m, ne), F32))
    kern = functools.partial(_nm_matmul_kernel, n_groups=n_groups, group_width=group_width, has_extra=has_extra, tm=tm,
                             tn=tn, row_base=row_base, rows_per_group=rows_per_group)
    return pl.pallas_call(
        kern,
        grid=(m // tm,),
        in_specs=in_specs,
        out_specs=out_specs,
        out_shape=out_shape,
        compiler_params=_cparams(("arbitrary",)),
        name="norm_mod_matmul",
    )(*args)


def _proj_res_kernel(*refs, n_parts, kp, tm, row_base, rows_per_group):
    x_ref, gate_ref, w_ref = refs[:3]
    a_refs = refs[3:3 + n_parts]
    o_ref = refs[3 + n_parts]
    r = _mod_row(pl.program_id(0), tm, row_base, rows_per_group)
    acc = _dot(a_refs[0][...], w_ref[0:kp, :])
    for k in range(1, n_parts):
        acc = acc + _dot(a_refs[k][...], w_ref[k * kp:(k + 1) * kp, :])
    o_ref[...] = x_ref[...] + gate_ref[0, pl.ds(r, 1), :] * acc


def _proj_res(x, mod, layer, gate_idx, parts, w, *, row_base, rows_per_group, tm=1024):
    m, d = x.shape
    n_parts = len(parts)
    kp = w.shape[0] // n_parts
    in_specs = [
        pl.BlockSpec((tm, d), lambda i: (i, 0)),
        pl.BlockSpec((1, 8, d), lambda i: (layer, 0, gate_idx)),
        _RESIDENT,
    ]
    in_specs += [pl.BlockSpec((tm, kp), lambda i: (i, 0)) for _ in parts]
    kern = functools.partial(_proj_res_kernel, n_parts=n_parts, kp=kp, tm=tm, row_base=row_base,
                             rows_per_group=rows_per_group)
    return pl.pallas_call(
        kern,
        grid=(m // tm,),
        in_specs=in_specs,
        out_specs=pl.BlockSpec((tm, d), lambda i: (i, 0)),
        out_shape=jax.ShapeDtypeStruct((m, d), F32),
        compiler_params=_cparams(("arbitrary",)),
        name="proj_residual",
    )(x, mod, w, *parts)


def _dwconv3(v, w, seq_len, row0=0):
    rows = v.shape[0]
    pos = (lax.broadcasted_iota(jnp.int32, v.shape, 0) + row0) % seq_len
    prev = jnp.where(pos == 0, 0.0, pltpu.roll(v, 1, 0))
    nxt = jnp.where(pos == seq_len - 1, 0.0, pltpu.roll(v, rows - 1, 0))
    return prev * w[0:1, :] + v * w[1:2, :] + nxt * w[2:3, :]


def _ffn_kernel(x_ref, sh_ref, sc_ref, gate_ref, g_ref, wg_ref, wu_ref, cw_ref, wd_ref, fg_ref, o_ref, h_ref, act_ref,
                *, tm, tf, row_base, rows_per_group, seq_len, final_norm):
    r = _mod_row(pl.program_id(0), tm, row_base, rows_per_group)
    sh = sh_ref[0, pl.ds(r, 1), :]
    sc = sc_ref[0, pl.ds(r, 1), :]
    gate = gate_ref[0, pl.ds(r, 1), :]
    h_ref[...] = _norm_mod(x_ref[...], g_ref[...], sh, sc).astype(BF16)
    dff = wg_ref.shape[1]
    half = tm // 2
    win = half + SUBLANES
    for hb in range(2):
        w0 = hb * (half - SUBLANES)
        v0 = hb * SUBLANES
        rows = slice(hb * half, (hb + 1) * half)
        hw = h_ref[w0:w0 + win, :]
        hv = h_ref[rows, :]
        for fc in range(dff // tf):
            cols = slice(fc * tf, (fc + 1) * tf)
            gpre = _dwconv3(_dot(hw, wg_ref[:, cols]), cw_ref[:, cols], seq_len, row0=w0)[v0:v0 + half]
            up = _dot(hv, wu_ref[:, cols])
            act_ref[:, cols] = (jax.nn.gelu(gpre, approximate=True) * up).astype(BF16)
        y = x_ref[rows, :] + gate * _dot(act_ref[...], wd_ref[...])
        if final_norm:
            ms = jnp.mean(y * y, axis=-1, keepdims=True)
            y = y * lax.rsqrt(ms + EPS) * fg_ref[...]
        o_ref[rows, :] = y


def _ffn(x, mod, layer, g, wg, wu, cw, wd, final_g, *, row_base, rows_per_group, seq_len, final_norm, tm=1024, tf=256):
    m, d = x.shape
    dff = wg.shape[1]
    kern = functools.partial(_ffn_kernel, tm=tm, tf=tf, row_base=row_base, rows_per_group=rows_per_group,
                             seq_len=seq_len, final_norm=final_norm)
    return pl.pallas_call(
        kern,
        grid=(m // tm,),
        in_specs=[
            pl.BlockSpec((tm, d), lambda i: (i, 0)),
            pl.BlockSpec((1, 8, d), lambda i: (layer, 0, 3)),
            pl.BlockSpec((1, 8, d), lambda i: (layer, 0, 4)),
            pl.BlockSpec((1, 8, d), lambda i: (layer, 0, 5)),
            pl.BlockSpec((1, d), lambda i: (0, 0)),
            _RESIDENT, _RESIDENT, _RESIDENT, _RESIDENT,
            pl.BlockSpec((1, d), lambda i: (0, 0)),
        ],
        out_specs=pl.BlockSpec((tm, d), lambda i: (i, 0)),
        out_shape=jax.ShapeDtypeStruct((m, d), F32),
        scratch_shapes=[pltpu.VMEM((tm, d), BF16), pltpu.VMEM((tm // 2, dff), BF16)],
        compiler_params=_cparams(("arbitrary",)),
        name="conv_ffn",
    )(x, mod, mod, mod, g.reshape(1, d), wg, wu, cw, wd, final_g.reshape(1, d))


def _short_conv_kernel(bg_ref, cg_ref, xin_ref, w_ref, o_ref, *, seq_len):
    o_ref[...] = (bg_ref[...] * _dwconv3(cg_ref[...] * xin_ref[...], w_ref[...], seq_len)).astype(BF16)


def _short_conv(u, w, *, seq_len, tm=1024, tc=512):
    m = u.shape[0]
    nc = SC_WIDTH // tc
    kern = functools.partial(_short_conv_kernel, seq_len=seq_len)
    return pl.pallas_call(
        kern,
        grid=(m // tm, nc),
        in_specs=[
            pl.BlockSpec((tm, tc), lambda i, j: (i, 2 * nc + j)),
            pl.BlockSpec((tm, tc), lambda i, j: (i, 3 * nc + j)),
            pl.BlockSpec((tm, tc), lambda i, j: (i, 4 * nc + j)),
            pl.BlockSpec((3, tc), lambda i, j: (0, j)),
        ],
        out_specs=pl.BlockSpec((tm, tc), lambda i, j: (i, j)),
        out_shape=jax.ShapeDtypeStruct((m, SC_WIDTH), BF16),
        compiler_params=_cparams(("arbitrary", "arbitrary")),
        name="short_conv",
    )(u, u, u, w)


def _pair_cols(first_half, arr, h0, h1):
    return jnp.where(first_half, arr[:, h0:h0 + 1], arr[:, h1:h1 + 1])


def _ssd_kernel(*refs, seq_len, has_h0, emit_state):
    z_ref, xs_ref, bc_ref, dt_ref, cw_ref, cb_ref, prm_ref, ng_ref = refs[:8]
    pos = 8
    if has_h0:
        h0_ref = refs[pos]
        pos += 1
    ya_ref = refs[pos]
    pos += 1
    if emit_state:
        st_ref = refs[pos]
        pos += 1
    yacc_ref = refs[pos]

    lc = SSD_LC
    nchunk = seq_len // lc
    nh = SSD_HEADS
    lane = lax.broadcasted_iota(jnp.int32, (1, LANES), 1)
    first_half = lane < SSD_HEADDIM
    fwd_lane = lane < nh

    bc = _silu(_dwconv3(bc_ref[0], cw_ref[:, SSD_INNER:SSD_XBC], seq_len) + cb_ref[:, SSD_INNER:SSD_XBC])
    valid = lane < 2 * nh
    dt = jnp.where(valid, jax.nn.softplus(dt_ref[0] + prm_ref[0:1, :]), 0.0)
    a = -jnp.exp(prm_ref[1:2, :])
    dta = jnp.where(valid, dt * a, 0.0)
    dsum = prm_ref[2:3, :] + prm_ref[3:4, :]
    tt = lax.broadcasted_iota(jnp.int32, (lc, lc), 0)
    ss = lax.broadcasted_iota(jnp.int32, (lc, lc), 1)
    lower = tt > ss
    upper = tt < ss
    tril = jnp.where(ss <= tt, 1.0, 0.0).astype(BF16)
    triu = jnp.where(ss >= tt, 1.0, 0.0).astype(BF16)

    acs_l, acst_l, dtt_l, wts_l, eacs_l, edec_l, cbm_l, bt_l, cm_l = [], [], [], [], [], [], [], [], []
    for c in range(nchunk):
        rows = slice(c * lc, (c + 1) * lc)
        acs = jnp.where(fwd_lane, _sel_dot_left(tril, dta[rows]), _sel_dot_left(triu, dta[rows]))
        edge = jnp.where(fwd_lane, acs[lc - 1:lc, :], acs[0:1, :])
        acs_l.append(acs)
        acst_l.append(acs.T)
        dtt_l.append(dt[rows].T)
        wts_l.append(dt[rows] * jnp.exp(edge - acs))
        eacs_l.append(jnp.exp(acs))
        edec_l.append(jnp.exp(edge))
        cbm_g, bt_g, cm_g = [], [], []
        for g in range(SSD_GROUPS):
            bm = bc[rows, g * SSD_STATE:(g + 1) * SSD_STATE]
            cm = bc[rows, (SSD_GROUPS + g) * SSD_STATE:(SSD_GROUPS + g + 1) * SSD_STATE].astype(BF16)
            bt = bm.T.astype(BF16)
            bt_g.append(bt)
            cm_g.append(cm)
            cbm_g.append(_dot(cm, bt))
        cbm_l.append(cbm_g)
        bt_l.append(bt_g)
        cm_l.append(cm_g)

    for p in range(HEAD_PAIRS):
        cols = slice(p * LANES, (p + 1) * LANES)
        g = p // (HEAD_PAIRS // SSD_GROUPS)
        hd = (2 * p, 2 * p + 1)
        x2 = _silu(_dwconv3(xs_ref[0, :, cols], cw_ref[:, cols], seq_len) + cb_ref[:, cols])
        dsum2 = _pair_cols(first_half, dsum, hd[0], hd[1])
        if has_h0:
            hf = h0_ref[0, 0, p].T
            hb = h0_ref[0, 1, p].T
        else:
            hf = jnp.zeros((SSD_STATE, LANES), F32)
            hb = jnp.zeros((SSD_STATE, LANES), F32)
        ys = []
        stbs = []
        for c in range(nchunk):
            rows = slice(c * lc, (c + 1) * lc)
            x2c = x2[rows]
            x2cb = x2c.astype(BF16)
            acs, acst, dtt = acs_l[c], acst_l[c], dtt_l[c]
            outs = []
            for h in hd:
                cf = acs[:, h:h + 1]
                cbk = acs[:, nh + h:nh + h + 1]
                rf = acst[h:h + 1, :]
                rb = acst[nh + h:nh + h + 1, :]
                df = dtt[h:h + 1, :]
                db = dtt[nh + h:nh + h + 1, :]
                arg = jnp.where(lower, cf - rf, jnp.where(upper, cbk - rb, 0.0))
                wgt = jnp.where(lower, df, jnp.where(upper, db, df + db))
                gm = (cbm_l[c][g] * jnp.exp(arg) * wgt).astype(BF16)
                outs.append(_dot(gm, x2cb))
            y = jnp.where(first_half, outs[0], outs[1]) + x2c * dsum2
            xdwf = (x2c * _pair_cols(first_half, wts_l[c], hd[0], hd[1])).astype(BF16)
            xdwb = (x2c * _pair_cols(first_half, wts_l[c], nh + hd[0], nh + hd[1])).astype(BF16)
            stf = _dot(bt_l[c][g], xdwf)
            stbs.append(_dot(bt_l[c][g], xdwb))
            if has_h0 or c > 0:
                y = y + _dot(cm_l[c][g], hf.astype(BF16)) * _pair_cols(first_half, eacs_l[c], hd[0], hd[1])
            hf = hf * _pair_cols(first_half, edec_l[c], hd[0], hd[1]) + stf
            ys.append(y)
        for c in reversed(range(nchunk)):
            if has_h0 or c < nchunk - 1:
                ys[c] = ys[c] + (_dot(cm_l[c][g], hb.astype(BF16))
                                 * _pair_cols(first_half, eacs_l[c], nh + hd[0], nh + hd[1]))
            hb = hb * _pair_cols(first_half, edec_l[c], nh + hd[0], nh + hd[1]) + stbs[c]
        zg = _silu(z_ref[0, :, cols])
        for c in range(nchunk):
            rows = slice(c * lc, (c + 1) * lc)
            yacc_ref[rows, cols] = ys[c] * zg[rows]
        if emit_state:
            st_ref[0, 0, p] = hf.T
            st_ref[0, 1, p] = hb.T

    y = yacc_ref[...]
    ms = jnp.mean(y * y, axis=-1, keepdims=True)
    ya_ref[0] = (y * lax.rsqrt(ms + EPS) * ng_ref[...]).astype(BF16)


def _ssd(u3, dt3, conv_w, conv_b, prm, norm_g, h0, *, emit_state):
    b, s, _ = u3.shape
    has_h0 = h0 is not None
    in_specs = [
        pl.BlockSpec((1, s, SSD_INNER), lambda i: (i, 0, 0)),
        pl.BlockSpec((1, s, SSD_INNER), lambda i: (i, 0, 1)),
        pl.BlockSpec((1, s, 512), lambda i: (i, 0, 10)),
        pl.BlockSpec((1, s, LANES), lambda i: (i, 0, 0)),
        pl.BlockSpec((3, SSD_XBC), lambda i: (0, 0)),
        pl.BlockSpec((1, SSD_XBC), lambda i: (0, 0)),
        pl.BlockSpec((8, LANES), lambda i: (0, 0)),
        pl.BlockSpec((1, SSD_INNER), lambda i: (0, 0)),
    ]
    args = [u3, u3, u3, dt3, conv_w, conv_b, prm, norm_g]
    state_block = (1, 2, HEAD_PAIRS, LANES, SSD_STATE)
    if has_h0:
        in_specs.append(pl.BlockSpec(state_block, lambda i: (i, 0, 0, 0, 0)))
        args.append(h0)
    out_specs = [pl.BlockSpec((1, s, SSD_INNER), lambda i: (i, 0, 0))]
    out_shape = [jax.ShapeDtypeStruct((b, s, SSD_INNER), BF16)]
    if emit_state:
        out_specs.append(pl.BlockSpec(state_block, lambda i: (i, 0, 0, 0, 0)))
        out_shape.append(jax.ShapeDtypeStruct((b, 2, HEAD_PAIRS, LANES, SSD_STATE), F32))
    kern = functools.partial(_ssd_kernel, seq_len=s, has_h0=has_h0, emit_state=emit_state)
    return pl.pallas_call(
        kern,
        grid=(b,),
        in_specs=in_specs,
        out_specs=out_specs,
        out_shape=out_shape,
        scratch_shapes=[pltpu.VMEM((s, SSD_INNER), F32)],
        compiler_params=_cparams(("arbitrary",)),
        name="ssd",
    )(*args)


def _pair_masks():
    lane = lax.broadcasted_iota(jnp.int32, (1, LANES), 1)
    return lane < NA_HEAD_DIM


def _ctx_attn_kernel(q_ref, k_ref, v_ref, o_ref):
    first_half = _pair_masks()
    s = q_ref.shape[0]
    scale = NA_HEAD_DIM ** -0.5
    for p in range(HEAD_PAIRS):
        cols = slice(p * LANES, (p + 1) * LANES)
        q2 = q_ref[:, cols] * scale
        kb = k_ref[:, cols].astype(BF16)
        vb = v_ref[:, cols].astype(BF16)
        qs = jnp.concatenate([jnp.where(first_half, q2, 0.0), jnp.where(first_half, 0.0, q2)], axis=0).astype(BF16)
        sc = _dot_nt(qs, kb)
        e = jnp.exp(sc - jnp.max(sc, axis=-1, keepdims=True))
        pv = _dot(e.astype(BF16), vb) / jnp.sum(e, axis=-1, keepdims=True)
        o_ref[:, cols] = jnp.where(first_half, pv[:s], pv[s:]).astype(BF16)


def _ctx_attn(q, k, v, *, seq_len):
    m, d = q.shape
    spec = pl.BlockSpec((seq_len, d), lambda i: (i, 0))
    return pl.pallas_call(
        _ctx_attn_kernel,
        grid=(m // seq_len,),
        in_specs=[spec, spec, spec],
        out_specs=spec,
        out_shape=jax.ShapeDtypeStruct((m, d), BF16),
        compiler_params=_cparams(("arbitrary",)),
        name="ctx_attention",
    )(q, k, v)


def _lat_attn_kernel(q_ref, k_ref, v_ref, kc_ref, vc_ref, rpb_ref, o_ref, bias_ref, *, rows, q_tile):
    b = pl.program_id(1)
    first_half = _pair_masks()
    n_tok = rows * GRID_W
    neg_inf = -jnp.inf

    @pl.when(b == 0)
    def _():
        qc = lax.broadcasted_iota(jnp.int32, (GRID_W, LANES), 0)
        kc = lax.broadcasted_iota(jnp.int32, (GRID_W, LANES), 1)
        cs = jnp.clip(qc - NA_KW // 2, 0, GRID_W - NA_KW)
        col_ok = (kc >= cs) & (kc < cs + NA_KW)
        kh = min(NA_KH, rows)
        for hh in range(2):
            bias_ref[hh] = jnp.full((n_tok, n_tok), neg_inf, F32)
            for dr in range(2 * NA_KH - 1):
                v = jnp.broadcast_to(rpb_ref[hh, dr:dr + 1, :], (GRID_W, LANES))
                t = pltpu.roll(v, LANES - (NA_KW - 1), 1, stride=1, stride_axis=0)
                tile = jnp.where(col_ok, t, neg_inf)[:, :GRID_W]
                for r in range(rows):
                    start = min(max(r - kh // 2, 0), rows - kh)
                    i = dr - (NA_KH - 1) + r - start
                    if 0 <= i < kh:
                        kr = start + i
                        bias_ref[hh, r * GRID_W:(r + 1) * GRID_W, kr * GRID_W:(kr + 1) * GRID_W] = tile

    scale = NA_HEAD_DIM ** -0.5
    kb = k_ref[...].astype(BF16)
    vb = v_ref[...].astype(BF16)
    kcb = kc_ref[...].astype(BF16)
    vcb = vc_ref[...].astype(BF16)
    for qt in range(n_tok // q_tile):
        qrows = slice(qt * q_tile, (qt + 1) * q_tile)
        q2 = q_ref[qrows, :] * scale
        outs = []
        for hh in range(2):
            qm = (jnp.where(first_half, q2, 0.0) if hh == 0 else jnp.where(first_half, 0.0, q2)).astype(BF16)
            s_loc = _dot_nt(qm, kb) + bias_ref[hh, qrows, :]
            s_ctx = _dot_nt(qm, kcb)
            mx = jnp.maximum(jnp.max(s_loc, axis=-1, keepdims=True), jnp.max(s_ctx, axis=-1, keepdims=True))
            e_loc = jnp.exp(s_loc - mx)
            e_ctx = jnp.exp(s_ctx - mx)
            den = jnp.sum(e_loc, axis=-1, keepdims=True) + jnp.sum(e_ctx, axis=-1, keepdims=True)
            outs.append((_dot(e_loc.astype(BF16), vb) + _dot(e_ctx.astype(BF16), vcb)) / den)
        o_ref[qrows, :] = jnp.where(first_half, outs[0], outs[1]).astype(BF16)


def _lat_attn(q, k, v, kc, vc, rpb_pad, *, n_batch, n_tok, n_ctx, q_tile=256):
    d = q.shape[1]
    rows = n_tok // GRID_W
    tok_spec = pl.BlockSpec((n_tok, LANES), lambda p, b: (b, p))
    ctx_spec = pl.BlockSpec((n_ctx, LANES), lambda p, b: (b, p))
    kern = functools.partial(_lat_attn_kernel, rows=rows, q_tile=q_tile)
    return pl.pallas_call(
        kern,
        grid=(HEAD_PAIRS, n_batch),
        in_specs=[tok_spec, tok_spec, tok_spec, ctx_spec, ctx_spec,
                  pl.BlockSpec((2, 2 * NA_KH, LANES), lambda p, b: (p, 0, 0))],
        out_specs=tok_spec,
        out_shape=jax.ShapeDtypeStruct((n_batch * n_tok, d), BF16),
        scratch_shapes=[pltpu.VMEM((2, n_tok, n_tok), F32)],
        compiler_params=_cparams(("arbitrary", "arbitrary")),
        name="latent_attention",
    )(q, k, v, kc, vc, rpb_pad)


def _permute_w_in(w_in):
    s1 = SSD_INNER
    s2 = s1 + SSD_XBC
    s3 = s2 + 2 * SSD_HEADS
    z = w_in[:, :s1]
    xs = w_in[:, s1:s1 + SSD_INNER]
    bc = w_in[:, s1 + SSD_INNER:s2]
    dt = w_in[:, s2:s3]
    rest = w_in[:, s3:]
    main = jnp.concatenate([z, xs, rest, bc], axis=1).astype(BF16)
    dtw = jnp.pad(dt, ((0, 0), (0, LANES - 2 * SSD_HEADS))).astype(BF16)
    return main, dtw


def _pad_lanes(v):
    return jnp.pad(v, (0, LANES - v.shape[0]))


def _run_stream(x3, mod, row_base, W, *, is_ctx, state_ssd=None, cache_k=None, cache_v=None):
    b, s, d = x3.shape
    m = b * s
    x = x3.reshape(m, d)
    rpg = m if is_ctx else s
    kw = dict(row_base=row_base, rows_per_group=rpg)

    u, dtr = _nm_matmul(x, mod, 0, 0, 1, W['norm_mix_g'][0], W['w_in'], n_groups=1, group_width=U_COLS,
                        w_extra=W['w_dt'], **kw)
    h0 = None
    if not is_ctx:
        h0 = state_ssd[:, 0].reshape(b, 2, HEAD_PAIRS, LANES, SSD_STATE)
    res = _ssd(u.reshape(b, s, U_COLS), dtr.reshape(b, s, LANES), W['ssd_conv_w'], W['ssd_conv_b'], W['ssd_prm'],
               W['ssd_norm_g'], h0, emit_state=is_ctx)
    ya = res[0].reshape(m, SSD_INNER)
    yb = _short_conv(u, W['sc_conv_w'], seq_len=s)
    x = _proj_res(x, mod, 0, 2, [ya, yb], W['mix0_w_out'], **kw)
    x = _ffn(x, mod, 0, W['norm_ffn_g'][0], W['ffn_w_gate'][0], W['ffn_w_up'][0], W['ffn_conv_w'][0], W['ffn_w_down'][0],
             W['final_norm_g'], seq_len=s, final_norm=False, **kw)

    q, k, v = _nm_matmul(x, mod, 1, 0, 1, W['norm_mix_g'][1], W['na_w_qkv'], n_groups=3, group_width=D_MODEL, **kw)
    if is_ctx:
        o = _ctx_attn(q, k, v, seq_len=s)
    else:
        n_ctx = cache_k.shape[2]
        kc = cache_k[:, 0].reshape(b * n_ctx, d)
        vc = cache_v[:, 0].reshape(b * n_ctx, d)
        o = _lat_attn(q, k, v, kc, vc, W['rpb_pad'], n_batch=b, n_tok=s, n_ctx=n_ctx)
    x = _proj_res(x, mod, 1, 2, [o], W['na_w_out'], **kw)
    x = _ffn(x, mod, 1, W['norm_ffn_g'][1], W['ffn_w_gate'][1], W['ffn_w_up'][1], W['ffn_conv_w'][1], W['ffn_w_down'][1],
             W['final_norm_g'], seq_len=s, final_norm=True, **kw)
    y = x.reshape(b, s, d)
    if is_ctx:
        new_state = res[1].reshape(b, 1, 2, SSD_HEADS, SSD_HEADDIM, SSD_STATE)
        new_k = k.reshape(b, 1, s, NA_HEADS, NA_HEAD_DIM)
        new_v = v.reshape(b, 1, s, NA_HEADS, NA_HEAD_DIM)
        return y, new_state, new_k, new_v
    return y


def kernel(x_prompt, x_sample, state_ssd, cache_k, cache_v, c, c_ctx, ada_w, ada_b, norm_mix_g, norm_ffn_g, ssd_w_in,
           ssd_conv_w, ssd_conv_b, ssd_dt_bias, ssd_a_log, ssd_d, ssd_norm_g, sc_conv_w, mix0_w_out, na_w_qkv, na_rpb,
           na_w_out, ffn_w_gate, ffn_w_up, ffn_conv_w, ffn_w_down, final_norm_g):
    n_lat = x_sample.shape[0]
    cvec = jnp.concatenate([c_ctx[None, :], c, jnp.zeros((8 - 1 - n_lat, D_MODEL), F32)], axis=0)
    mod = _ada(cvec, ada_w, ada_b)

    w_in, w_dt = _permute_w_in(ssd_w_in[0])
    prm = jnp.stack([_pad_lanes(ssd_dt_bias[0].reshape(-1)), _pad_lanes(ssd_a_log[0].reshape(-1)),
                     _pad_lanes(ssd_d[0, 0]), _pad_lanes(ssd_d[0, 1])] + [jnp.zeros((LANES,), F32)] * 4, axis=0)
    rpb = na_rpb[0]
    rpb_pad = jnp.pad(rpb, ((0, 0), (0, 1), (0, LANES - rpb.shape[2])))
    W = {
        'norm_mix_g': norm_mix_g, 'norm_ffn_g': norm_ffn_g, 'final_norm_g': final_norm_g,
        'w_in': w_in, 'w_dt': w_dt,
        'ssd_conv_w': ssd_conv_w[0], 'ssd_conv_b': ssd_conv_b[0].reshape(1, SSD_XBC), 'ssd_prm': prm,
        'ssd_norm_g': ssd_norm_g[0].reshape(1, SSD_INNER), 'sc_conv_w': sc_conv_w[0],
        'mix0_w_out': mix0_w_out[0].astype(BF16),
        'na_w_qkv': na_w_qkv[0].astype(BF16), 'rpb_pad': rpb_pad, 'na_w_out': na_w_out[0].astype(BF16),
        'ffn_w_gate': ffn_w_gate.astype(BF16), 'ffn_w_up': ffn_w_up.astype(BF16), 'ffn_conv_w': ffn_conv_w,
        'ffn_w_down': ffn_w_down.astype(BF16),
    }
    y_prompt, new_state, new_k, new_v = _run_stream(x_prompt, mod, 0, W, is_ctx=True)
    y_sample = _run_stream(x_sample, mod, 1, W, is_ctx=False, state_ssd=state_ssd, cache_k=cache_k, cache_v=cache_v)
    return (y_prompt, y_sample, new_state, new_k, new_v)
```

```python
import functools

import jax
import jax.numpy as jnp
from jax import lax
from jax.experimental import pallas as pl
from jax.experimental.pallas import tpu as pltpu

F32 = jnp.float32
BF16 = jnp.bfloat16

D_MODEL = 1024
EPS = 1e-6
GRID_W = 64
SSD_HEADDIM = 64
SSD_HEADS = 16
SSD_STATE = 128
SSD_GROUPS = 2
SSD_INNER = 1024
SSD_XBC = 1536
SSD_LC = 256
SC_WIDTH = 1024
NA_HEADS = 16
NA_HEAD_DIM = 64
NA_KH = 8
NA_KW = 16

LOG2E = 1.4426950408889634
LANES = 128
SUBLANES = 8
HEAD_PAIRS = 8
U_COLS = 5 * 1024 + 512

VMEM_LIMIT = 56 * 1024 * 1024

_RESIDENT = pl.BlockSpec(memory_space=pltpu.VMEM)


def _cparams(sem):
    return pltpu.CompilerParams(dimension_semantics=sem, vmem_limit_bytes=VMEM_LIMIT)


def _silu(x):
    return x * jax.nn.sigmoid(x)


def _dot(a, b):
    return jnp.dot(a, b, preferred_element_type=F32)


def _dot_nt(a, b):
    return lax.dot_general(a, b, (((1,), (1,)), ((), ())), preferred_element_type=F32)


def _split3(v):
    hi = v.astype(BF16)
    r1 = v - hi.astype(F32)
    mid = r1.astype(BF16)
    lo = (r1 - mid.astype(F32)).astype(BF16)
    return hi, mid, lo


def _sel_dot_left(sel, v):
    hi, mid, lo = _split3(v)
    return _dot(sel, hi) + _dot(sel, mid) + _dot(sel, lo)


def _norm_mod(x, g, shift, scale):
    ms = jnp.mean(x * x, axis=-1, keepdims=True)
    y = x * lax.rsqrt(ms + EPS) * g
    return y * (1.0 + scale) + shift


def _mod_row(i, tm, row_base, rows_per_group):
    return row_base + (i * tm) // rows_per_group


def _ada_kernel(c_ref, w_ref, b_ref, o_ref):
    s = _silu(c_ref[...]).astype(BF16)
    o_ref[0] = _dot(s, w_ref[0].astype(BF16)) + b_ref[0]


def _ada(cvec8, ada_w, ada_b):
    depth, d, n = ada_w.shape
    tn = 1536
    return pl.pallas_call(
        _ada_kernel,
        grid=(depth, n // tn),
        in_specs=[
            pl.BlockSpec((8, d), lambda l, j: (0, 0)),
            pl.BlockSpec((1, d, tn), lambda l, j: (l, 0, j)),
            pl.BlockSpec((1, 1, tn), lambda l, j: (l, 0, j)),
        ],
        out_specs=pl.BlockSpec((1, 8, tn), lambda l, j: (l, 0, j)),
        out_shape=jax.ShapeDtypeStruct((depth, 8, n), F32),
        compiler_params=_cparams(("arbitrary", "arbitrary")),
        name="ada_mod",
    )(cvec8, ada_w, ada_b.reshape(depth, 1, n))


def _nm_matmul_kernel(*refs, n_groups, group_width, has_extra, tm, tn, row_base, rows_per_group):
    x_ref, sh_ref, sc_ref, g_ref, w_ref = refs[:5]
    pos = 5
    if has_extra:
        we_ref = refs[pos]
        pos += 1
    o_refs = refs[pos:pos + n_groups]
    pos += n_groups
    if has_extra:
        oe_ref = refs[pos]
    r = _mod_row(pl.program_id(0), tm, row_base, rows_per_group)
    sh = sh_ref[0, pl.ds(r, 1), :]
    sc = sc_ref[0, pl.ds(r, 1), :]
    h = _norm_mod(x_ref[...], g_ref[...], sh, sc).astype(BF16)
    if has_extra:
        oe_ref[...] = _dot(h, we_ref[...])
    for gi, o_ref in enumerate(o_refs):
        for j in range(group_width // tn):
            res = _dot(h, w_ref[:, gi * group_width + j * tn:gi * group_width + (j + 1) * tn])
            o_ref[:, j * tn:(j + 1) * tn] = res.astype(o_ref.dtype)


def _nm_matmul(x, mod, layer, shift_idx, scale_idx, g, w, *, n_groups, group_width, row_base, rows_per_group,
               w_extra=None, out_dtype=F32, tm=512, tn=512):
    m, d = x.shape
    has_extra = w_extra is not None
    in_specs = [
        pl.BlockSpec((tm, d), lambda i: (i, 0)),
        pl.BlockSpec((1, 8, d), lambda i: (layer, 0, shift_idx)),
        pl.BlockSpec((1, 8, d), lambda i: (layer, 0, scale_idx)),
        pl.BlockSpec((1, d), lambda i: (0, 0)),
        _RESIDENT,
    ]
    args = [x, mod, mod, g.reshape(1, d), w]
    out_specs = [pl.BlockSpec((tm, group_width), lambda i: (i, 0)) for _ in range(n_groups)]
    out_shape = [jax.ShapeDtypeStruct((m, group_width), out_dtype) for _ in range(n_groups)]
    if has_extra:
        ne = w_extra.shape[1]
        in_specs.append(_RESIDENT)
        args.append(w_extra)
        out_specs.append(pl.BlockSpec((tm, ne), lambda i: (i, 0)))
        out_shape.append(jax.ShapeDtypeStruct((m, ne), F32))
    kern = functools.partial(_nm_matmul_kernel, n_groups=n_groups, group_width=group_width, has_extra=has_extra, tm=tm,
                             tn=tn, row_base=row_base, rows_per_group=rows_per_group)
    return pl.pallas_call(
        kern,
        grid=(m // tm,),
        in_specs=in_specs,
        out_specs=out_specs,
        out_shape=out_shape,
        compiler_params=_cparams(("arbitrary",)),
        name="norm_mod_matmul",
    )(*args)


def _proj_res_kernel(*refs, n_parts, tm, row_base, rows_per_group):
    x_ref, gate_ref, w_ref = refs[:3]
    a_refs = refs[3:3 + n_parts]
    o_ref = refs[3 + n_parts]
    r = _mod_row(pl.program_id(0), tm, row_base, rows_per_group)
    kp = w_ref.shape[0] // n_parts
    acc = _dot(a_refs[0][...], w_ref[0:kp, :])
    for k in range(1, n_parts):
        acc = acc + _dot(a_refs[k][...], w_ref[k * kp:(k + 1) * kp, :])
    o_ref[...] = x_ref[...] + gate_ref[0, pl.ds(r, 1), :] * acc


def _proj_res(x, mod, layer, gate_idx, parts, w, *, row_base, rows_per_group, tm=1024):
    m, d = x.shape
    n_parts = len(parts)
    kp = w.shape[0] // n_parts
    in_specs = [
        pl.BlockSpec((tm, d), lambda i: (i, 0)),
        pl.BlockSpec((1, 8, d), lambda i: (layer, 0, gate_idx)),
        _RESIDENT,
    ]
    in_specs += [pl.BlockSpec((tm, kp), lambda i: (i, 0)) for _ in parts]
    kern = functools.partial(_proj_res_kernel, n_parts=n_parts, tm=tm, row_base=row_base, rows_per_group=rows_per_group)
    return pl.pallas_call(
        kern,
        grid=(m // tm,),
        in_specs=in_specs,
        out_specs=pl.BlockSpec((tm, d), lambda i: (i, 0)),
        out_shape=jax.ShapeDtypeStruct((m, d), F32),
        compiler_params=_cparams(("arbitrary",)),
        name="proj_residual",
    )(x, mod, w, *parts)


def _dwconv3(v, w, seq_len, row0=0):
    rows = v.shape[0]
    pos = (lax.broadcasted_iota(jnp.int32, v.shape, 0) + row0) % seq_len
    prev = jnp.where(pos == 0, 0.0, pltpu.roll(v, 1, 0))
    nxt = jnp.where(pos == seq_len - 1, 0.0, pltpu.roll(v, rows - 1, 0))
    return prev * w[0:1, :] + v * w[1:2, :] + nxt * w[2:3, :]


MIX_SHIFT = 2 * SSD_HEADS
DT_COL0 = SSD_INNER + SSD_XBC
TAIL_COL0 = DT_COL0 + 3 * SC_WIDTH
W_IN_PAD = TAIL_COL0 + LANES


def _in_proj_kernel(x_ref, sh_ref, sc_ref, g_ref, w_ref, cw_ref, cb_ref, scw_ref, zs_ref, x2_ref, bc_ref, yb_ref, dt_ref,
                    *, tm, row_base, rows_per_group, seq_len):
    r = _mod_row(pl.program_id(0), tm, row_base, rows_per_group)
    sh = sh_ref[0, pl.ds(r, 1), :]
    sc = sc_ref[0, pl.ds(r, 1), :]
    h = _norm_mod(x_ref[...], g_ref[...], sh, sc).astype(BF16)
    tn = 512

    def proj(c0, width=tn):
        return _dot(h, w_ref[:, c0:c0 + width])

    for j in range(SSD_INNER // tn):
        cols = slice(j * tn, (j + 1) * tn)
        zs_ref[:, cols] = _silu(proj(j * tn)).astype(BF16)
        xc = _dwconv3(proj(SSD_INNER + j * tn), cw_ref[:, cols], seq_len) + cb_ref[:, cols]
        x2_ref[:, cols] = _silu(xc).astype(BF16)
    bcc = _dwconv3(proj(2 * SSD_INNER), cw_ref[:, SSD_INNER:SSD_XBC], seq_len) + cb_ref[:, SSD_INNER:SSD_XBC]
    bc_ref[...] = _silu(bcc).astype(BF16)
    dt_ref[...] = proj(DT_COL0, LANES)
    q3 = proj(TAIL_COL0, LANES)
    tail = lax.broadcasted_iota(jnp.int32, (1, LANES), 1) < MIX_SHIFT
    for j in range(SC_WIDTH // tn):
        c0 = DT_COL0 + j * tn
        cols = slice(j * tn, (j + 1) * tn)
        p0, p1, p2 = proj(c0), proj(c0 + SC_WIDTH), proj(c0 + 2 * SC_WIDTH)
        y = p0 * _dwconv3(p1 * p2, scw_ref[:, cols], seq_len)
        if j == 0:
            yt = p1[:, :LANES] * _dwconv3(p2[:, :LANES] * q3, scw_ref[:, :LANES], seq_len)
            y = jnp.concatenate([jnp.where(tail, yt, y[:, :LANES]), y[:, LANES:]], axis=1)
        yb_ref[:, cols] = y.astype(BF16)


def _in_proj(x, mod, g, w, conv_w, conv_b, sc_conv_w_rot, *, seq_len, row_base, rows_per_group, tm):
    m, d = x.shape
    kern = functools.partial(_in_proj_kernel, tm=tm, row_base=row_base, rows_per_group=rows_per_group, seq_len=seq_len)
    widths = (SSD_INNER, SSD_INNER, SSD_XBC - SSD_INNER, SC_WIDTH, LANES)
    dtypes = (BF16, BF16, BF16, BF16, F32)
    return pl.pallas_call(
        kern,
        grid=(m // tm,),
        in_specs=[
            pl.BlockSpec((tm, d), lambda i: (i, 0)),
            pl.BlockSpec((1, 8, d), lambda i: (0, 0, 0)),
            pl.BlockSpec((1, 8, d), lambda i: (0, 0, 1)),
            pl.BlockSpec((1, d), lambda i: (0, 0)),
            _RESIDENT,
            pl.BlockSpec((3, SSD_XBC), lambda i: (0, 0)),
            pl.BlockSpec((1, SSD_XBC), lambda i: (0, 0)),
            pl.BlockSpec((3, SC_WIDTH), lambda i: (0, 0)),
        ],
        out_specs=[pl.BlockSpec((tm, wd), lambda i: (i, 0)) for wd in widths],
        out_shape=[jax.ShapeDtypeStruct((m, wd), dt) for wd, dt in zip(widths, dtypes)],
        compiler_params=_cparams(("arbitrary",)),
        name="in_proj",
    )(x, mod, mod, g.reshape(1, d), w, conv_w, conv_b, sc_conv_w_rot)


def _ffn_kernel(x_ref, sh_ref, sc_ref, gate_ref, g_ref, wg_ref, wu_ref, cw_ref, wd_ref, fg_ref, o_ref, h_ref, act_ref,
                *, tm, tf, row_base, rows_per_group, seq_len, final_norm):
    r = _mod_row(pl.program_id(0), tm, row_base, rows_per_group)
    sh = sh_ref[0, pl.ds(r, 1), :]
    sc = sc_ref[0, pl.ds(r, 1), :]
    gate = gate_ref[0, pl.ds(r, 1), :]
    h_ref[...] = _norm_mod(x_ref[...], g_ref[...], sh, sc).astype(BF16)
    dff = wg_ref.shape[1]
    half = tm // 2
    win = half + SUBLANES
    for hb in range(2):
        w0 = hb * (half - SUBLANES)
        v0 = hb * SUBLANES
        rows = slice(hb * half, (hb + 1) * half)
        hw = h_ref[w0:w0 + win, :]
        hv = h_ref[rows, :]
        for fc in range(dff // tf):
            cols = slice(fc * tf, (fc + 1) * tf)
            gpre = _dwconv3(_dot(hw, wg_ref[:, cols]), cw_ref[:, cols], seq_len, row0=w0)[v0:v0 + half]
            up = _dot(hv, wu_ref[:, cols])
            act_ref[:, cols] = (jax.nn.gelu(gpre, approximate=True) * up).astype(BF16)
        y = x_ref[rows, :] + gate * _dot(act_ref[...], wd_ref[...])
        if final_norm:
            ms = jnp.mean(y * y, axis=-1, keepdims=True)
            y = y * lax.rsqrt(ms + EPS) * fg_ref[...]
        o_ref[rows, :] = y


def _ffn(x, mod, layer, g, wg, wu, cw, wd, final_g, *, row_base, rows_per_group, seq_len, final_norm, tm=1024, tf=256):
    m, d = x.shape
    dff = wg.shape[1]
    kern = functools.partial(_ffn_kernel, tm=tm, tf=tf, row_base=row_base, rows_per_group=rows_per_group,
                             seq_len=seq_len, final_norm=final_norm)
    return pl.pallas_call(
        kern,
        grid=(m // tm,),
        in_specs=[
            pl.BlockSpec((tm, d), lambda i: (i, 0)),
            pl.BlockSpec((1, 8, d), lambda i: (layer, 0, 3)),
            pl.BlockSpec((1, 8, d), lambda i: (layer, 0, 4)),
            pl.BlockSpec((1, 8, d), lambda i: (layer, 0, 5)),
            pl.BlockSpec((1, d), lambda i: (0, 0)),
            _RESIDENT, _RESIDENT, _RESIDENT, _RESIDENT,
            pl.BlockSpec((1, d), lambda i: (0, 0)),
        ],
        out_specs=pl.BlockSpec((tm, d), lambda i: (i, 0)),
        out_shape=jax.ShapeDtypeStruct((m, d), F32),
        scratch_shapes=[pltpu.VMEM((tm, d), BF16), pltpu.VMEM((tm // 2, dff), BF16)],
        compiler_params=_cparams(("arbitrary",)),
        name="conv_ffn",
    )(x, mod, mod, mod, g.reshape(1, d), wg, wu, cw, wd, final_g.reshape(1, d))


def _pair_cols(first_half, arr, h0, h1):
    return jnp.where(first_half, arr[:, h0:h0 + 1], arr[:, h1:h1 + 1])


def _pair_rows(arr_t, h0, h1, width):
    return jnp.concatenate([jnp.broadcast_to(arr_t[h0:h0 + 1, :], (SSD_HEADDIM, width)),
                            jnp.broadcast_to(arr_t[h1:h1 + 1, :], (SSD_HEADDIM, width))], axis=0)


def _pair_scalars(row, h0, h1):
    return jnp.concatenate([jnp.broadcast_to(row[:, h0:h0 + 1], (SSD_HEADDIM, SSD_STATE)),
                            jnp.broadcast_to(row[:, h1:h1 + 1], (SSD_HEADDIM, SSD_STATE))], axis=0)


def _ssd_kernel(*refs, seq_len, has_h0, emit_state):
    zs_ref, x2_ref, bc_ref, dt_ref, prm_ref, ng_ref = refs[:6]
    pos = 6
    if has_h0:
        h0_ref = refs[pos]
        pos += 1
    ya_ref = refs[pos]
    pos += 1
    if emit_state:
        st_ref = refs[pos]
        pos += 1
    yacc_ref = refs[pos]

    lc = SSD_LC
    nchunk = seq_len // lc
    nh = SSD_HEADS
    lane = lax.broadcasted_iota(jnp.int32, (1, LANES), 1)
    first_half = lane < SSD_HEADDIM
    fwd_lane = lane < nh

    bc = bc_ref[0]
    valid = lane < 2 * nh
    dt = jnp.where(valid, jax.nn.softplus(dt_ref[0] + prm_ref[0:1, :]), 0.0)
    a = -jnp.exp(prm_ref[1:2, :])
    dta = jnp.where(valid, dt * a, 0.0)
    dsum = prm_ref[2:3, :] + prm_ref[3:4, :]
    tt = lax.broadcasted_iota(jnp.int32, (lc, lc), 0)
    ss = lax.broadcasted_iota(jnp.int32, (lc, lc), 1)
    tril = jnp.where(ss <= tt, 1.0, 0.0).astype(BF16)
    triu = jnp.where(ss >= tt, 1.0, 0.0).astype(BF16)
    nq = lc // LANES
    td = lax.broadcasted_iota(jnp.int32, (LANES, LANES), 0)
    sd = lax.broadcasted_iota(jnp.int32, (LANES, LANES), 1)
    lower_d = td > sd
    upper_d = td < sd

    acs2_l, rowt_l, wts_l, eacs_l, edec_l, cbm_l, bm_l, cm_l = [], [], [], [], [], [], [], []
    for c in range(nchunk):
        rows = slice(c * lc, (c + 1) * lc)
        acs = jnp.where(fwd_lane, _sel_dot_left(tril, dta[rows]), _sel_dot_left(triu, dta[rows]))
        edge = jnp.where(fwd_lane, acs[lc - 1:lc, :], acs[0:1, :])
        acs2 = acs * LOG2E
        dtt = dt[rows].T
        rowt = acs2.T - jnp.log2(dtt)
        dgt = jnp.log2(dtt[0:nh, :] + dtt[nh:2 * nh, :])
        acs2_l.append(acs2)
        rowt_l.append((rowt, dgt))
        wts_l.append((dt[rows] * jnp.exp(edge - acs)).T)
        eacs_l.append(jnp.exp(acs))
        edec_l.append(jnp.exp(edge))
        cbm_g, bm_g, cm_g = [], [], []
        for g in range(SSD_GROUPS):
            bm = bc[rows, g * SSD_STATE:(g + 1) * SSD_STATE]
            cm = bc[rows, (SSD_GROUPS + g) * SSD_STATE:(SSD_GROUPS + g + 1) * SSD_STATE]
            bm_g.append(bm)
            cm_g.append(cm)
            cbm_g.append(_dot_nt(cm, bm))
        cbm_l.append(cbm_g)
        bm_l.append(bm_g)
        cm_l.append(cm_g)

    for p in range(HEAD_PAIRS):
        cols = slice(p * LANES, (p + 1) * LANES)
        g = p // (HEAD_PAIRS // SSD_GROUPS)
        hd = (2 * p, 2 * p + 1)
        x2b = x2_ref[0, :, cols]
        x2 = x2b.astype(F32)
        dsum2 = _pair_cols(first_half, dsum, hd[0], hd[1])
        if has_h0:
            hf = h0_ref[0, 0, p]
            hb = h0_ref[0, 1, p]
        else:
            hf = jnp.zeros((LANES, SSD_STATE), F32)
            hb = jnp.zeros((LANES, SSD_STATE), F32)
        ys = []
        stbs = []
        for c in range(nchunk):
            rows = slice(c * lc, (c + 1) * lc)
            x2c = x2[rows]
            x2cb = x2b[rows]
            acs2 = acs2_l[c]
            rowt, dgt = rowt_l[c]
            cbm = cbm_l[c][g]
            outs = []
            for h in hd:
                blocks = []
                for bi in range(nq):
                    ri = slice(bi * LANES, (bi + 1) * LANES)
                    cf = acs2[ri, h:h + 1]
                    cbk = acs2[ri, nh + h:nh + h + 1]
                    row_blocks = []
                    for bj in range(nq):
                        cj = slice(bj * LANES, (bj + 1) * LANES)
                        rf = rowt[h:h + 1, cj]
                        rb = rowt[nh + h:nh + h + 1, cj]
                        if bi > bj:
                            arg = cf - rf
                        elif bi < bj:
                            arg = cbk - rb
                        else:
                            arg = jnp.where(lower_d, cf - rf, jnp.where(upper_d, cbk - rb, dgt[h:h + 1, cj]))
                        row_blocks.append((cbm[ri, cj] * jnp.exp2(arg)).astype(BF16))
                    blocks.append(jnp.concatenate(row_blocks, axis=1))
                gm = jnp.concatenate(blocks, axis=0)
                outs.append(_dot(gm, x2cb))
            y = jnp.where(first_half, outs[0], outs[1]) + x2c * dsum2
            xt = x2c.T
            xwf = (xt * _pair_rows(wts_l[c], hd[0], hd[1], lc)).astype(BF16)
            xwb = (xt * _pair_rows(wts_l[c], nh + hd[0], nh + hd[1], lc)).astype(BF16)
            stf = _dot(xwf, bm_l[c][g])
            stbs.append(_dot(xwb, bm_l[c][g]))
            if has_h0 or c > 0:
                y = y + _dot_nt(cm_l[c][g], hf.astype(BF16)) * _pair_cols(first_half, eacs_l[c], hd[0], hd[1])
            hf = hf * _pair_scalars(edec_l[c], hd[0], hd[1]) + stf
            ys.append(y)
        for c in reversed(range(nchunk)):
            if has_h0 or c < nchunk - 1:
                ys[c] = ys[c] + (_dot_nt(cm_l[c][g], hb.astype(BF16))
                                 * _pair_cols(first_half, eacs_l[c], nh + hd[0], nh + hd[1]))
            hb = hb * _pair_scalars(edec_l[c], nh + hd[0], nh + hd[1]) + stbs[c]
        zg = zs_ref[0, :, cols].astype(F32)
        for c in range(nchunk):
            rows = slice(c * lc, (c + 1) * lc)
            yacc_ref[rows, cols] = ys[c] * zg[rows]
        if emit_state:
            st_ref[0, 0, p] = hf
            st_ref[0, 1, p] = hb

    y = yacc_ref[...]
    ms = jnp.mean(y * y, axis=-1, keepdims=True)
    ya_ref[0] = (y * lax.rsqrt(ms + EPS) * ng_ref[...]).astype(BF16)


def _ssd(zs3, x23, bc3, dt3, prm, norm_g, h0, *, emit_state):
    b, s, _ = zs3.shape
    has_h0 = h0 is not None
    in_specs = [
        pl.BlockSpec((1, s, SSD_INNER), lambda i: (i, 0, 0)),
        pl.BlockSpec((1, s, SSD_INNER), lambda i: (i, 0, 0)),
        pl.BlockSpec((1, s, SSD_XBC - SSD_INNER), lambda i: (i, 0, 0)),
        pl.BlockSpec((1, s, LANES), lambda i: (i, 0, 0)),
        pl.BlockSpec((8, LANES), lambda i: (0, 0)),
        pl.BlockSpec((1, SSD_INNER), lambda i: (0, 0)),
    ]
    args = [zs3, x23, bc3, dt3, prm, norm_g]
    state_block = (1, 2, HEAD_PAIRS, LANES, SSD_STATE)
    if has_h0:
        in_specs.append(pl.BlockSpec(state_block, lambda i: (i, 0, 0, 0, 0)))
        args.append(h0)
    out_specs = [pl.BlockSpec((1, s, SSD_INNER), lambda i: (i, 0, 0))]
    out_shape = [jax.ShapeDtypeStruct((b, s, SSD_INNER), BF16)]
    if emit_state:
        out_specs.append(pl.BlockSpec(state_block, lambda i: (i, 0, 0, 0, 0)))
        out_shape.append(jax.ShapeDtypeStruct((b, 2, HEAD_PAIRS, LANES, SSD_STATE), F32))
    kern = functools.partial(_ssd_kernel, seq_len=s, has_h0=has_h0, emit_state=emit_state)
    return pl.pallas_call(
        kern,
        grid=(b,),
        in_specs=in_specs,
        out_specs=out_specs,
        out_shape=out_shape,
        scratch_shapes=[pltpu.VMEM((s, SSD_INNER), F32)],
        compiler_params=_cparams(("arbitrary",)),
        name="ssd",
    )(*args)


def _pair_masks():
    lane = lax.broadcasted_iota(jnp.int32, (1, LANES), 1)
    return lane < NA_HEAD_DIM


def _qkv_ctx_attn_kernel(x_ref, sh_ref, sc_ref, g_ref, w_ref, o_ref, nk_ref, nv_ref, qkv_ref,
                         *, tm, seq_len, row_base, rows_per_group):
    r = _mod_row(pl.program_id(0), tm, row_base, rows_per_group)
    sh = sh_ref[0, pl.ds(r, 1), :]
    sc = sc_ref[0, pl.ds(r, 1), :]
    h = _norm_mod(x_ref[...], g_ref[...], sh, sc).astype(BF16)
    d = x_ref.shape[1]
    tn = 512
    for j in range(3 * d // tn):
        qkv_ref[:, j * tn:(j + 1) * tn] = _dot(h, w_ref[:, j * tn:(j + 1) * tn])
    first_half = _pair_masks()
    scale = NA_HEAD_DIM ** -0.5
    s = seq_len
    for b in range(tm // s):
        rows = slice(b * s, (b + 1) * s)
        nk_ref[b, 0] = qkv_ref[rows, d:2 * d].reshape(s, NA_HEADS, NA_HEAD_DIM)
        nv_ref[b, 0] = qkv_ref[rows, 2 * d:3 * d].reshape(s, NA_HEADS, NA_HEAD_DIM)
        for p in range(HEAD_PAIRS):
            cols = slice(p * LANES, (p + 1) * LANES)
            q2 = qkv_ref[rows, cols] * scale
            kb = qkv_ref[rows, d + p * LANES:d + (p + 1) * LANES].astype(BF16)
            vb = qkv_ref[rows, 2 * d + p * LANES:2 * d + (p + 1) * LANES].astype(BF16)
            qs = jnp.concatenate([jnp.where(first_half, q2, 0.0), jnp.where(first_half, 0.0, q2)], axis=0).astype(BF16)
            sco = _dot_nt(qs, kb)
            e = jnp.exp(sco - jnp.max(sco, axis=-1, keepdims=True))
            pv = _dot(e.astype(BF16), vb) / jnp.sum(e, axis=-1, keepdims=True)
            o_ref[rows, cols] = jnp.where(first_half, pv[:s], pv[s:]).astype(BF16)


def _qkv_ctx_attn(x, mod, layer, g, w, *, seq_len, row_base, rows_per_group, tm=512):
    m, d = x.shape
    nb = tm // seq_len
    cache_block = (nb, 1, seq_len, NA_HEADS, NA_HEAD_DIM)
    cache_shape = jax.ShapeDtypeStruct((m // seq_len, 1, seq_len, NA_HEADS, NA_HEAD_DIM), F32)
    kern = functools.partial(_qkv_ctx_attn_kernel, tm=tm, seq_len=seq_len, row_base=row_base,
                             rows_per_group=rows_per_group)
    return pl.pallas_call(
        kern,
        grid=(m // tm,),
        in_specs=[
            pl.BlockSpec((tm, d), lambda i: (i, 0)),
            pl.BlockSpec((1, 8, d), lambda i: (layer, 0, 0)),
            pl.BlockSpec((1, 8, d), lambda i: (layer, 0, 1)),
            pl.BlockSpec((1, d), lambda i: (0, 0)),
            _RESIDENT,
        ],
        out_specs=[
            pl.BlockSpec((tm, d), lambda i: (i, 0)),
            pl.BlockSpec(cache_block, lambda i: (i, 0, 0, 0, 0)),
            pl.BlockSpec(cache_block, lambda i: (i, 0, 0, 0, 0)),
        ],
        out_shape=[jax.ShapeDtypeStruct((m, d), BF16), cache_shape, cache_shape],
        scratch_shapes=[pltpu.VMEM((tm, 3 * d), F32)],
        compiler_params=_cparams(("arbitrary",)),
        name="qkv_ctx_attention",
    )(x, mod, mod, g.reshape(1, d), w)


def _lat_attn_kernel(q_ref, k_ref, v_ref, kc_ref, vc_ref, rpb_ref, o_ref, bias_ref, *, rows, q_tile):
    b = pl.program_id(1)
    first_half = _pair_masks()
    n_tok = rows * GRID_W
    neg_inf = -jnp.inf

    @pl.when(b == 0)
    def _():
        qc = lax.broadcasted_iota(jnp.int32, (GRID_W, LANES), 0)
        kc = lax.broadcasted_iota(jnp.int32, (GRID_W, LANES), 1)
        cs = jnp.clip(qc - NA_KW // 2, 0, GRID_W - NA_KW)
        col_ok = (kc >= cs) & (kc < cs + NA_KW)
        kh = min(NA_KH, rows)
        for hh in range(2):
            bias_ref[hh] = jnp.full((n_tok, n_tok), neg_inf, F32)
            for dr in range(2 * NA_KH - 1):
                v = jnp.broadcast_to(rpb_ref[hh, dr:dr + 1, :], (GRID_W, LANES))
                t = pltpu.roll(v, LANES - (NA_KW - 1), 1, stride=1, stride_axis=0)
                tile = jnp.where(col_ok, t, neg_inf)[:, :GRID_W]
                for r in range(rows):
                    start = min(max(r - kh // 2, 0), rows - kh)
                    i = dr - (NA_KH - 1) + r - start
                    if 0 <= i < kh:
                        kr = start + i
                        bias_ref[hh, r * GRID_W:(r + 1) * GRID_W, kr * GRID_W:(kr + 1) * GRID_W] = tile

    scale = NA_HEAD_DIM ** -0.5
    kb = k_ref[...].astype(BF16)
    vb = v_ref[...].astype(BF16)
    kcb = kc_ref[...].astype(BF16)
    vcb = vc_ref[...].astype(BF16)
    for qt in range(n_tok // q_tile):
        qrows = slice(qt * q_tile, (qt + 1) * q_tile)
        q2 = q_ref[qrows, :] * scale
        outs = []
        for hh in range(2):
            qm = (jnp.where(first_half, q2, 0.0) if hh == 0 else jnp.where(first_half, 0.0, q2)).astype(BF16)
            s_loc = _dot_nt(qm, kb) + bias_ref[hh, qrows, :]
            s_ctx = _dot_nt(qm, kcb)
            mx = jnp.maximum(jnp.max(s_loc, axis=-1, keepdims=True), jnp.max(s_ctx, axis=-1, keepdims=True))
            e_loc = jnp.exp(s_loc - mx)
            e_ctx = jnp.exp(s_ctx - mx)
            den = jnp.sum(e_loc, axis=-1, keepdims=True) + jnp.sum(e_ctx, axis=-1, keepdims=True)
            outs.append((_dot(e_loc.astype(BF16), vb) + _dot(e_ctx.astype(BF16), vcb)) / den)
        o_ref[qrows, :] = jnp.where(first_half, outs[0], outs[1]).astype(BF16)


def _lat_attn(q, k, v, kc, vc, rpb_pad, *, n_batch, n_tok, n_ctx, q_tile=256):
    d = q.shape[1]
    rows = n_tok // GRID_W
    tok_spec = pl.BlockSpec((n_tok, LANES), lambda p, b: (b, p))
    ctx_spec = pl.BlockSpec((n_ctx, LANES), lambda p, b: (b, p))
    kern = functools.partial(_lat_attn_kernel, rows=rows, q_tile=q_tile)
    return pl.pallas_call(
        kern,
        grid=(HEAD_PAIRS, n_batch),
        in_specs=[tok_spec, tok_spec, tok_spec, ctx_spec, ctx_spec,
                  pl.BlockSpec((2, 2 * NA_KH, LANES), lambda p, b: (p, 0, 0))],
        out_specs=tok_spec,
        out_shape=jax.ShapeDtypeStruct((n_batch * n_tok, d), BF16),
        scratch_shapes=[pltpu.VMEM((2, n_tok, n_tok), F32)],
        compiler_params=_cparams(("arbitrary", "arbitrary")),
        name="latent_attention",
    )(q, k, v, kc, vc, rpb_pad)


def _pad_lanes(v):
    return jnp.pad(v, (0, LANES - v.shape[0]))


def _run_stream(x3, mod, row_base, W, *, is_ctx, state_ssd=None, cache_k=None, cache_v=None):
    b, s, d = x3.shape
    m = b * s
    x = x3.reshape(m, d)
    rpg = m if is_ctx else s
    kw = dict(row_base=row_base, rows_per_group=rpg)

    zs, x2, bca, yb, dtr = _in_proj(x, mod, W['norm_mix_g'][0], W['w_in'], W['ssd_conv_w'], W['ssd_conv_b'],
                                    W['sc_conv_w_rot'], seq_len=s, tm=max(s, 512), **kw)
    h0 = None
    if not is_ctx:
        h0 = state_ssd[:, 0].reshape(b, 2, HEAD_PAIRS, LANES, SSD_STATE)
    res = _ssd(zs.reshape(b, s, -1), x2.reshape(b, s, -1), bca.reshape(b, s, -1), dtr.reshape(b, s, LANES),
               W['ssd_prm'], W['ssd_norm_g'], h0, emit_state=is_ctx)
    ya = res[0].reshape(m, SSD_INNER)
    x = _proj_res(x, mod, 0, 2, [ya, yb], W['mix0_w_out'], **kw)
    x = _ffn(x, mod, 0, W['norm_ffn_g'][0], *W['ffn'][0], W['final_norm_g'], seq_len=s, final_norm=False, **kw)

    if is_ctx:
        o, new_k, new_v = _qkv_ctx_attn(x, mod, 1, W['norm_mix_g'][1], W['na_w_qkv'], seq_len=s, **kw)
    else:
        q, k, v = _nm_matmul(x, mod, 1, 0, 1, W['norm_mix_g'][1], W['na_w_qkv'], n_groups=3, group_width=D_MODEL, **kw)
        n_ctx = cache_k.shape[2]
        kc = cache_k[:, 0].reshape(b * n_ctx, d)
        vc = cache_v[:, 0].reshape(b * n_ctx, d)
        o = _lat_attn(q, k, v, kc, vc, W['rpb_pad'], n_batch=b, n_tok=s, n_ctx=n_ctx)
    x = _proj_res(x, mod, 1, 2, [o], W['na_w_out'], **kw)
    x = _ffn(x, mod, 1, W['norm_ffn_g'][1], *W['ffn'][1], W['final_norm_g'], seq_len=s, final_norm=True, **kw)
    y = x.reshape(b, s, d)
    if is_ctx:
        new_state = res[1].reshape(b, 1, 2, SSD_HEADS, SSD_HEADDIM, SSD_STATE)
        return y, new_state, new_k, new_v
    return y


def kernel(x_prompt, x_sample, state_ssd, cache_k, cache_v, c, c_ctx, ada_w, ada_b, norm_mix_g, norm_ffn_g, ssd_w_in,
           ssd_conv_w, ssd_conv_b, ssd_dt_bias, ssd_a_log, ssd_d, ssd_norm_g, sc_conv_w, mix0_w_out, na_w_qkv, na_rpb,
           na_w_out, ffn_w_gate, ffn_w_up, ffn_conv_w, ffn_w_down, final_norm_g):
    n_lat = x_sample.shape[0]
    cvec = jnp.concatenate([c_ctx[None, :], c, jnp.zeros((8 - 1 - n_lat, D_MODEL), F32)], axis=0)
    mod = _ada(cvec, ada_w, ada_b)

    w_in = jnp.pad(ssd_w_in[0].astype(BF16), ((0, 0), (0, W_IN_PAD - ssd_w_in.shape[2])))
    w_out0 = mix0_w_out[0]
    w_out0 = jnp.concatenate([w_out0[:SSD_INNER], jnp.roll(w_out0[SSD_INNER:], MIX_SHIFT, axis=0)], axis=0).astype(BF16)
    prm =jnp.stack([_pad_lanes(ssd_dt_bias[0].reshape(-1)), _pad_lanes(ssd_a_log[0].reshape(-1)),
                     _pad_lanes(ssd_d[0, 0]), _pad_lanes(ssd_d[0, 1])] + [jnp.zeros((LANES,), F32)] * 4, axis=0)
    rpb = na_rpb[0]
    rpb_pad = jnp.pad(rpb, ((0, 0), (0, 1), (0, LANES - rpb.shape[2])))
    W = {
        'norm_mix_g': norm_mix_g, 'norm_ffn_g': norm_ffn_g, 'final_norm_g': final_norm_g,
        'w_in': w_in,
        'ssd_conv_w': ssd_conv_w[0], 'ssd_conv_b': ssd_conv_b[0].reshape(1, SSD_XBC), 'ssd_prm': prm,
        'ssd_norm_g': ssd_norm_g[0].reshape(1, SSD_INNER), 'sc_conv_w_rot': jnp.roll(sc_conv_w[0], MIX_SHIFT, axis=1),
        'mix0_w_out': w_out0,
        'na_w_qkv': na_w_qkv[0].astype(BF16), 'rpb_pad': rpb_pad, 'na_w_out': na_w_out[0].astype(BF16),
        'ffn': [(ffn_w_gate[i].astype(BF16), ffn_w_up[i].astype(BF16), ffn_conv_w[i], ffn_w_down[i].astype(BF16))
                for i in range(ffn_w_gate.shape[0])],
    }
    y_prompt, new_state, new_k, new_v = _run_stream(x_prompt, mod, 0, W, is_ctx=True)
    y_sample = _run_stream(x_sample, mod, 1, W, is_ctx=False, state_ssd=state_ssd, cache_k=cache_k, cache_v=cache_v)
    return (y_prompt, y_sample, new_state, new_k, new_v)
```

```python
import functools

import jax
import jax.numpy as jnp
from jax import lax
from jax.experimental import pallas as pl
from jax.experimental.pallas import tpu as pltpu

F32 = jnp.float32
BF16 = jnp.bfloat16

D_MODEL = 1024
EPS = 1e-6
GRID_W = 64
SSD_HEADDIM = 64
SSD_HEADS = 16
SSD_STATE = 128
SSD_GROUPS = 2
SSD_INNER = 1024
SSD_XBC = 1536
SSD_LC = 256
SC_WIDTH = 1024
NA_HEADS = 16
NA_HEAD_DIM = 64
NA_KH = 8
NA_KW = 16

LOG2E = 1.4426950408889634
LANES = 128
SUBLANES = 8
HEAD_PAIRS = 8
U_COLS = 5 * 1024 + 512

VMEM_LIMIT = 56 * 1024 * 1024

_RESIDENT = pl.BlockSpec(memory_space=pltpu.VMEM)


def _cparams(sem):
    return pltpu.CompilerParams(dimension_semantics=sem, vmem_limit_bytes=VMEM_LIMIT)


def _silu(x):
    return x * jax.nn.sigmoid(x)


def _dot(a, b):
    return jnp.dot(a, b, preferred_element_type=F32)


def _dot_nt(a, b):
    return lax.dot_general(a, b, (((1,), (1,)), ((), ())), preferred_element_type=F32)


def _split3(v):
    hi = v.astype(BF16)
    r1 = v - hi.astype(F32)
    mid = r1.astype(BF16)
    lo = (r1 - mid.astype(F32)).astype(BF16)
    return hi, mid, lo


def _sel_dot_left(sel, v):
    hi, mid, lo = _split3(v)
    return _dot(sel, hi) + _dot(sel, mid) + _dot(sel, lo)


def _norm_mod(x, g, shift, scale):
    ms = jnp.mean(x * x, axis=-1, keepdims=True)
    y = x * lax.rsqrt(ms + EPS) * g
    return y * (1.0 + scale) + shift


def _mod_row(i, tm, row_base, rows_per_group):
    return row_base + (i * tm) // rows_per_group


def _ada_kernel(c_ref, w_ref, b_ref, o_ref):
    s = _silu(c_ref[...]).astype(BF16)
    o_ref[0] = _dot(s, w_ref[0].astype(BF16)) + b_ref[0]


def _ada(cvec8, ada_w, ada_b):
    depth, d, n = ada_w.shape
    tn = 1536
    return pl.pallas_call(
        _ada_kernel,
        grid=(depth, n // tn),
        in_specs=[
            pl.BlockSpec((8, d), lambda l, j: (0, 0)),
            pl.BlockSpec((1, d, tn), lambda l, j: (l, 0, j)),
            pl.BlockSpec((1, 1, tn), lambda l, j: (l, 0, j)),
        ],
        out_specs=pl.BlockSpec((1, 8, tn), lambda l, j: (l, 0, j)),
        out_shape=jax.ShapeDtypeStruct((depth, 8, n), F32),
        compiler_params=_cparams(("arbitrary", "arbitrary")),
        name="ada_mod",
    )(cvec8, ada_w, ada_b.reshape(depth, 1, n))


def _nm_matmul_kernel(*refs, n_groups, group_width, has_extra, tm, tn, row_base, rows_per_group):
    x_ref, sh_ref, sc_ref, g_ref, w_ref = refs[:5]
    pos = 5
    if has_extra:
        we_ref = refs[pos]
        pos += 1
    o_refs = refs[pos:pos + n_groups]
    pos += n_groups
    if has_extra:
        oe_ref = refs[pos]
    r = _mod_row(pl.program_id(0), tm, row_base, rows_per_group)
    sh = sh_ref[0, pl.ds(r, 1), :]
    sc = sc_ref[0, pl.ds(r, 1), :]
    h = _norm_mod(x_ref[...], g_ref[...], sh, sc).astype(BF16)
    if has_extra:
        oe_ref[...] = _dot(h, we_ref[...])
    for gi, o_ref in enumerate(o_refs):
        for j in range(group_width // tn):
            res = _dot(h, w_ref[:, gi * group_width + j * tn:gi * group_width + (j + 1) * tn])
            o_ref[:, j * tn:(j + 1) * tn] = res.astype(o_ref.dtype)


def _nm_matmul(x, mod, layer, shift_idx, scale_idx, g, w, *, n_groups, group_width, row_base, rows_per_group,
               w_extra=None, out_dtype=F32, tm=512, tn=512):
    m, d = x.shape
    has_extra = w_extra is not None
    in_specs = [
        pl.BlockSpec((tm, d), lambda i: (i, 0)),
        pl.BlockSpec((1, 8, d), lambda i: (layer, 0, shift_idx)),
        pl.BlockSpec((1, 8, d), lambda i: (layer, 0, scale_idx)),
        pl.BlockSpec((1, d), lambda i: (0, 0)),
        _RESIDENT,
    ]
    args = [x, mod, mod, g.reshape(1, d), w]
    out_specs = [pl.BlockSpec((tm, group_width), lambda i: (i, 0)) for _ in range(n_groups)]
    out_shape = [jax.ShapeDtypeStruct((m, group_width), out_dtype) for _ in range(n_groups)]
    if has_extra:
        ne = w_extra.shape[1]
        in_specs.append(_RESIDENT)
        args.append(w_extra)
        out_specs.append(pl.BlockSpec((tm, ne), lambda i: (i, 0)))
        out_shape.append(jax.ShapeDtypeStruct((m, ne), F32))
    kern = functools.partial(_nm_matmul_kernel, n_groups=n_groups, group_width=group_width, has_extra=has_extra, tm=tm,
                             tn=tn, row_base=row_base, rows_per_group=rows_per_group)
    return pl.pallas_call(
        kern,
        grid=(m // tm,),
        in_specs=in_specs,
        out_specs=out_specs,
        out_shape=out_shape,
        compiler_params=_cparams(("arbitrary",)),
        name="norm_mod_matmul",
    )(*args)


def _dwconv3(v, w, seq_len, row0=0):
    rows = v.shape[0]
    pos = (lax.broadcasted_iota(jnp.int32, v.shape, 0) + row0) % seq_len
    prev = jnp.where(pos == 0, 0.0, pltpu.roll(v, 1, 0))
    nxt = jnp.where(pos == seq_len - 1, 0.0, pltpu.roll(v, rows - 1, 0))
    return prev * w[0:1, :] + v * w[1:2, :] + nxt * w[2:3, :]


MIX_SHIFT = 2 * SSD_HEADS
DT_COL0 = SSD_INNER + SSD_XBC
TAIL_COL0 = DT_COL0 + 3 * SC_WIDTH


def _in_proj_kernel(x_ref, sh_ref, sc_ref, g_ref, w_ref, wt_ref, cw_ref, cb_ref, scw_ref, zs_ref, x2_ref, bc_ref, yb_ref,
                    dt_ref,
                    *, tm, row_base, rows_per_group, seq_len):
    r = _mod_row(pl.program_id(0), tm, row_base, rows_per_group)
    sh = sh_ref[0, pl.ds(r, 1), :]
    sc = sc_ref[0, pl.ds(r, 1), :]
    h = _norm_mod(x_ref[...], g_ref[...], sh, sc).astype(BF16)
    tn = 256

    def proj(c0, width=tn):
        return _dot(h, w_ref[:, c0:c0 + width])

    for j in range(SSD_INNER // tn):
        cols = slice(j * tn, (j + 1) * tn)
        zs_ref[:, cols] = _silu(proj(j * tn)).astype(BF16)
        xc = _dwconv3(proj(SSD_INNER + j * tn), cw_ref[:, cols], seq_len) + cb_ref[:, cols]
        x2_ref[:, cols] = _silu(xc).astype(BF16)
    for j in range((SSD_XBC - SSD_INNER) // tn):
        cols = slice(SSD_INNER + j * tn, SSD_INNER + (j + 1) * tn)
        bcc = _dwconv3(proj(SSD_INNER + cols.start), cw_ref[:, cols], seq_len) + cb_ref[:, cols]
        bc_ref[:, j * tn:(j + 1) * tn] = _silu(bcc).astype(BF16)
    dt_ref[...] = proj(DT_COL0, LANES)
    q3 = _dot(h, wt_ref[...])
    tail = lax.broadcasted_iota(jnp.int32, (1, LANES), 1) < MIX_SHIFT
    for j in range(SC_WIDTH // tn):
        c0 = DT_COL0 + j * tn
        cols = slice(j * tn, (j + 1) * tn)
        p0, p1, p2 = proj(c0), proj(c0 + SC_WIDTH), proj(c0 + 2 * SC_WIDTH)
        y = p0 * _dwconv3(p1 * p2, scw_ref[:, cols], seq_len)
        if j == 0:
            yt = p1[:, :LANES] * _dwconv3(p2[:, :LANES] * q3, scw_ref[:, :LANES], seq_len)
            y = jnp.concatenate([jnp.where(tail, yt, y[:, :LANES]), y[:, LANES:]], axis=1)
        yb_ref[:, cols] = y.astype(BF16)


def _in_proj(x, mod, g, w, w_tail, conv_w, conv_b, sc_conv_w_rot, *, seq_len, row_base, rows_per_group, tm):
    m, d = x.shape
    kern = functools.partial(_in_proj_kernel, tm=tm, row_base=row_base, rows_per_group=rows_per_group, seq_len=seq_len)
    widths = (SSD_INNER, SSD_INNER, SSD_XBC - SSD_INNER, SC_WIDTH, LANES)
    dtypes = (BF16, BF16, BF16, BF16, F32)
    return pl.pallas_call(
        kern,
        grid=(m // tm,),
        in_specs=[
            pl.BlockSpec((tm, d), lambda i: (i, 0)),
            pl.BlockSpec((1, 8, d), lambda i: (0, 0, 0)),
            pl.BlockSpec((1, 8, d), lambda i: (0, 0, 1)),
            pl.BlockSpec((1, d), lambda i: (0, 0)),
            _RESIDENT,
            _RESIDENT,
            pl.BlockSpec((3, SSD_XBC), lambda i: (0, 0)),
            pl.BlockSpec((1, SSD_XBC), lambda i: (0, 0)),
            pl.BlockSpec((3, SC_WIDTH), lambda i: (0, 0)),
        ],
        out_specs=[pl.BlockSpec((tm, wd), lambda i: (i, 0)) for wd in widths],
        out_shape=[jax.ShapeDtypeStruct((m, wd), dt) for wd, dt in zip(widths, dtypes)],
        compiler_params=_cparams(("arbitrary",)),
        name="in_proj",
    )(x, mod, mod, g.reshape(1, d), w, w_tail, conv_w, conv_b, sc_conv_w_rot)


def _ffn_kernel(*refs, n_parts, tm, tf, row_base, rows_per_group, seq_len, final_norm):
    x_ref, gate1_ref, sh_ref, sc_ref, gate_ref, g_ref, wo_ref = refs[:7]
    a_refs = refs[7:7 + n_parts]
    wg_ref, wu_ref, cw_ref, wd_ref, fg_ref, o_ref, h_ref, act_ref = refs[7 + n_parts:]
    r = _mod_row(pl.program_id(0), tm, row_base, rows_per_group)
    sh = sh_ref[0, pl.ds(r, 1), :]
    sc = sc_ref[0, pl.ds(r, 1), :]
    gate = gate_ref[0, pl.ds(r, 1), :]
    kp = wo_ref.shape[0] // n_parts
    mix = _dot(a_refs[0][...], wo_ref[0:kp, :])
    for k in range(1, n_parts):
        mix = mix + _dot(a_refs[k][...], wo_ref[k * kp:(k + 1) * kp, :])
    x1 = x_ref[...] + gate1_ref[0, pl.ds(r, 1), :] * mix
    o_ref[...] = x1
    h_ref[...] = _norm_mod(x1, g_ref[...], sh, sc).astype(BF16)
    dff = wg_ref.shape[1]
    half = tm // 2
    win = half + SUBLANES
    for hb in range(2):
        w0 = hb * (half - SUBLANES)
        v0 = hb * SUBLANES
        rows = slice(hb * half, (hb + 1) * half)
        hw = h_ref[w0:w0 + win, :]
        hv = h_ref[rows, :]
        for fc in range(dff // tf):
            cols = slice(fc * tf, (fc + 1) * tf)
            gpre = _dwconv3(_dot(hw, wg_ref[:, cols]), cw_ref[:, cols], seq_len, row0=w0)[v0:v0 + half]
            up = _dot(hv, wu_ref[:, cols])
            act_ref[:, cols] = (jax.nn.gelu(gpre, approximate=True) * up).astype(BF16)
        y = o_ref[rows, :] + gate * _dot(act_ref[...], wd_ref[...])
        if final_norm:
            ms = jnp.mean(y * y, axis=-1, keepdims=True)
            y = y * lax.rsqrt(ms + EPS) * fg_ref[...]
        o_ref[rows, :] = y


def _layer_weight(shape):
    return lambda layer: pl.BlockSpec((None,) + shape, lambda i: (layer,) + (0,) * len(shape),
                                      pipeline_mode=pl.Buffered(1))


def _mix_ffn(x, mod, layer, parts, wo, g, wg, wu, cw, wd, final_g, *, row_base, rows_per_group, seq_len, final_norm,
             tm=1024, tf=256):
    m, d = x.shape
    dff = wg.shape[2]
    n_parts = len(parts)
    kern = functools.partial(_ffn_kernel, n_parts=n_parts, tm=tm, tf=tf, row_base=row_base,
                             rows_per_group=rows_per_group, seq_len=seq_len, final_norm=final_norm)
    mod_spec = lambda k: pl.BlockSpec((1, 8, d), lambda i: (layer, 0, k))
    in_specs = [pl.BlockSpec((tm, d), lambda i: (i, 0)), mod_spec(2), mod_spec(3), mod_spec(4), mod_spec(5),
                pl.BlockSpec((1, d), lambda i: (0, 0)), _RESIDENT]
    in_specs += [pl.BlockSpec((tm, a.shape[1]), lambda i: (i, 0)) for a in parts]
    in_specs += [_layer_weight((d, dff))(layer), _layer_weight((d, dff))(layer), _layer_weight((3, dff))(layer),
                 _layer_weight((dff, d))(layer), pl.BlockSpec((1, d), lambda i: (0, 0))]
    return pl.pallas_call(
        kern,
        grid=(m // tm,),
        in_specs=in_specs,
        out_specs=pl.BlockSpec((tm, d), lambda i: (i, 0)),
        out_shape=jax.ShapeDtypeStruct((m, d), F32),
        scratch_shapes=[pltpu.VMEM((tm, d), BF16), pltpu.VMEM((tm // 2, dff), BF16)],
        compiler_params=_cparams(("arbitrary",)),
        name="mix_ffn",
    )(x, mod, mod, mod, mod, g.reshape(1, d), wo, *parts, wg, wu, cw, wd, final_g.reshape(1, d))


def _pair_cols(first_half, arr, h0, h1):
    return jnp.where(first_half, arr[:, h0:h0 + 1], arr[:, h1:h1 + 1])


def _pair_rows(arr_t, h0, h1, width):
    return jnp.concatenate([jnp.broadcast_to(arr_t[h0:h0 + 1, :], (SSD_HEADDIM, width)),
                            jnp.broadcast_to(arr_t[h1:h1 + 1, :], (SSD_HEADDIM, width))], axis=0)


def _pair_scalars(row, h0, h1):
    return jnp.concatenate([jnp.broadcast_to(row[:, h0:h0 + 1], (SSD_HEADDIM, SSD_STATE)),
                            jnp.broadcast_to(row[:, h1:h1 + 1], (SSD_HEADDIM, SSD_STATE))], axis=0)


def _ssd_kernel(*refs, seq_len, has_h0, emit_state):
    zs_ref, x2_ref, bc_ref, dt_ref, prm_ref, ng_ref = refs[:6]
    pos = 6
    if has_h0:
        h0_ref = refs[pos]
        pos += 1
    ya_ref = refs[pos]
    pos += 1
    if emit_state:
        st_ref = refs[pos]
        pos += 1
    yacc_ref = refs[pos]

    lc = SSD_LC
    nchunk = seq_len // lc
    nh = SSD_HEADS
    lane = lax.broadcasted_iota(jnp.int32, (1, LANES), 1)
    first_half = lane < SSD_HEADDIM
    fwd_lane = lane < nh

    bc = bc_ref[0]
    valid = lane < 2 * nh
    dt = jnp.where(valid, jax.nn.softplus(dt_ref[0] + prm_ref[0:1, :]), 0.0)
    a = -jnp.exp(prm_ref[1:2, :])
    dta = jnp.where(valid, dt * a, 0.0)
    dsum = prm_ref[2:3, :] + prm_ref[3:4, :]
    tt = lax.broadcasted_iota(jnp.int32, (lc, lc), 0)
    ss = lax.broadcasted_iota(jnp.int32, (lc, lc), 1)
    tril = jnp.where(ss <= tt, 1.0, 0.0).astype(BF16)
    triu = jnp.where(ss >= tt, 1.0, 0.0).astype(BF16)
    nq = lc // LANES
    td = lax.broadcasted_iota(jnp.int32, (LANES, LANES), 0)
    sd = lax.broadcasted_iota(jnp.int32, (LANES, LANES), 1)
    lower_d = td > sd
    upper_d = td < sd

    acs2_l, rowt_l, wts_l, eacs_l, edec_l, cbm_l, bm_l, cm_l = [], [], [], [], [], [], [], []
    for c in range(nchunk):
        rows = slice(c * lc, (c + 1) * lc)
        acs = jnp.where(fwd_lane, _sel_dot_left(tril, dta[rows]), _sel_dot_left(triu, dta[rows]))
        edge = jnp.where(fwd_lane, acs[lc - 1:lc, :], acs[0:1, :])
        acs2 = acs * LOG2E
        dtt = dt[rows].T
        rowt = acs2.T - jnp.log2(dtt)
        dgt = jnp.log2(dtt[0:nh, :] + dtt[nh:2 * nh, :])
        acs2_l.append(acs2)
        rowt_l.append((rowt, dgt))
        wts_l.append((dt[rows] * jnp.exp(edge - acs)).T)
        eacs_l.append(jnp.exp(acs))
        edec_l.append(jnp.exp(edge))
        cbm_g, bm_g, cm_g = [], [], []
        for g in range(SSD_GROUPS):
            bm = bc[rows, g * SSD_STATE:(g + 1) * SSD_STATE]
            cm = bc[rows, (SSD_GROUPS + g) * SSD_STATE:(SSD_GROUPS + g + 1) * SSD_STATE]
            bm_g.append(bm)
            cm_g.append(cm)
            cbm_g.append(_dot_nt(cm, bm))
        cbm_l.append(cbm_g)
        bm_l.append(bm_g)
        cm_l.append(cm_g)

    for p in range(HEAD_PAIRS):
        cols = slice(p * LANES, (p + 1) * LANES)
        g = p // (HEAD_PAIRS // SSD_GROUPS)
        hd = (2 * p, 2 * p + 1)
        x2b = x2_ref[0, :, cols]
        x2 = x2b.astype(F32)
        dsum2 = _pair_cols(first_half, dsum, hd[0], hd[1])
        if has_h0:
            hf = h0_ref[0, 0, p]
            hb = h0_ref[0, 1, p]
        else:
            hf = jnp.zeros((LANES, SSD_STATE), F32)
            hb = jnp.zeros((LANES, SSD_STATE), F32)
        ys = []
        stbs = []
        for c in range(nchunk):
            rows = slice(c * lc, (c + 1) * lc)
            x2c = x2[rows]
            x2cb = x2b[rows]
            acs2 = acs2_l[c]
            rowt, dgt = rowt_l[c]
            cbm = cbm_l[c][g]
            outs = []
            for h in hd:
                blocks = []
                for bi in range(nq):
                    ri = slice(bi * LANES, (bi + 1) * LANES)
                    cf = acs2[ri, h:h + 1]
                    cbk = acs2[ri, nh + h:nh + h + 1]
                    row_blocks = []
                    for bj in range(nq):
                        cj = slice(bj * LANES, (bj + 1) * LANES)
                        rf = rowt[h:h + 1, cj]
                        rb = rowt[nh + h:nh + h + 1, cj]
                        if bi > bj:
                            arg = cf - rf
                        elif bi < bj:
                            arg = cbk - rb
                        else:
                            arg = jnp.where(lower_d, cf - rf, jnp.where(upper_d, cbk - rb, dgt[h:h + 1, cj]))
                        row_blocks.append((cbm[ri, cj] * jnp.exp2(arg)).astype(BF16))
                    blocks.append(jnp.concatenate(row_blocks, axis=1))
                gm = jnp.concatenate(blocks, axis=0)
                outs.append(_dot(gm, x2cb))
            y = jnp.where(first_half, outs[0], outs[1]) + x2c * dsum2
            xt = x2c.T
            xwf = (xt * _pair_rows(wts_l[c], hd[0], hd[1], lc)).astype(BF16)
            xwb = (xt * _pair_rows(wts_l[c], nh + hd[0], nh + hd[1], lc)).astype(BF16)
            stf = _dot(xwf, bm_l[c][g])
            stbs.append(_dot(xwb, bm_l[c][g]))
            if has_h0 or c > 0:
                y = y + _dot_nt(cm_l[c][g], hf.astype(BF16)) * _pair_cols(first_half, eacs_l[c], hd[0], hd[1])
            hf = hf * _pair_scalars(edec_l[c], hd[0], hd[1]) + stf
            ys.append(y)
        for c in reversed(range(nchunk)):
            if has_h0 or c < nchunk - 1:
                ys[c] = ys[c] + (_dot_nt(cm_l[c][g], hb.astype(BF16))
                                 * _pair_cols(first_half, eacs_l[c], nh + hd[0], nh + hd[1]))
            hb = hb * _pair_scalars(edec_l[c], nh + hd[0], nh + hd[1]) + stbs[c]
        zg = zs_ref[0, :, cols].astype(F32)
        for c in range(nchunk):
            rows = slice(c * lc, (c + 1) * lc)
            yacc_ref[rows, cols] = ys[c] * zg[rows]
        if emit_state:
            st_ref[0, 0, p] = hf
            st_ref[0, 1, p] = hb

    y = yacc_ref[...]
    ms = jnp.mean(y * y, axis=-1, keepdims=True)
    ya_ref[0] = (y * lax.rsqrt(ms + EPS) * ng_ref[...]).astype(BF16)


def _ssd(zs3, x23, bc3, dt3, prm, norm_g, h0, *, emit_state):
    b, s, _ = zs3.shape
    has_h0 = h0 is not None
    in_specs = [
        pl.BlockSpec((1, s, SSD_INNER), lambda i: (i, 0, 0)),
        pl.BlockSpec((1, s, SSD_INNER), lambda i: (i, 0, 0)),
        pl.BlockSpec((1, s, SSD_XBC - SSD_INNER), lambda i: (i, 0, 0)),
        pl.BlockSpec((1, s, LANES), lambda i: (i, 0, 0)),
        pl.BlockSpec((8, LANES), lambda i: (0, 0)),
        pl.BlockSpec((1, SSD_INNER), lambda i: (0, 0)),
    ]
    args = [zs3, x23, bc3, dt3, prm, norm_g]
    state_block = (1, 2, HEAD_PAIRS, LANES, SSD_STATE)
    if has_h0:
        in_specs.append(pl.BlockSpec(state_block, lambda i: (i, 0, 0, 0, 0)))
        args.append(h0)
    out_specs = [pl.BlockSpec((1, s, SSD_INNER), lambda i: (i, 0, 0))]
    out_shape = [jax.ShapeDtypeStruct((b, s, SSD_INNER), BF16)]
    if emit_state:
        out_specs.append(pl.BlockSpec(state_block, lambda i: (i, 0, 0, 0, 0)))
        out_shape.append(jax.ShapeDtypeStruct((b, 2, HEAD_PAIRS, LANES, SSD_STATE), F32))
    kern = functools.partial(_ssd_kernel, seq_len=s, has_h0=has_h0, emit_state=emit_state)
    return pl.pallas_call(
        kern,
        grid=(b,),
        in_specs=in_specs,
        out_specs=out_specs,
        out_shape=out_shape,
        scratch_shapes=[pltpu.VMEM((s, SSD_INNER), F32)],
        compiler_params=_cparams(("arbitrary",)),
        name="ssd",
    )(*args)


def _pair_masks():
    lane = lax.broadcasted_iota(jnp.int32, (1, LANES), 1)
    return lane < NA_HEAD_DIM


def _qkv_ctx_attn_kernel(x_ref, sh_ref, sc_ref, g_ref, w_ref, o_ref, nk_ref, nv_ref, qkv_ref,
                         *, tm, seq_len, row_base, rows_per_group):
    r = _mod_row(pl.program_id(0), tm, row_base, rows_per_group)
    sh = sh_ref[0, pl.ds(r, 1), :]
    sc = sc_ref[0, pl.ds(r, 1), :]
    h = _norm_mod(x_ref[...], g_ref[...], sh, sc).astype(BF16)
    d = x_ref.shape[1]
    tn = 512
    for j in range(3 * d // tn):
        qkv_ref[:, j * tn:(j + 1) * tn] = _dot(h, w_ref[:, j * tn:(j + 1) * tn])
    first_half = _pair_masks()
    scale = NA_HEAD_DIM ** -0.5
    s = seq_len
    for b in range(tm // s):
        rows = slice(b * s, (b + 1) * s)
        nk_ref[b, 0] = qkv_ref[rows, d:2 * d].reshape(s, NA_HEADS, NA_HEAD_DIM)
        nv_ref[b, 0] = qkv_ref[rows, 2 * d:3 * d].reshape(s, NA_HEADS, NA_HEAD_DIM)
        for p in range(HEAD_PAIRS):
            cols = slice(p * LANES, (p + 1) * LANES)
            q2 = qkv_ref[rows, cols] * scale
            kb = qkv_ref[rows, d + p * LANES:d + (p + 1) * LANES].astype(BF16)
            vb = qkv_ref[rows, 2 * d + p * LANES:2 * d + (p + 1) * LANES].astype(BF16)
            qs = jnp.concatenate([jnp.where(first_half, q2, 0.0), jnp.where(first_half, 0.0, q2)], axis=0).astype(BF16)
            sco = _dot_nt(qs, kb)
            e = jnp.exp(sco - jnp.max(sco, axis=-1, keepdims=True))
            pv = _dot(e.astype(BF16), vb) / jnp.sum(e, axis=-1, keepdims=True)
            o_ref[rows, cols] = jnp.where(first_half, pv[:s], pv[s:]).astype(BF16)


def _qkv_ctx_attn(x, mod, layer, g, w, *, seq_len, row_base, rows_per_group, tm=512):
    m, d = x.shape
    nb = tm // seq_len
    cache_block = (nb, 1, seq_len, NA_HEADS, NA_HEAD_DIM)
    cache_shape = jax.ShapeDtypeStruct((m // seq_len, 1, seq_len, NA_HEADS, NA_HEAD_DIM), F32)
    kern = functools.partial(_qkv_ctx_attn_kernel, tm=tm, seq_len=seq_len, row_base=row_base,
                             rows_per_group=rows_per_group)
    return pl.pallas_call(
        kern,
        grid=(m // tm,),
        in_specs=[
            pl.BlockSpec((tm, d), lambda i: (i, 0)),
            pl.BlockSpec((1, 8, d), lambda i: (layer, 0, 0)),
            pl.BlockSpec((1, 8, d), lambda i: (layer, 0, 1)),
            pl.BlockSpec((1, d), lambda i: (0, 0)),
            _RESIDENT,
        ],
        out_specs=[
            pl.BlockSpec((tm, d), lambda i: (i, 0)),
            pl.BlockSpec(cache_block, lambda i: (i, 0, 0, 0, 0)),
            pl.BlockSpec(cache_block, lambda i: (i, 0, 0, 0, 0)),
        ],
        out_shape=[jax.ShapeDtypeStruct((m, d), BF16), cache_shape, cache_shape],
        scratch_shapes=[pltpu.VMEM((tm, 3 * d), F32)],
        compiler_params=_cparams(("arbitrary",)),
        name="qkv_ctx_attention",
    )(x, mod, mod, g.reshape(1, d), w)


def _lat_attn_kernel(q_ref, k_ref, v_ref, kc_ref, vc_ref, rpb_ref, o_ref, bias_ref, *, rows, q_tile):
    b = pl.program_id(1)
    first_half = _pair_masks()
    n_tok = rows * GRID_W
    neg_inf = -jnp.inf

    @pl.when(b == 0)
    def _():
        qc = lax.broadcasted_iota(jnp.int32, (GRID_W, LANES), 0)
        kc = lax.broadcasted_iota(jnp.int32, (GRID_W, LANES), 1)
        cs = jnp.clip(qc - NA_KW // 2, 0, GRID_W - NA_KW)
        col_ok = (kc >= cs) & (kc < cs + NA_KW)
        kh = min(NA_KH, rows)
        for hh in range(2):
            bias_ref[hh] = jnp.full((n_tok, n_tok), neg_inf, F32)
            for dr in range(2 * NA_KH - 1):
                v = jnp.broadcast_to(rpb_ref[hh, dr:dr + 1, :], (GRID_W, LANES))
                t = pltpu.roll(v, LANES - (NA_KW - 1), 1, stride=1, stride_axis=0)
                tile = jnp.where(col_ok, t, neg_inf)[:, :GRID_W]
                for r in range(rows):
                    start = min(max(r - kh // 2, 0), rows - kh)
                    i = dr - (NA_KH - 1) + r - start
                    if 0 <= i < kh:
                        kr = start + i
                        bias_ref[hh, r * GRID_W:(r + 1) * GRID_W, kr * GRID_W:(kr + 1) * GRID_W] = tile

    scale = NA_HEAD_DIM ** -0.5
    kb = k_ref[...].astype(BF16)
    vb = v_ref[...].astype(BF16)
    kcb = kc_ref[...].astype(BF16)
    vcb = vc_ref[...].astype(BF16)
    for qt in range(n_tok // q_tile):
        qrows = slice(qt * q_tile, (qt + 1) * q_tile)
        q2 = q_ref[qrows, :] * scale
        outs = []
        for hh in range(2):
            qm = (jnp.where(first_half, q2, 0.0) if hh == 0 else jnp.where(first_half, 0.0, q2)).astype(BF16)
            s_loc = _dot_nt(qm, kb) + bias_ref[hh, qrows, :]
            s_ctx = _dot_nt(qm, kcb)
            mx = jnp.maximum(jnp.max(s_loc, axis=-1, keepdims=True), jnp.max(s_ctx, axis=-1, keepdims=True))
            e_loc = jnp.exp(s_loc - mx)
            e_ctx = jnp.exp(s_ctx - mx)
            den = jnp.sum(e_loc, axis=-1, keepdims=True) + jnp.sum(e_ctx, axis=-1, keepdims=True)
            outs.append((_dot(e_loc.astype(BF16), vb) + _dot(e_ctx.astype(BF16), vcb)) / den)
        o_ref[qrows, :] = jnp.where(first_half, outs[0], outs[1]).astype(BF16)


def _lat_attn(q, k, v, kc, vc, rpb_pad, *, n_batch, n_tok, n_ctx, q_tile=256):
    d = q.shape[1]
    rows = n_tok // GRID_W
    tok_spec = pl.BlockSpec((n_tok, LANES), lambda p, b: (b, p))
    ctx_spec = pl.BlockSpec((n_ctx, LANES), lambda p, b: (b, p))
    kern = functools.partial(_lat_attn_kernel, rows=rows, q_tile=q_tile)
    return pl.pallas_call(
        kern,
        grid=(HEAD_PAIRS, n_batch),
        in_specs=[tok_spec, tok_spec, tok_spec, ctx_spec, ctx_spec,
                  pl.BlockSpec((2, 2 * NA_KH, LANES), lambda p, b: (p, 0, 0))],
        out_specs=tok_spec,
        out_shape=jax.ShapeDtypeStruct((n_batch * n_tok, d), BF16),
        scratch_shapes=[pltpu.VMEM((2, n_tok, n_tok), F32)],
        compiler_params=_cparams(("arbitrary", "arbitrary")),
        name="latent_attention",
    )(q, k, v, kc, vc, rpb_pad)


def _pad_lanes(v):
    return jnp.pad(v, (0, LANES - v.shape[0]))


def _run_stream(x3, mod, row_base, W, *, is_ctx, state_ssd=None, cache_k=None, cache_v=None):
    b, s, d = x3.shape
    m = b * s
    x = x3.reshape(m, d)
    rpg = m if is_ctx else s
    kw = dict(row_base=row_base, rows_per_group=rpg)

    zs, x2, bca, yb, dtr = _in_proj(x, mod, W['norm_mix_g'][0], W['w_in'], W['w_in_tail'], W['ssd_conv_w'], W['ssd_conv_b'],
                                    W['sc_conv_w_rot'], seq_len=s, tm=max(s, 512), **kw)
    h0 = None
    if not is_ctx:
        h0 = state_ssd[:, 0].reshape(b, 2, HEAD_PAIRS, LANES, SSD_STATE)
    res = _ssd(zs.reshape(b, s, -1), x2.reshape(b, s, -1), bca.reshape(b, s, -1), dtr.reshape(b, s, LANES),
               W['ssd_prm'], W['ssd_norm_g'], h0, emit_state=is_ctx)
    ya = res[0].reshape(m, SSD_INNER)
    x = _mix_ffn(x, mod, 0, [ya, yb], W['mix0_w_out'], W['norm_ffn_g'][0], *W['ffn'], W['final_norm_g'], seq_len=s,
                 final_norm=False, **kw)

    if is_ctx:
        o, new_k, new_v = _qkv_ctx_attn(x, mod, 1, W['norm_mix_g'][1], W['na_w_qkv'], seq_len=s, **kw)
    else:
        q, k, v = _nm_matmul(x, mod, 1, 0, 1, W['norm_mix_g'][1], W['na_w_qkv'], n_groups=3, group_width=D_MODEL, **kw)
        n_ctx = cache_k.shape[2]
        kc = cache_k[:, 0].reshape(b * n_ctx, d)
        vc = cache_v[:, 0].reshape(b * n_ctx, d)
        o = _lat_attn(q, k, v, kc, vc, W['rpb_pad'], n_batch=b, n_tok=s, n_ctx=n_ctx)
    x = _mix_ffn(x, mod, 1, [o], W['na_w_out'], W['norm_ffn_g'][1], *W['ffn'], W['final_norm_g'], seq_len=s,
                 final_norm=True, **kw)
    y = x.reshape(b, s, d)
    if is_ctx:
        new_state = res[1].reshape(b, 1, 2, SSD_HEADS, SSD_HEADDIM, SSD_STATE)
        return y, new_state, new_k, new_v
    return y


def kernel(x_prompt, x_sample, state_ssd, cache_k, cache_v, c, c_ctx, ada_w, ada_b, norm_mix_g, norm_ffn_g, ssd_w_in,
           ssd_conv_w, ssd_conv_b, ssd_dt_bias, ssd_a_log, ssd_d, ssd_norm_g, sc_conv_w, mix0_w_out, na_w_qkv, na_rpb,
           na_w_out, ffn_w_gate, ffn_w_up, ffn_conv_w, ffn_w_down, final_norm_g):
    n_lat = x_sample.shape[0]
    cvec = jnp.concatenate([c_ctx[None, :], c, jnp.zeros((8 - 1 - n_lat, D_MODEL), F32)], axis=0)
    mod = _ada(cvec, ada_w, ada_b)

    w_in = ssd_w_in[0].astype(BF16)
    w_in_tail = jnp.pad(ssd_w_in[0][:, TAIL_COL0:], ((0, 0), (0, LANES - MIX_SHIFT))).astype(BF16)
    w_out0 = mix0_w_out[0]
    w_out0 = jnp.concatenate([w_out0[:SSD_INNER], jnp.roll(w_out0[SSD_INNER:], MIX_SHIFT, axis=0)], axis=0).astype(BF16)
    prm =jnp.stack([_pad_lanes(ssd_dt_bias[0].reshape(-1)), _pad_lanes(ssd_a_log[0].reshape(-1)),
                     _pad_lanes(ssd_d[0, 0]), _pad_lanes(ssd_d[0, 1])] + [jnp.zeros((LANES,), F32)] * 4, axis=0)
    rpb = na_rpb[0]
    rpb_pad = jnp.pad(rpb, ((0, 0), (0, 1), (0, LANES - rpb.shape[2])))
    W = {
        'norm_mix_g': norm_mix_g, 'norm_ffn_g': norm_ffn_g, 'final_norm_g': final_norm_g,
        'w_in': w_in, 'w_in_tail': w_in_tail,
        'ssd_conv_w': ssd_conv_w[0], 'ssd_conv_b': ssd_conv_b[0].reshape(1, SSD_XBC), 'ssd_prm': prm,
        'ssd_norm_g': ssd_norm_g[0].reshape(1, SSD_INNER), 'sc_conv_w_rot': jnp.roll(sc_conv_w[0], MIX_SHIFT, axis=1),
        'mix0_w_out': w_out0,
        'na_w_qkv': na_w_qkv[0].astype(BF16), 'rpb_pad': rpb_pad, 'na_w_out': na_w_out[0].astype(BF16),
        'ffn': (ffn_w_gate.astype(BF16), ffn_w_up.astype(BF16), ffn_conv_w, ffn_w_down.astype(BF16)),
    }
    y_prompt, new_state, new_k, new_v = _run_stream(x_prompt, mod, 0, W, is_ctx=True)
    y_sample = _run_stream(x_sample, mod, 1, W, is_ctx=False, state_ssd=state_ssd, cache_k=cache_k, cache_v=cache_v)
    return (y_prompt, y_sample, new_state, new_k, new_v)
```

```python
import functools

import jax
import jax.numpy as jnp
from jax import lax
from jax.experimental import pallas as pl
from jax.experimental.pallas import tpu as pltpu

F32 = jnp.float32
BF16 = jnp.bfloat16

D_MODEL = 1024
EPS = 1e-6
GRID_W = 64
SSD_HEADDIM = 64
SSD_HEADS = 16
SSD_STATE = 128
SSD_GROUPS = 2
SSD_INNER = 1024
SSD_XBC = 1536
SSD_LC = 256
SC_WIDTH = 1024
NA_HEADS = 16
NA_HEAD_DIM = 64
NA_KH = 8
NA_KW = 16

LOG2E = 1.4426950408889634
LANES = 128
SUBLANES = 8
HEAD_PAIRS = 8
U_COLS = 5 * 1024 + 512

VMEM_LIMIT = 56 * 1024 * 1024

_RESIDENT = pl.BlockSpec(memory_space=pltpu.VMEM)


def _cparams(sem):
    return pltpu.CompilerParams(dimension_semantics=sem, vmem_limit_bytes=VMEM_LIMIT)


def _silu(x):
    return x * jax.nn.sigmoid(x)


def _dot(a, b):
    return jnp.dot(a, b, preferred_element_type=F32)


def _dot_nt(a, b):
    return lax.dot_general(a, b, (((1,), (1,)), ((), ())), preferred_element_type=F32)


def _split3(v):
    hi = v.astype(BF16)
    r1 = v - hi.astype(F32)
    mid = r1.astype(BF16)
    lo = (r1 - mid.astype(F32)).astype(BF16)
    return hi, mid, lo


def _sel_dot_left(sel, v):
    hi, mid, lo = _split3(v)
    return _dot(sel, hi) + _dot(sel, mid) + _dot(sel, lo)


def _norm_mod(x, g, shift, scale):
    ms = jnp.mean(x * x, axis=-1, keepdims=True)
    y = x * lax.rsqrt(ms + EPS) * g
    return y * (1.0 + scale) + shift


def _mod_row(i, tm, row_base, rows_per_group):
    return row_base + (i * tm) // rows_per_group


def _ada_kernel(c_ref, w_ref, b_ref, o_ref):
    s = _silu(c_ref[...]).astype(BF16)
    o_ref[0] = _dot(s, w_ref[0].astype(BF16)) + b_ref[0]


def _ada(cvec8, ada_w, ada_b):
    depth, d, n = ada_w.shape
    tn = 1536
    return pl.pallas_call(
        _ada_kernel,
        grid=(depth, n // tn),
        in_specs=[
            pl.BlockSpec((8, d), lambda l, j: (0, 0)),
            pl.BlockSpec((1, d, tn), lambda l, j: (l, 0, j)),
            pl.BlockSpec((1, 1, tn), lambda l, j: (l, 0, j)),
        ],
        out_specs=pl.BlockSpec((1, 8, tn), lambda l, j: (l, 0, j)),
        out_shape=jax.ShapeDtypeStruct((depth, 8, n), F32),
        compiler_params=_cparams(("arbitrary", "arbitrary")),
        name="ada_mod",
    )(cvec8, ada_w, ada_b.reshape(depth, 1, n))


def _nm_matmul_kernel(*refs, n_groups, group_width, has_extra, tm, tn, row_base, rows_per_group):
    x_ref, sh_ref, sc_ref, g_ref, w_ref = refs[:5]
    pos = 5
    if has_extra:
        we_ref = refs[pos]
        pos += 1
    o_refs = refs[pos:pos + n_groups]
    pos += n_groups
    if has_extra:
        oe_ref = refs[pos]
    r = _mod_row(pl.program_id(0), tm, row_base, rows_per_group)
    sh = sh_ref[0, pl.ds(r, 1), :]
    sc = sc_ref[0, pl.ds(r, 1), :]
    h = _norm_mod(x_ref[...], g_ref[...], sh, sc).astype(BF16)
    if has_extra:
        oe_ref[...] = _dot(h, we_ref[...])
    for gi, o_ref in enumerate(o_refs):
        for j in range(group_width // tn):
            res = _dot(h, w_ref[:, gi * group_width + j * tn:gi * group_width + (j + 1) * tn])
            o_ref[:, j * tn:(j + 1) * tn] = res.astype(o_ref.dtype)


def _nm_matmul(x, mod, layer, shift_idx, scale_idx, g, w, *, n_groups, group_width, row_base, rows_per_group,
               w_extra=None, out_dtype=F32, tm=512, tn=512):
    m, d = x.shape
    has_extra = w_extra is not None
    in_specs = [
        pl.BlockSpec((tm, d), lambda i: (i, 0)),
        pl.BlockSpec((1, 8, d), lambda i: (layer, 0, shift_idx)),
        pl.BlockSpec((1, 8, d), lambda i: (layer, 0, scale_idx)),
        pl.BlockSpec((1, d), lambda i: (0, 0)),
        _RESIDENT,
    ]
    args = [x, mod, mod, g.reshape(1, d), w]
    out_specs = [pl.BlockSpec((tm, group_width), lambda i: (i, 0)) for _ in range(n_groups)]
    out_shape = [jax.ShapeDtypeStruct((m, group_width), out_dtype) for _ in range(n_groups)]
    if has_extra:
        ne = w_extra.shape[1]
        in_specs.append(_RESIDENT)
        args.append(w_extra)
        out_specs.append(pl.BlockSpec((tm, ne), lambda i: (i, 0)))
        out_shape.append(jax.ShapeDtypeStruct((m, ne), F32))
    kern = functools.partial(_nm_matmul_kernel, n_groups=n_groups, group_width=group_width, has_extra=has_extra, tm=tm,
                             tn=tn, row_base=row_base, rows_per_group=rows_per_group)
    return pl.pallas_call(
        kern,
        grid=(m // tm,),
        in_specs=in_specs,
        out_specs=out_specs,
        out_shape=out_shape,
        compiler_params=_cparams(("arbitrary",)),
        name="norm_mod_matmul",
    )(*args)


def _dwconv3(v, w, seq_len, row0=0):
    rows = v.shape[0]
    pos = (lax.broadcasted_iota(jnp.int32, v.shape, 0) + row0) % seq_len
    prev = jnp.where(pos == 0, 0.0, pltpu.roll(v, 1, 0))
    nxt = jnp.where(pos == seq_len - 1, 0.0, pltpu.roll(v, rows - 1, 0))
    return prev * w[0:1, :] + v * w[1:2, :] + nxt * w[2:3, :]


MIX_SHIFT = 2 * SSD_HEADS
DT_COL0 = SSD_INNER + SSD_XBC
TAIL_COL0 = DT_COL0 + 3 * SC_WIDTH


def _in_proj_kernel(x_ref, sh_ref, sc_ref, g_ref, w_ref, wt_ref, cw_ref, cb_ref, scw_ref, zs_ref, x2_ref, bc_ref, yb_ref,
                    dt_ref,
                    *, tm, row_base, rows_per_group, seq_len):
    r = _mod_row(pl.program_id(0), tm, row_base, rows_per_group)
    sh = sh_ref[0, pl.ds(r, 1), :]
    sc = sc_ref[0, pl.ds(r, 1), :]
    h = _norm_mod(x_ref[...], g_ref[...], sh, sc).astype(BF16)
    tn = 256

    def proj(c0, width=tn):
        return _dot(h, w_ref[:, c0:c0 + width])

    for j in range(SSD_INNER // tn):
        cols = slice(j * tn, (j + 1) * tn)
        zs_ref[:, cols] = _silu(proj(j * tn)).astype(BF16)
        xc = _dwconv3(proj(SSD_INNER + j * tn), cw_ref[:, cols], seq_len) + cb_ref[:, cols]
        x2_ref[:, cols] = _silu(xc).astype(BF16)
    for j in range((SSD_XBC - SSD_INNER) // tn):
        cols = slice(SSD_INNER + j * tn, SSD_INNER + (j + 1) * tn)
        bcc = _dwconv3(proj(SSD_INNER + cols.start), cw_ref[:, cols], seq_len) + cb_ref[:, cols]
        bc_ref[:, j * tn:(j + 1) * tn] = _silu(bcc).astype(BF16)
    dt_ref[...] = proj(DT_COL0, LANES)
    q3 = _dot(h, wt_ref[...])
    tail = lax.broadcasted_iota(jnp.int32, (1, LANES), 1) < MIX_SHIFT
    for j in range(SC_WIDTH // tn):
        c0 = DT_COL0 + j * tn
        cols = slice(j * tn, (j + 1) * tn)
        p0, p1, p2 = proj(c0), proj(c0 + SC_WIDTH), proj(c0 + 2 * SC_WIDTH)
        y = p0 * _dwconv3(p1 * p2, scw_ref[:, cols], seq_len)
        if j == 0:
            yt = p1[:, :LANES] * _dwconv3(p2[:, :LANES] * q3, scw_ref[:, :LANES], seq_len)
            y = jnp.concatenate([jnp.where(tail, yt, y[:, :LANES]), y[:, LANES:]], axis=1)
        yb_ref[:, cols] = y.astype(BF16)


def _in_proj(x, mod, g, w, w_tail, conv_w, conv_b, sc_conv_w_rot, *, seq_len, row_base, rows_per_group, tm):
    m, d = x.shape
    kern = functools.partial(_in_proj_kernel, tm=tm, row_base=row_base, rows_per_group=rows_per_group, seq_len=seq_len)
    widths = (SSD_INNER, SSD_INNER, SSD_XBC - SSD_INNER, SC_WIDTH, LANES)
    dtypes = (BF16, BF16, BF16, BF16, F32)
    return pl.pallas_call(
        kern,
        grid=(m // tm,),
        in_specs=[
            pl.BlockSpec((tm, d), lambda i: (i, 0)),
            pl.BlockSpec((1, 8, d), lambda i: (0, 0, 0)),
            pl.BlockSpec((1, 8, d), lambda i: (0, 0, 1)),
            pl.BlockSpec((1, d), lambda i: (0, 0)),
            _RESIDENT,
            _RESIDENT,
            pl.BlockSpec((3, SSD_XBC), lambda i: (0, 0)),
            pl.BlockSpec((1, SSD_XBC), lambda i: (0, 0)),
            pl.BlockSpec((3, SC_WIDTH), lambda i: (0, 0)),
        ],
        out_specs=[pl.BlockSpec((tm, wd), lambda i: (i, 0)) for wd in widths],
        out_shape=[jax.ShapeDtypeStruct((m, wd), dt) for wd, dt in zip(widths, dtypes)],
        compiler_params=_cparams(("arbitrary",)),
        name="in_proj",
    )(x, mod, mod, g.reshape(1, d), w, w_tail, conv_w, conv_b, sc_conv_w_rot)


def _ffn_kernel(*refs, n_parts, tm, tf, row_base, rows_per_group, seq_len, final_norm):
    x_ref, gate1_ref, sh_ref, sc_ref, gate_ref, g_ref, wo_ref = refs[:7]
    a_refs = refs[7:7 + n_parts]
    wg_ref, wu_ref, cw_ref, wd_ref, fg_ref, o_ref, h_ref, act_ref = refs[7 + n_parts:]
    r = _mod_row(pl.program_id(0), tm, row_base, rows_per_group)
    sh = sh_ref[0, pl.ds(r, 1), :]
    sc = sc_ref[0, pl.ds(r, 1), :]
    gate = gate_ref[0, pl.ds(r, 1), :]
    kp = wo_ref.shape[0] // n_parts
    mix = _dot(a_refs[0][...], wo_ref[0:kp, :])
    for k in range(1, n_parts):
        mix = mix + _dot(a_refs[k][...], wo_ref[k * kp:(k + 1) * kp, :])
    x1 = x_ref[...] + gate1_ref[0, pl.ds(r, 1), :] * mix
    o_ref[...] = x1
    h_ref[...] = _norm_mod(x1, g_ref[...], sh, sc).astype(BF16)
    dff = wg_ref.shape[1]
    half = tm // 2
    win = half + SUBLANES
    for hb in range(2):
        w0 = hb * (half - SUBLANES)
        v0 = hb * SUBLANES
        rows = slice(hb * half, (hb + 1) * half)
        hw = h_ref[w0:w0 + win, :]
        hv = h_ref[rows, :]
        for fc in range(dff // tf):
            cols = slice(fc * tf, (fc + 1) * tf)
            gpre = _dwconv3(_dot(hw, wg_ref[:, cols]), cw_ref[:, cols], seq_len, row0=w0)[v0:v0 + half]
            up = _dot(hv, wu_ref[:, cols])
            act_ref[:, cols] = (jax.nn.gelu(gpre, approximate=True) * up).astype(BF16)
        y = o_ref[rows, :] + gate * _dot(act_ref[...], wd_ref[...])
        if final_norm:
            ms = jnp.mean(y * y, axis=-1, keepdims=True)
            y = y * lax.rsqrt(ms + EPS) * fg_ref[...]
        o_ref[rows, :] = y


def _layer_weight(shape):
    return lambda layer: pl.BlockSpec((None,) + shape, lambda i: (layer,) + (0,) * len(shape),
                                      pipeline_mode=pl.Buffered(1))


def _mix_ffn(x, mod, layer, parts, wo, g, wg, wu, cw, wd, final_g, *, row_base, rows_per_group, seq_len, final_norm,
             tm=1024, tf=256):
    m, d = x.shape
    dff = wg.shape[2]
    n_parts = len(parts)
    kern = functools.partial(_ffn_kernel, n_parts=n_parts, tm=tm, tf=tf, row_base=row_base,
                             rows_per_group=rows_per_group, seq_len=seq_len, final_norm=final_norm)
    mod_spec = lambda k: pl.BlockSpec((1, 8, d), lambda i: (layer, 0, k))
    in_specs = [pl.BlockSpec((tm, d), lambda i: (i, 0)), mod_spec(2), mod_spec(3), mod_spec(4), mod_spec(5),
                pl.BlockSpec((1, d), lambda i: (0, 0)), _RESIDENT]
    in_specs += [pl.BlockSpec((tm, a.shape[1]), lambda i: (i, 0)) for a in parts]
    in_specs += [_layer_weight((d, dff))(layer), _layer_weight((d, dff))(layer), _layer_weight((3, dff))(layer),
                 _layer_weight((dff, d))(layer), pl.BlockSpec((1, d), lambda i: (0, 0))]
    return pl.pallas_call(
        kern,
        grid=(m // tm,),
        in_specs=in_specs,
        out_specs=pl.BlockSpec((tm, d), lambda i: (i, 0)),
        out_shape=jax.ShapeDtypeStruct((m, d), F32),
        scratch_shapes=[pltpu.VMEM((tm, d), BF16), pltpu.VMEM((tm // 2, dff), BF16)],
        compiler_params=_cparams(("arbitrary",)),
        name="mix_ffn",
    )(x, mod, mod, mod, mod, g.reshape(1, d), wo, *parts, wg, wu, cw, wd, final_g.reshape(1, d))


def _pair_cols(first_half, arr, h0, h1):
    return jnp.where(first_half, arr[:, h0:h0 + 1], arr[:, h1:h1 + 1])


def _pair_rows(arr_t, h0, h1, width):
    return jnp.concatenate([jnp.broadcast_to(arr_t[h0:h0 + 1, :], (SSD_HEADDIM, width)),
                            jnp.broadcast_to(arr_t[h1:h1 + 1, :], (SSD_HEADDIM, width))], axis=0)


def _pair_scalars(row, h0, h1):
    return jnp.concatenate([jnp.broadcast_to(row[:, h0:h0 + 1], (SSD_HEADDIM, SSD_STATE)),
                            jnp.broadcast_to(row[:, h1:h1 + 1], (SSD_HEADDIM, SSD_STATE))], axis=0)


def _ssd_kernel(*refs, seq_len, has_h0, emit_state):
    zs_ref, x2_ref, bc_ref, dt_ref, prm_ref, ng_ref = refs[:6]
    pos = 6
    if has_h0:
        h0_ref = refs[pos]
        pos += 1
    ya_ref = refs[pos]
    pos += 1
    if emit_state:
        st_ref = refs[pos]
        pos += 1
    yacc_ref = refs[pos]

    lc = SSD_LC
    nchunk = seq_len // lc
    nh = SSD_HEADS
    lane = lax.broadcasted_iota(jnp.int32, (1, LANES), 1)
    first_half = lane < SSD_HEADDIM
    fwd_lane = lane < nh

    bc = bc_ref[0]
    valid = lane < 2 * nh
    dt = jnp.where(valid, jax.nn.softplus(dt_ref[0] + prm_ref[0:1, :]), 0.0)
    a = -jnp.exp(prm_ref[1:2, :])
    dta = jnp.where(valid, dt * a, 0.0)
    dsum = prm_ref[2:3, :] + prm_ref[3:4, :]
    tt = lax.broadcasted_iota(jnp.int32, (lc, lc), 0)
    ss = lax.broadcasted_iota(jnp.int32, (lc, lc), 1)
    tril = jnp.where(ss <= tt, 1.0, 0.0).astype(BF16)
    triu = jnp.where(ss >= tt, 1.0, 0.0).astype(BF16)
    nq = lc // LANES
    td = lax.broadcasted_iota(jnp.int32, (LANES, LANES), 0)
    sd = lax.broadcasted_iota(jnp.int32, (LANES, LANES), 1)
    lower_d = td > sd
    upper_d = td < sd

    acs2_l, rowt_l, wts_l, eacs_l, edec_l, cbm_l, bm_l, cm_l = [], [], [], [], [], [], [], []
    for c in range(nchunk):
        rows = slice(c * lc, (c + 1) * lc)
        acs = jnp.where(fwd_lane, _sel_dot_left(tril, dta[rows]), _sel_dot_left(triu, dta[rows]))
        edge = jnp.where(fwd_lane, acs[lc - 1:lc, :], acs[0:1, :])
        acs2 = acs * LOG2E
        dtt = dt[rows].T
        rowt = acs2.T - jnp.log2(dtt)
        dgt = jnp.log2(dtt[0:nh, :] + dtt[nh:2 * nh, :])
        acs2_l.append(acs2)
        rowt_l.append((rowt, dgt))
        wts_l.append((dt[rows] * jnp.exp(edge - acs)).T)
        eacs_l.append(jnp.exp(acs))
        edec_l.append(jnp.exp(edge))
        cbm_g, bm_g, cm_g = [], [], []
        for g in range(SSD_GROUPS):
            bm = bc[rows, g * SSD_STATE:(g + 1) * SSD_STATE]
            cm = bc[rows, (SSD_GROUPS + g) * SSD_STATE:(SSD_GROUPS + g + 1) * SSD_STATE]
            bm_g.append(bm)
            cm_g.append(cm)
            cbm_g.append(_dot_nt(cm, bm))
        cbm_l.append(cbm_g)
        bm_l.append(bm_g)
        cm_l.append(cm_g)

    for p in range(HEAD_PAIRS):
        cols = slice(p * LANES, (p + 1) * LANES)
        g = p // (HEAD_PAIRS // SSD_GROUPS)
        hd = (2 * p, 2 * p + 1)
        x2b = x2_ref[0, :, cols]
        x2 = x2b.astype(F32)
        dsum2 = _pair_cols(first_half, dsum, hd[0], hd[1])
        if has_h0:
            hf = h0_ref[0, 0, p]
            hb = h0_ref[0, 1, p]
        else:
            hf = jnp.zeros((LANES, SSD_STATE), F32)
            hb = jnp.zeros((LANES, SSD_STATE), F32)
        ys = []
        stbs = []
        for c in range(nchunk):
            rows = slice(c * lc, (c + 1) * lc)
            x2c = x2[rows]
            x2cb = x2b[rows]
            acs2 = acs2_l[c]
            rowt, dgt = rowt_l[c]
            cbm = cbm_l[c][g]
            outs = []
            for h in hd:
                blocks = []
                for bi in range(nq):
                    ri = slice(bi * LANES, (bi + 1) * LANES)
                    cf = acs2[ri, h:h + 1]
                    cbk = acs2[ri, nh + h:nh + h + 1]
                    row_blocks = []
                    for bj in range(nq):
                        cj = slice(bj * LANES, (bj + 1) * LANES)
                        rf = rowt[h:h + 1, cj]
                        rb = rowt[nh + h:nh + h + 1, cj]
                        if bi > bj:
                            arg = cf - rf
                        elif bi < bj:
                            arg = cbk - rb
                        else:
                            arg = jnp.where(lower_d, cf - rf, jnp.where(upper_d, cbk - rb, dgt[h:h + 1, cj]))
                        row_blocks.append((cbm[ri, cj] * jnp.exp2(arg)).astype(BF16))
                    blocks.append(jnp.concatenate(row_blocks, axis=1))
                gm = jnp.concatenate(blocks, axis=0)
                outs.append(_dot(gm, x2cb))
            y = jnp.where(first_half, outs[0], outs[1]) + x2c * dsum2
            xt = x2c.T
            xwf = (xt * _pair_rows(wts_l[c], hd[0], hd[1], lc)).astype(BF16)
            xwb = (xt * _pair_rows(wts_l[c], nh + hd[0], nh + hd[1], lc)).astype(BF16)
            stf = _dot(xwf, bm_l[c][g])
            stbs.append(_dot(xwb, bm_l[c][g]))
            if has_h0 or c > 0:
                y = y + _dot_nt(cm_l[c][g], hf.astype(BF16)) * _pair_cols(first_half, eacs_l[c], hd[0], hd[1])
            hf = hf * _pair_scalars(edec_l[c], hd[0], hd[1]) + stf
            ys.append(y)
        for c in reversed(range(nchunk)):
            if has_h0 or c < nchunk - 1:
                ys[c] = ys[c] + (_dot_nt(cm_l[c][g], hb.astype(BF16))
                                 * _pair_cols(first_half, eacs_l[c], nh + hd[0], nh + hd[1]))
            hb = hb * _pair_scalars(edec_l[c], nh + hd[0], nh + hd[1]) + stbs[c]
        zg = zs_ref[0, :, cols].astype(F32)
        for c in range(nchunk):
            rows = slice(c * lc, (c + 1) * lc)
            yacc_ref[rows, cols] = ys[c] * zg[rows]
        if emit_state:
            st_ref[0, 0, p] = hf
            st_ref[0, 1, p] = hb

    y = yacc_ref[...]
    ms = jnp.mean(y * y, axis=-1, keepdims=True)
    ya_ref[0] = (y * lax.rsqrt(ms + EPS) * ng_ref[...]).astype(BF16)


def _ssd(zs3, x23, bc3, dt3, prm, norm_g, h0, *, emit_state):
    b, s, _ = zs3.shape
    has_h0 = h0 is not None
    in_specs = [
        pl.BlockSpec((1, s, SSD_INNER), lambda i: (i, 0, 0)),
        pl.BlockSpec((1, s, SSD_INNER), lambda i: (i, 0, 0)),
        pl.BlockSpec((1, s, SSD_XBC - SSD_INNER), lambda i: (i, 0, 0)),
        pl.BlockSpec((1, s, LANES), lambda i: (i, 0, 0)),
        pl.BlockSpec((8, LANES), lambda i: (0, 0)),
        pl.BlockSpec((1, SSD_INNER), lambda i: (0, 0)),
    ]
    args = [zs3, x23, bc3, dt3, prm, norm_g]
    state_block = (1, 2, HEAD_PAIRS, LANES, SSD_STATE)
    if has_h0:
        in_specs.append(pl.BlockSpec(state_block, lambda i: (i, 0, 0, 0, 0)))
        args.append(h0)
    out_specs = [pl.BlockSpec((1, s, SSD_INNER), lambda i: (i, 0, 0))]
    out_shape = [jax.ShapeDtypeStruct((b, s, SSD_INNER), BF16)]
    if emit_state:
        out_specs.append(pl.BlockSpec(state_block, lambda i: (i, 0, 0, 0, 0)))
        out_shape.append(jax.ShapeDtypeStruct((b, 2, HEAD_PAIRS, LANES, SSD_STATE), F32))
    kern = functools.partial(_ssd_kernel, seq_len=s, has_h0=has_h0, emit_state=emit_state)
    return pl.pallas_call(
        kern,
        grid=(b,),
        in_specs=in_specs,
        out_specs=out_specs,
        out_shape=out_shape,
        scratch_shapes=[pltpu.VMEM((s, SSD_INNER), F32)],
        compiler_params=_cparams(("arbitrary",)),
        name="ssd",
    )(*args)


def _pair_masks():
    lane = lax.broadcasted_iota(jnp.int32, (1, LANES), 1)
    return lane < NA_HEAD_DIM


def _qkv_ctx_attn_kernel(x_ref, sh_ref, sc_ref, g_ref, w_ref, o_ref, nk_ref, nv_ref, qkv_ref,
                         *, tm, seq_len, row_base, rows_per_group):
    r = _mod_row(pl.program_id(0), tm, row_base, rows_per_group)
    sh = sh_ref[0, pl.ds(r, 1), :]
    sc = sc_ref[0, pl.ds(r, 1), :]
    h = _norm_mod(x_ref[...], g_ref[...], sh, sc).astype(BF16)
    d = x_ref.shape[1]
    tn = 512
    for j in range(3 * d // tn):
        qkv_ref[:, j * tn:(j + 1) * tn] = _dot(h, w_ref[:, j * tn:(j + 1) * tn])
    first_half = _pair_masks()
    scale = NA_HEAD_DIM ** -0.5
    s = seq_len
    for b in range(tm // s):
        rows = slice(b * s, (b + 1) * s)
        nk_ref[b, 0] = qkv_ref[rows, d:2 * d].reshape(s, NA_HEADS, NA_HEAD_DIM)
        nv_ref[b, 0] = qkv_ref[rows, 2 * d:3 * d].reshape(s, NA_HEADS, NA_HEAD_DIM)
        for p in range(HEAD_PAIRS):
            cols = slice(p * LANES, (p + 1) * LANES)
            q2 = qkv_ref[rows, cols] * scale
            kb = qkv_ref[rows, d + p * LANES:d + (p + 1) * LANES].astype(BF16)
            vb = qkv_ref[rows, 2 * d + p * LANES:2 * d + (p + 1) * LANES].astype(BF16)
            qs = jnp.concatenate([jnp.where(first_half, q2, 0.0), jnp.where(first_half, 0.0, q2)], axis=0).astype(BF16)
            sco = _dot_nt(qs, kb)
            e = jnp.exp(sco - jnp.max(sco, axis=-1, keepdims=True))
            pv = _dot(e.astype(BF16), vb) / jnp.sum(e, axis=-1, keepdims=True)
            o_ref[rows, cols] = jnp.where(first_half, pv[:s], pv[s:]).astype(BF16)


def _qkv_ctx_attn(x, mod, layer, g, w, *, seq_len, row_base, rows_per_group, tm=512):
    m, d = x.shape
    nb = tm // seq_len
    cache_block = (nb, 1, seq_len, NA_HEADS, NA_HEAD_DIM)
    cache_shape = jax.ShapeDtypeStruct((m // seq_len, 1, seq_len, NA_HEADS, NA_HEAD_DIM), F32)
    kern = functools.partial(_qkv_ctx_attn_kernel, tm=tm, seq_len=seq_len, row_base=row_base,
                             rows_per_group=rows_per_group)
    return pl.pallas_call(
        kern,
        grid=(m // tm,),
        in_specs=[
            pl.BlockSpec((tm, d), lambda i: (i, 0)),
            pl.BlockSpec((1, 8, d), lambda i: (layer, 0, 0)),
            pl.BlockSpec((1, 8, d), lambda i: (layer, 0, 1)),
            pl.BlockSpec((1, d), lambda i: (0, 0)),
            _RESIDENT,
        ],
        out_specs=[
            pl.BlockSpec((tm, d), lambda i: (i, 0)),
            pl.BlockSpec(cache_block, lambda i: (i, 0, 0, 0, 0)),
            pl.BlockSpec(cache_block, lambda i: (i, 0, 0, 0, 0)),
        ],
        out_shape=[jax.ShapeDtypeStruct((m, d), BF16), cache_shape, cache_shape],
        scratch_shapes=[pltpu.VMEM((tm, 3 * d), F32)],
        compiler_params=_cparams(("arbitrary",)),
        name="qkv_ctx_attention",
    )(x, mod, mod, g.reshape(1, d), w)


def _lat_attn_kernel(q_ref, k_ref, v_ref, kc_ref, vc_ref, rpb_ref, o_ref, bias_ref, *, rows):
    b = pl.program_id(1)
    first_half = _pair_masks()
    kh = min(NA_KH, rows)
    win = kh * GRID_W
    neg_inf = -jnp.inf

    def win_start(r):
        return min(max(r - kh // 2, 0), rows - kh)

    @pl.when(b == 0)
    def _():
        qc = lax.broadcasted_iota(jnp.int32, (GRID_W, LANES), 0)
        kc = lax.broadcasted_iota(jnp.int32, (GRID_W, LANES), 1)
        cs = jnp.clip(qc - NA_KW // 2, 0, GRID_W - NA_KW)
        col_ok = (kc >= cs) & (kc < cs + NA_KW)
        for hh in range(2):
            for dr in range(2 * NA_KH - 1):
                v = jnp.broadcast_to(rpb_ref[hh, dr:dr + 1, :], (GRID_W, LANES))
                t = pltpu.roll(v, LANES - (NA_KW - 1), 1, stride=1, stride_axis=0)
                tile = jnp.where(col_ok, t, neg_inf)[:, :GRID_W]
                for r in range(rows):
                    i = dr - (NA_KH - 1) + r - win_start(r)
                    if 0 <= i < kh:
                        bias_ref[r, hh * GRID_W:(hh + 1) * GRID_W, i * GRID_W:(i + 1) * GRID_W] = tile

    scale = NA_HEAD_DIM ** -0.5
    kb = k_ref[...].astype(BF16)
    vb = v_ref[...].astype(BF16)
    kcb = kc_ref[...].astype(BF16)
    vcb = vc_ref[...].astype(BF16)
    group = 4
    for r0 in range(0, rows, group):
        rr = range(r0, r0 + group)
        qrows = [slice(r * GRID_W, (r + 1) * GRID_W) for r in rr]
        krows = [slice(win_start(r) * GRID_W, win_start(r) * GRID_W + win) for r in rr]
        qs = []
        for qr in qrows:
            q2 = q_ref[qr, :] * scale
            qs.append(jnp.concatenate([jnp.where(first_half, q2, 0.0), jnp.where(first_half, 0.0, q2)],
                                      axis=0).astype(BF16))
        s_loc = [_dot_nt(qi, kb[kr]) + bias_ref[r] for qi, kr, r in zip(qs, krows, rr)]
        s_ctx = [_dot_nt(qi, kcb) for qi in qs]
        mx = [jnp.maximum(jnp.max(sl, axis=-1, keepdims=True), jnp.max(sc, axis=-1, keepdims=True))
              for sl, sc in zip(s_loc, s_ctx)]
        e_loc = [jnp.exp(sl - m) for sl, m in zip(s_loc, mx)]
        e_ctx = [jnp.exp(sc - m) for sc, m in zip(s_ctx, mx)]
        den = [jnp.sum(el, axis=-1, keepdims=True) + jnp.sum(ec, axis=-1, keepdims=True) for el, ec in zip(e_loc, e_ctx)]
        pv = [(_dot(el.astype(BF16), vb[kr]) + _dot(ec.astype(BF16), vcb)) / dn
              for el, ec, kr, dn in zip(e_loc, e_ctx, krows, den)]
        for qr, o in zip(qrows, pv):
            o_ref[qr, :] = jnp.where(first_half, o[:GRID_W], o[GRID_W:]).astype(BF16)


def _lat_attn(q, k, v, kc, vc, rpb_pad, *, n_batch, n_tok, n_ctx):
    d = q.shape[1]
    rows = n_tok // GRID_W
    kh = min(NA_KH, rows)
    tok_spec = pl.BlockSpec((n_tok, LANES), lambda p, b: (b, p))
    ctx_spec = pl.BlockSpec((n_ctx, LANES), lambda p, b: (b, p))
    kern = functools.partial(_lat_attn_kernel, rows=rows)
    return pl.pallas_call(
        kern,
        grid=(HEAD_PAIRS, n_batch),
        in_specs=[tok_spec, tok_spec, tok_spec, ctx_spec, ctx_spec,
                  pl.BlockSpec((2, 2 * NA_KH, LANES), lambda p, b: (p, 0, 0))],
        out_specs=tok_spec,
        out_shape=jax.ShapeDtypeStruct((n_batch * n_tok, d), BF16),
        scratch_shapes=[pltpu.VMEM((rows, 2 * GRID_W, kh * GRID_W), F32)],
        compiler_params=_cparams(("arbitrary", "arbitrary")),
        name="latent_attention",
    )(q, k, v, kc, vc, rpb_pad)


def _pad_lanes(v):
    return jnp.pad(v, (0, LANES - v.shape[0]))


def _run_stream(x3, mod, row_base, W, *, is_ctx, state_ssd=None, cache_k=None, cache_v=None):
    b, s, d = x3.shape
    m = b * s
    x = x3.reshape(m, d)
    rpg = m if is_ctx else s
    kw = dict(row_base=row_base, rows_per_group=rpg)

    zs, x2, bca, yb, dtr = _in_proj(x, mod, W['norm_mix_g'][0], W['w_in'], W['w_in_tail'], W['ssd_conv_w'], W['ssd_conv_b'],
                                    W['sc_conv_w_rot'], seq_len=s, tm=max(s, 512), **kw)
    h0 = None
    if not is_ctx:
        h0 = state_ssd[:, 0].reshape(b, 2, HEAD_PAIRS, LANES, SSD_STATE)
    res = _ssd(zs.reshape(b, s, -1), x2.reshape(b, s, -1), bca.reshape(b, s, -1), dtr.reshape(b, s, LANES),
               W['ssd_prm'], W['ssd_norm_g'], h0, emit_state=is_ctx)
    ya = res[0].reshape(m, SSD_INNER)
    x = _mix_ffn(x, mod, 0, [ya, yb], W['mix0_w_out'], W['norm_ffn_g'][0], *W['ffn'], W['final_norm_g'], seq_len=s,
                 final_norm=False, **kw)

    if is_ctx:
        o, new_k, new_v = _qkv_ctx_attn(x, mod, 1, W['norm_mix_g'][1], W['na_w_qkv'], seq_len=s, **kw)
    else:
        q, k, v = _nm_matmul(x, mod, 1, 0, 1, W['norm_mix_g'][1], W['na_w_qkv'], n_groups=3, group_width=D_MODEL, **kw)
        n_ctx = cache_k.shape[2]
        kc = cache_k[:, 0].reshape(b * n_ctx, d)
        vc = cache_v[:, 0].reshape(b * n_ctx, d)
        o = _lat_attn(q, k, v, kc, vc, W['rpb_pad'], n_batch=b, n_tok=s, n_ctx=n_ctx)
    x = _mix_ffn(x, mod, 1, [o], W['na_w_out'], W['norm_ffn_g'][1], *W['ffn'], W['final_norm_g'], seq_len=s,
                 final_norm=True, **kw)
    y = x.reshape(b, s, d)
    if is_ctx:
        new_state = res[1].reshape(b, 1, 2, SSD_HEADS, SSD_HEADDIM, SSD_STATE)
        return y, new_state, new_k, new_v
    return y


def kernel(x_prompt, x_sample, state_ssd, cache_k, cache_v, c, c_ctx, ada_w, ada_b, norm_mix_g, norm_ffn_g, ssd_w_in,
           ssd_conv_w, ssd_conv_b, ssd_dt_bias, ssd_a_log, ssd_d, ssd_norm_g, sc_conv_w, mix0_w_out, na_w_qkv, na_rpb,
           na_w_out, ffn_w_gate, ffn_w_up, ffn_conv_w, ffn_w_down, final_norm_g):
    n_lat = x_sample.shape[0]
    cvec = jnp.concatenate([c_ctx[None, :], c, jnp.zeros((8 - 1 - n_lat, D_MODEL), F32)], axis=0)
    mod = _ada(cvec, ada_w, ada_b)

    w_in = ssd_w_in[0].astype(BF16)
    w_in_tail = jnp.pad(ssd_w_in[0][:, TAIL_COL0:], ((0, 0), (0, LANES - MIX_SHIFT))).astype(BF16)
    w_out0 = mix0_w_out[0]
    w_out0 = jnp.concatenate([w_out0[:SSD_INNER], jnp.roll(w_out0[SSD_INNER:], MIX_SHIFT, axis=0)], axis=0).astype(BF16)
    prm =jnp.stack([_pad_lanes(ssd_dt_bias[0].reshape(-1)), _pad_lanes(ssd_a_log[0].reshape(-1)),
                     _pad_lanes(ssd_d[0, 0]), _pad_lanes(ssd_d[0, 1])] + [jnp.zeros((LANES,), F32)] * 4, axis=0)
    rpb = na_rpb[0]
    rpb_pad = jnp.pad(rpb, ((0, 0), (0, 1), (0, LANES - rpb.shape[2])))
    W = {
        'norm_mix_g': norm_mix_g, 'norm_ffn_g': norm_ffn_g, 'final_norm_g': final_norm_g,
        'w_in': w_in, 'w_in_tail': w_in_tail,
        'ssd_conv_w': ssd_conv_w[0], 'ssd_conv_b': ssd_conv_b[0].reshape(1, SSD_XBC), 'ssd_prm': prm,
        'ssd_norm_g': ssd_norm_g[0].reshape(1, SSD_INNER), 'sc_conv_w_rot': jnp.roll(sc_conv_w[0], MIX_SHIFT, axis=1),
        'mix0_w_out': w_out0,
        'na_w_qkv': na_w_qkv[0].astype(BF16), 'rpb_pad': rpb_pad, 'na_w_out': na_w_out[0].astype(BF16),
        'ffn': (ffn_w_gate.astype(BF16), ffn_w_up.astype(BF16), ffn_conv_w, ffn_w_down.astype(BF16)),
    }
    y_prompt, new_state, new_k, new_v = _run_stream(x_prompt, mod, 0, W, is_ctx=True)
    y_sample = _run_stream(x_sample, mod, 1, W, is_ctx=False, state_ssd=state_ssd, cache_k=cache_k, cache_v=cache_v)
    return (y_prompt, y_sample, new_state, new_k, new_v)
```

```python
import functools

import jax
import jax.numpy as jnp
from jax import lax
from jax.experimental import pallas as pl
from jax.experimental.pallas import tpu as pltpu

F32 = jnp.float32
BF16 = jnp.bfloat16

D_MODEL = 1024
EPS = 1e-6
GRID_W = 64
SSD_HEADDIM = 64
SSD_HEADS = 16
SSD_STATE = 128
SSD_GROUPS = 2
SSD_INNER = 1024
SSD_XBC = 1536
SSD_LC = 256
SC_WIDTH = 1024
NA_HEADS = 16
NA_HEAD_DIM = 64
NA_KH = 8
NA_KW = 16

LOG2E = 1.4426950408889634
LANES = 128
SUBLANES = 8
HEAD_PAIRS = 8
U_COLS = 5 * 1024 + 512

VMEM_LIMIT = 56 * 1024 * 1024

_RESIDENT = pl.BlockSpec(memory_space=pltpu.VMEM)


def _cparams(sem):
    return pltpu.CompilerParams(dimension_semantics=sem, vmem_limit_bytes=VMEM_LIMIT)


def _silu(x):
    return x * jax.nn.sigmoid(x)


def _dot(a, b):
    return jnp.dot(a, b, preferred_element_type=F32)


def _dot_nt(a, b):
    return lax.dot_general(a, b, (((1,), (1,)), ((), ())), preferred_element_type=F32)


def _split3(v):
    hi = v.astype(BF16)
    r1 = v - hi.astype(F32)
    mid = r1.astype(BF16)
    lo = (r1 - mid.astype(F32)).astype(BF16)
    return hi, mid, lo


def _sel_dot_left(sel, v):
    hi, mid, lo = _split3(v)
    return _dot(sel, hi) + _dot(sel, mid) + _dot(sel, lo)


def _norm_mod(x, g, shift, scale):
    ms = jnp.mean(x * x, axis=-1, keepdims=True)
    y = x * lax.rsqrt(ms + EPS) * g
    return y * (1.0 + scale) + shift


def _mod_row(i, tm, row_base, rows_per_group):
    return row_base + (i * tm) // rows_per_group


def _ada_kernel(c_ref, w_ref, b_ref, o_ref):
    s = _silu(c_ref[...]).astype(BF16)
    o_ref[0] = _dot(s, w_ref[0].astype(BF16)) + b_ref[0]


def _ada(cvec8, ada_w, ada_b):
    depth, d, n = ada_w.shape
    tn = 1536
    return pl.pallas_call(
        _ada_kernel,
        grid=(depth, n // tn),
        in_specs=[
            pl.BlockSpec((8, d), lambda l, j: (0, 0)),
            pl.BlockSpec((1, d, tn), lambda l, j: (l, 0, j)),
            pl.BlockSpec((1, 1, tn), lambda l, j: (l, 0, j)),
        ],
        out_specs=pl.BlockSpec((1, 8, tn), lambda l, j: (l, 0, j)),
        out_shape=jax.ShapeDtypeStruct((depth, 8, n), F32),
        compiler_params=_cparams(("arbitrary", "arbitrary")),
        name="ada_mod",
    )(cvec8, ada_w, ada_b.reshape(depth, 1, n))


def _nm_matmul_kernel(*refs, n_groups, group_width, has_extra, tm, tn, row_base, rows_per_group):
    x_ref, sh_ref, sc_ref, g_ref, w_ref = refs[:5]
    pos = 5
    if has_extra:
        we_ref = refs[pos]
        pos += 1
    o_refs = refs[pos:pos + n_groups]
    pos += n_groups
    if has_extra:
        oe_ref = refs[pos]
    r = _mod_row(pl.program_id(0), tm, row_base, rows_per_group)
    sh = sh_ref[0, pl.ds(r, 1), :]
    sc = sc_ref[0, pl.ds(r, 1), :]
    h = _norm_mod(x_ref[...], g_ref[...], sh, sc).astype(BF16)
    if has_extra:
        oe_ref[...] = _dot(h, we_ref[...])
    for gi, o_ref in enumerate(o_refs):
        for j in range(group_width // tn):
            res = _dot(h, w_ref[:, gi * group_width + j * tn:gi * group_width + (j + 1) * tn])
            o_ref[:, j * tn:(j + 1) * tn] = res.astype(o_ref.dtype)


def _nm_matmul(x, mod, layer, shift_idx, scale_idx, g, w, *, n_groups, group_width, row_base, rows_per_group,
               w_extra=None, out_dtype=F32, tm=512, tn=512):
    m, d = x.shape
    has_extra = w_extra is not None
    in_specs = [
        pl.BlockSpec((tm, d), lambda i: (i, 0)),
        pl.BlockSpec((1, 8, d), lambda i: (layer, 0, shift_idx)),
        pl.BlockSpec((1, 8, d), lambda i: (layer, 0, scale_idx)),
        pl.BlockSpec((1, d), lambda i: (0, 0)),
        _RESIDENT,
    ]
    args = [x, mod, mod, g.reshape(1, d), w]
    out_specs = [pl.BlockSpec((tm, group_width), lambda i: (i, 0)) for _ in range(n_groups)]
    out_shape = [jax.ShapeDtypeStruct((m, group_width), out_dtype) for _ in range(n_groups)]
    if has_extra:
        ne = w_extra.shape[1]
        in_specs.append(_RESIDENT)
        args.append(w_extra)
        out_specs.append(pl.BlockSpec((tm, ne), lambda i: (i, 0)))
        out_shape.append(jax.ShapeDtypeStruct((m, ne), F32))
    kern = functools.partial(_nm_matmul_kernel, n_groups=n_groups, group_width=group_width, has_extra=has_extra, tm=tm,
                             tn=tn, row_base=row_base, rows_per_group=rows_per_group)
    return pl.pallas_call(
        kern,
        grid=(m // tm,),
        in_specs=in_specs,
        out_specs=out_specs,
        out_shape=out_shape,
        compiler_params=_cparams(("arbitrary",)),
        name="norm_mod_matmul",
    )(*args)


def _dwconv3(v, w, seq_len, row0=0):
    rows = v.shape[0]
    pos = (lax.broadcasted_iota(jnp.int32, v.shape, 0) + row0) % seq_len
    prev = jnp.where(pos == 0, 0.0, pltpu.roll(v, 1, 0))
    nxt = jnp.where(pos == seq_len - 1, 0.0, pltpu.roll(v, rows - 1, 0))
    return prev * w[0:1, :] + v * w[1:2, :] + nxt * w[2:3, :]


MIX_SHIFT = 2 * SSD_HEADS
DT_COL0 = SSD_INNER + SSD_XBC
TAIL_COL0 = DT_COL0 + 3 * SC_WIDTH


def _in_proj_kernel(x_ref, sh_ref, sc_ref, g_ref, w_ref, wt_ref, cw_ref, cb_ref, scw_ref, zs_ref, x2_ref, bc_ref, yb_ref,
                    dt_ref,
                    *, tm, row_base, rows_per_group, seq_len):
    r = _mod_row(pl.program_id(0), tm, row_base, rows_per_group)
    sh = sh_ref[0, pl.ds(r, 1), :]
    sc = sc_ref[0, pl.ds(r, 1), :]
    h = _norm_mod(x_ref[...], g_ref[...], sh, sc).astype(BF16)
    tn = 256

    def proj(c0, width=tn):
        return _dot(h, w_ref[:, c0:c0 + width])

    for j in range(SSD_INNER // tn):
        cols = slice(j * tn, (j + 1) * tn)
        zs_ref[:, cols] = _silu(proj(j * tn)).astype(BF16)
        xc = _dwconv3(proj(SSD_INNER + j * tn), cw_ref[:, cols], seq_len) + cb_ref[:, cols]
        x2_ref[:, cols] = _silu(xc).astype(BF16)
    for j in range((SSD_XBC - SSD_INNER) // tn):
        cols = slice(SSD_INNER + j * tn, SSD_INNER + (j + 1) * tn)
        bcc = _dwconv3(proj(SSD_INNER + cols.start), cw_ref[:, cols], seq_len) + cb_ref[:, cols]
        bc_ref[:, j * tn:(j + 1) * tn] = _silu(bcc).astype(BF16)
    dt_ref[...] = proj(DT_COL0, LANES)
    q3 = _dot(h, wt_ref[...])
    tail = lax.broadcasted_iota(jnp.int32, (1, LANES), 1) < MIX_SHIFT
    for j in range(SC_WIDTH // tn):
        c0 = DT_COL0 + j * tn
        cols = slice(j * tn, (j + 1) * tn)
        p0, p1, p2 = proj(c0), proj(c0 + SC_WIDTH), proj(c0 + 2 * SC_WIDTH)
        y = p0 * _dwconv3(p1 * p2, scw_ref[:, cols], seq_len)
        if j == 0:
            yt = p1[:, :LANES] * _dwconv3(p2[:, :LANES] * q3, scw_ref[:, :LANES], seq_len)
            y = jnp.concatenate([jnp.where(tail, yt, y[:, :LANES]), y[:, LANES:]], axis=1)
        yb_ref[:, cols] = y.astype(BF16)


def _in_proj(x, mod, g, w, w_tail, conv_w, conv_b, sc_conv_w_rot, *, seq_len, row_base, rows_per_group, tm):
    m, d = x.shape
    kern = functools.partial(_in_proj_kernel, tm=tm, row_base=row_base, rows_per_group=rows_per_group, seq_len=seq_len)
    widths = (SSD_INNER, SSD_INNER, SSD_XBC - SSD_INNER, SC_WIDTH, LANES)
    dtypes = (BF16, BF16, BF16, BF16, F32)
    return pl.pallas_call(
        kern,
        grid=(m // tm,),
        in_specs=[
            pl.BlockSpec((tm, d), lambda i: (i, 0)),
            pl.BlockSpec((1, 8, d), lambda i: (0, 0, 0)),
            pl.BlockSpec((1, 8, d), lambda i: (0, 0, 1)),
            pl.BlockSpec((1, d), lambda i: (0, 0)),
            _RESIDENT,
            _RESIDENT,
            pl.BlockSpec((3, SSD_XBC), lambda i: (0, 0)),
            pl.BlockSpec((1, SSD_XBC), lambda i: (0, 0)),
            pl.BlockSpec((3, SC_WIDTH), lambda i: (0, 0)),
        ],
        out_specs=[pl.BlockSpec((tm, wd), lambda i: (i, 0)) for wd in widths],
        out_shape=[jax.ShapeDtypeStruct((m, wd), dt) for wd, dt in zip(widths, dtypes)],
        compiler_params=_cparams(("arbitrary",)),
        name="in_proj",
    )(x, mod, mod, g.reshape(1, d), w, w_tail, conv_w, conv_b, sc_conv_w_rot)


def _ffn_kernel(*refs, n_parts, tm, tf, row_base, rows_per_group, seq_len, final_norm):
    x_ref, gate1_ref, sh_ref, sc_ref, gate_ref, g_ref, wo_ref = refs[:7]
    a_refs = refs[7:7 + n_parts]
    wg_ref, wu_ref, cw_ref, wd_ref, fg_ref, o_ref, h_ref, act_ref = refs[7 + n_parts:]
    r = _mod_row(pl.program_id(0), tm, row_base, rows_per_group)
    sh = sh_ref[0, pl.ds(r, 1), :]
    sc = sc_ref[0, pl.ds(r, 1), :]
    gate = gate_ref[0, pl.ds(r, 1), :]
    kp = wo_ref.shape[0] // n_parts
    sub = tm // 4
    for q in range(tm // sub):
        rq = slice(q * sub, (q + 1) * sub)
        mix = _dot(a_refs[0][rq, :], wo_ref[0:kp, :])
        for k in range(1, n_parts):
            mix = mix + _dot(a_refs[k][rq, :], wo_ref[k * kp:(k + 1) * kp, :])
        x1 = x_ref[rq, :] + gate1_ref[0, pl.ds(r, 1), :] * mix
        o_ref[rq, :] = x1
        h_ref[rq, :] = _norm_mod(x1, g_ref[...], sh, sc).astype(BF16)
    dff = wg_ref.shape[1]
    half = tm // 2
    win = half + SUBLANES
    for hb in range(2):
        w0 = hb * (half - SUBLANES)
        v0 = hb * SUBLANES
        rows = slice(hb * half, (hb + 1) * half)
        hw = h_ref[w0:w0 + win, :]
        hv = h_ref[rows, :]
        for fc in range(dff // tf):
            cols = slice(fc * tf, (fc + 1) * tf)
            gpre = _dwconv3(_dot(hw, wg_ref[:, cols]), cw_ref[:, cols], seq_len, row0=w0)[v0:v0 + half]
            up = _dot(hv, wu_ref[:, cols])
            act_ref[:, cols] = (jax.nn.gelu(gpre, approximate=True) * up).astype(BF16)
        y = o_ref[rows, :] + gate * _dot(act_ref[...], wd_ref[...])
        if final_norm:
            ms = jnp.mean(y * y, axis=-1, keepdims=True)
            y = y * lax.rsqrt(ms + EPS) * fg_ref[...]
        o_ref[rows, :] = y


def _layer_weight(shape):
    return lambda layer: pl.BlockSpec((None,) + shape, lambda i: (layer,) + (0,) * len(shape),
                                      pipeline_mode=pl.Buffered(1))


def _mix_ffn(x, mod, layer, parts, wo, g, wg, wu, cw, wd, final_g, *, row_base, rows_per_group, seq_len, final_norm,
             tm=1024, tf=256):
    m, d = x.shape
    dff = wg.shape[2]
    n_parts = len(parts)
    kern = functools.partial(_ffn_kernel, n_parts=n_parts, tm=tm, tf=tf, row_base=row_base,
                             rows_per_group=rows_per_group, seq_len=seq_len, final_norm=final_norm)
    mod_spec = lambda k: pl.BlockSpec((1, 8, d), lambda i: (layer, 0, k))
    in_specs = [pl.BlockSpec((tm, d), lambda i: (i, 0)), mod_spec(2), mod_spec(3), mod_spec(4), mod_spec(5),
                pl.BlockSpec((1, d), lambda i: (0, 0)), _RESIDENT]
    in_specs += [pl.BlockSpec((tm, a.shape[1]), lambda i: (i, 0)) for a in parts]
    in_specs += [_layer_weight((d, dff))(layer), _layer_weight((d, dff))(layer), _layer_weight((3, dff))(layer),
                 _layer_weight((dff, d))(layer), pl.BlockSpec((1, d), lambda i: (0, 0))]
    return pl.pallas_call(
        kern,
        grid=(m // tm,),
        in_specs=in_specs,
        out_specs=pl.BlockSpec((tm, d), lambda i: (i, 0)),
        out_shape=jax.ShapeDtypeStruct((m, d), F32),
        scratch_shapes=[pltpu.VMEM((tm, d), BF16), pltpu.VMEM((tm // 2, dff), BF16)],
        compiler_params=_cparams(("arbitrary",)),
        name="mix_ffn",
    )(x, mod, mod, mod, mod, g.reshape(1, d), wo, *parts, wg, wu, cw, wd, final_g.reshape(1, d))


def _pair_cols(first_half, arr, h0, h1):
    return jnp.where(first_half, arr[:, h0:h0 + 1], arr[:, h1:h1 + 1])


def _pair_rows(arr_t, h0, h1, width):
    return jnp.concatenate([jnp.broadcast_to(arr_t[h0:h0 + 1, :], (SSD_HEADDIM, width)),
                            jnp.broadcast_to(arr_t[h1:h1 + 1, :], (SSD_HEADDIM, width))], axis=0)


def _pair_scalars(row, h0, h1):
    return jnp.concatenate([jnp.broadcast_to(row[:, h0:h0 + 1], (SSD_HEADDIM, SSD_STATE)),
                            jnp.broadcast_to(row[:, h1:h1 + 1], (SSD_HEADDIM, SSD_STATE))], axis=0)


def _ssd_kernel(*refs, n_elems, **kw):
    for e in range(n_elems):
        _ssd_element(e, *refs, **kw)


def _ssd_element(e, *refs, seq_len, has_h0, emit_state):
    zs_ref, x2_ref, bc_ref, dt_ref, prm_ref, ng_ref = refs[:6]
    pos = 6
    if has_h0:
        h0_ref = refs[pos]
        pos += 1
    ya_ref = refs[pos]
    pos += 1
    if emit_state:
        st_ref = refs[pos]
        pos += 1
    yacc_ref = refs[pos]

    lc = SSD_LC
    nchunk = seq_len // lc
    nh = SSD_HEADS
    lane = lax.broadcasted_iota(jnp.int32, (1, LANES), 1)
    first_half = lane < SSD_HEADDIM
    fwd_lane = lane < nh

    bc = bc_ref[e]
    valid = lane < 2 * nh
    dt = jnp.where(valid, jax.nn.softplus(dt_ref[e] + prm_ref[0:1, :]), 0.0)
    a = -jnp.exp(prm_ref[1:2, :])
    dta = jnp.where(valid, dt * a, 0.0)
    dsum = prm_ref[2:3, :] + prm_ref[3:4, :]
    tt = lax.broadcasted_iota(jnp.int32, (lc, lc), 0)
    ss = lax.broadcasted_iota(jnp.int32, (lc, lc), 1)
    tril = jnp.where(ss <= tt, 1.0, 0.0).astype(BF16)
    triu = jnp.where(ss >= tt, 1.0, 0.0).astype(BF16)
    nq = lc // LANES
    td = lax.broadcasted_iota(jnp.int32, (LANES, LANES), 0)
    sd = lax.broadcasted_iota(jnp.int32, (LANES, LANES), 1)
    lower_d = td > sd
    upper_d = td < sd

    acs2_l, rowt_l, wts_l, eacs_l, edec_l, cbm_l, bm_l, cm_l = [], [], [], [], [], [], [], []
    for c in range(nchunk):
        rows = slice(c * lc, (c + 1) * lc)
        acs = jnp.where(fwd_lane, _sel_dot_left(tril, dta[rows]), _sel_dot_left(triu, dta[rows]))
        edge = jnp.where(fwd_lane, acs[lc - 1:lc, :], acs[0:1, :])
        acs2 = acs * LOG2E
        dtt = dt[rows].T
        rowt = acs2.T - jnp.log2(dtt)
        dgt = jnp.log2(dtt[0:nh, :] + dtt[nh:2 * nh, :])
        acs2_l.append(acs2)
        rowt_l.append((rowt, dgt))
        wts_l.append((dt[rows] * jnp.exp(edge - acs)).T)
        eacs_l.append(jnp.exp(acs))
        edec_l.append(jnp.exp(edge))
        cbm_g, bm_g, cm_g = [], [], []
        for g in range(SSD_GROUPS):
            bm = bc[rows, g * SSD_STATE:(g + 1) * SSD_STATE]
            cm = bc[rows, (SSD_GROUPS + g) * SSD_STATE:(SSD_GROUPS + g + 1) * SSD_STATE]
            bm_g.append(bm)
            cm_g.append(cm)
            cbm_g.append(_dot_nt(cm, bm))
        cbm_l.append(cbm_g)
        bm_l.append(bm_g)
        cm_l.append(cm_g)

    for p in range(HEAD_PAIRS):
        cols = slice(p * LANES, (p + 1) * LANES)
        g = p // (HEAD_PAIRS // SSD_GROUPS)
        hd = (2 * p, 2 * p + 1)
        x2b = x2_ref[e, :, cols]
        x2 = x2b.astype(F32)
        dsum2 = _pair_cols(first_half, dsum, hd[0], hd[1])
        if has_h0:
            hf = h0_ref[e, 0, p]
            hb = h0_ref[e, 1, p]
        else:
            hf = jnp.zeros((LANES, SSD_STATE), F32)
            hb = jnp.zeros((LANES, SSD_STATE), F32)
        ys = []
        stbs = []
        for c in range(nchunk):
            rows = slice(c * lc, (c + 1) * lc)
            x2c = x2[rows]
            x2cb = x2b[rows]
            acs2 = acs2_l[c]
            rowt, dgt = rowt_l[c]
            cbm = cbm_l[c][g]
            outs = []
            for h in hd:
                blocks = []
                for bi in range(nq):
                    ri = slice(bi * LANES, (bi + 1) * LANES)
                    cf = acs2[ri, h:h + 1]
                    cbk = acs2[ri, nh + h:nh + h + 1]
                    row_blocks = []
                    for bj in range(nq):
                        cj = slice(bj * LANES, (bj + 1) * LANES)
                        rf = rowt[h:h + 1, cj]
                        rb = rowt[nh + h:nh + h + 1, cj]
                        if bi > bj:
                            arg = cf - rf
                        elif bi < bj:
                            arg = cbk - rb
                        else:
                            arg = jnp.where(lower_d, cf - rf, jnp.where(upper_d, cbk - rb, dgt[h:h + 1, cj]))
                        row_blocks.append((cbm[ri, cj] * jnp.exp2(arg)).astype(BF16))
                    blocks.append(jnp.concatenate(row_blocks, axis=1))
                gm = jnp.concatenate(blocks, axis=0)
                outs.append(_dot(gm, x2cb))
            y = jnp.where(first_half, outs[0], outs[1]) + x2c * dsum2
            xt = x2c.T
            xwf = (xt * _pair_rows(wts_l[c], hd[0], hd[1], lc)).astype(BF16)
            xwb = (xt * _pair_rows(wts_l[c], nh + hd[0], nh + hd[1], lc)).astype(BF16)
            stf = _dot(xwf, bm_l[c][g])
            stbs.append(_dot(xwb, bm_l[c][g]))
            if has_h0 or c > 0:
                y = y + _dot_nt(cm_l[c][g], hf.astype(BF16)) * _pair_cols(first_half, eacs_l[c], hd[0], hd[1])
            hf = hf * _pair_scalars(edec_l[c], hd[0], hd[1]) + stf
            ys.append(y)
        for c in reversed(range(nchunk)):
            if has_h0 or c < nchunk - 1:
                ys[c] = ys[c] + (_dot_nt(cm_l[c][g], hb.astype(BF16))
                                 * _pair_cols(first_half, eacs_l[c], nh + hd[0], nh + hd[1]))
            hb = hb * _pair_scalars(edec_l[c], nh + hd[0], nh + hd[1]) + stbs[c]
        zg = zs_ref[e, :, cols].astype(F32)
        for c in range(nchunk):
            rows = slice(c * lc, (c + 1) * lc)
            yacc_ref[rows, cols] = ys[c] * zg[rows]
        if emit_state:
            st_ref[e, 0, p] = hf
            st_ref[e, 1, p] = hb

    y = yacc_ref[...]
    ms = jnp.mean(y * y, axis=-1, keepdims=True)
    ya_ref[e] = (y * lax.rsqrt(ms + EPS) * ng_ref[...]).astype(BF16)


def _ssd(zs3, x23, bc3, dt3, prm, norm_g, h0, *, emit_state, n_elems):
    b, s, _ = zs3.shape
    has_h0 = h0 is not None
    ne = n_elems
    in_specs = [
        pl.BlockSpec((ne, s, SSD_INNER), lambda i: (i, 0, 0)),
        pl.BlockSpec((ne, s, SSD_INNER), lambda i: (i, 0, 0)),
        pl.BlockSpec((ne, s, SSD_XBC - SSD_INNER), lambda i: (i, 0, 0)),
        pl.BlockSpec((ne, s, LANES), lambda i: (i, 0, 0)),
        pl.BlockSpec((8, LANES), lambda i: (0, 0)),
        pl.BlockSpec((1, SSD_INNER), lambda i: (0, 0)),
    ]
    args = [zs3, x23, bc3, dt3, prm, norm_g]
    state_block = (ne, 2, HEAD_PAIRS, LANES, SSD_STATE)
    if has_h0:
        in_specs.append(pl.BlockSpec(state_block, lambda i: (i, 0, 0, 0, 0)))
        args.append(h0)
    out_specs = [pl.BlockSpec((ne, s, SSD_INNER), lambda i: (i, 0, 0))]
    out_shape = [jax.ShapeDtypeStruct((b, s, SSD_INNER), BF16)]
    if emit_state:
        out_specs.append(pl.BlockSpec(state_block, lambda i: (i, 0, 0, 0, 0)))
        out_shape.append(jax.ShapeDtypeStruct((b, 2, HEAD_PAIRS, LANES, SSD_STATE), F32))
    kern = functools.partial(_ssd_kernel, n_elems=ne, seq_len=s, has_h0=has_h0, emit_state=emit_state)
    return pl.pallas_call(
        kern,
        grid=(b // ne,),
        in_specs=in_specs,
        out_specs=out_specs,
        out_shape=out_shape,
        scratch_shapes=[pltpu.VMEM((s, SSD_INNER), F32)],
        compiler_params=_cparams(("arbitrary",)),
        name="ssd",
    )(*args)


def _pair_masks():
    lane = lax.broadcasted_iota(jnp.int32, (1, LANES), 1)
    return lane < NA_HEAD_DIM


def _qkv_ctx_attn_kernel(x_ref, sh_ref, sc_ref, g_ref, w_ref, o_ref, nk_ref, nv_ref, qkv_ref,
                         *, tm, seq_len, row_base, rows_per_group):
    r = _mod_row(pl.program_id(0), tm, row_base, rows_per_group)
    sh = sh_ref[0, pl.ds(r, 1), :]
    sc = sc_ref[0, pl.ds(r, 1), :]
    h = _norm_mod(x_ref[...], g_ref[...], sh, sc).astype(BF16)
    d = x_ref.shape[1]
    tn = 512
    s = seq_len
    for part, out_ref in ((1, nk_ref), (2, nv_ref)):
        for j in range(d // tn):
            c0 = part * d + j * tn
            qkv_ref[:, c0:c0 + tn] = _dot(h, w_ref[:, c0:c0 + tn])
        for b in range(tm // s):
            rows = slice(b * s, (b + 1) * s)
            out_ref[b, 0] = qkv_ref[rows, part * d:(part + 1) * d].reshape(s, NA_HEADS, NA_HEAD_DIM)
    for j in range(d // tn):
        qkv_ref[:, j * tn:(j + 1) * tn] = _dot(h, w_ref[:, j * tn:(j + 1) * tn])
    first_half = _pair_masks()
    scale = NA_HEAD_DIM ** -0.5
    for b in range(tm // s):
        rows = slice(b * s, (b + 1) * s)
        for p in range(HEAD_PAIRS):
            cols = slice(p * LANES, (p + 1) * LANES)
            q2 = qkv_ref[rows, cols] * scale
            kb = qkv_ref[rows, d + p * LANES:d + (p + 1) * LANES].astype(BF16)
            vb = qkv_ref[rows, 2 * d + p * LANES:2 * d + (p + 1) * LANES].astype(BF16)
            qs = jnp.concatenate([jnp.where(first_half, q2, 0.0), jnp.where(first_half, 0.0, q2)], axis=0).astype(BF16)
            sco = _dot_nt(qs, kb)
            e = jnp.exp(sco - jnp.max(sco, axis=-1, keepdims=True))
            pv = _dot(e.astype(BF16), vb) / jnp.sum(e, axis=-1, keepdims=True)
            o_ref[rows, cols] = jnp.where(first_half, pv[:s], pv[s:]).astype(BF16)


def _qkv_ctx_attn(x, mod, layer, g, w, *, seq_len, row_base, rows_per_group, tm=512):
    m, d = x.shape
    nb = tm // seq_len
    cache_block = (nb, 1, seq_len, NA_HEADS, NA_HEAD_DIM)
    cache_shape = jax.ShapeDtypeStruct((m // seq_len, 1, seq_len, NA_HEADS, NA_HEAD_DIM), F32)
    kern = functools.partial(_qkv_ctx_attn_kernel, tm=tm, seq_len=seq_len, row_base=row_base,
                             rows_per_group=rows_per_group)
    return pl.pallas_call(
        kern,
        grid=(m // tm,),
        in_specs=[
            pl.BlockSpec((tm, d), lambda i: (i, 0)),
            pl.BlockSpec((1, 8, d), lambda i: (layer, 0, 0)),
            pl.BlockSpec((1, 8, d), lambda i: (layer, 0, 1)),
            pl.BlockSpec((1, d), lambda i: (0, 0)),
            _RESIDENT,
        ],
        out_specs=[
            pl.BlockSpec((tm, d), lambda i: (i, 0)),
            pl.BlockSpec(cache_block, lambda i: (i, 0, 0, 0, 0)),
            pl.BlockSpec(cache_block, lambda i: (i, 0, 0, 0, 0)),
        ],
        out_shape=[jax.ShapeDtypeStruct((m, d), BF16), cache_shape, cache_shape],
        scratch_shapes=[pltpu.VMEM((tm, 3 * d), F32)],
        compiler_params=_cparams(("arbitrary",)),
        name="qkv_ctx_attention",
    )(x, mod, mod, g.reshape(1, d), w)


def _lat_attn_kernel(q_ref, k_ref, v_ref, kc_ref, vc_ref, rpb_ref, o_ref, bias_ref, *, rows):
    b = pl.program_id(1)
    first_half = _pair_masks()
    kh = min(NA_KH, rows)
    win = kh * GRID_W
    neg_inf = -jnp.inf

    def win_start(r):
        return min(max(r - kh // 2, 0), rows - kh)

    @pl.when(b == 0)
    def _():
        qc = lax.broadcasted_iota(jnp.int32, (GRID_W, LANES), 0)
        kc = lax.broadcasted_iota(jnp.int32, (GRID_W, LANES), 1)
        cs = jnp.clip(qc - NA_KW // 2, 0, GRID_W - NA_KW)
        col_ok = (kc >= cs) & (kc < cs + NA_KW)
        for hh in range(2):
            for dr in range(2 * NA_KH - 1):
                v = jnp.broadcast_to(rpb_ref[hh, dr:dr + 1, :], (GRID_W, LANES))
                t = pltpu.roll(v, LANES - (NA_KW - 1), 1, stride=1, stride_axis=0)
                tile = jnp.where(col_ok, t, neg_inf)[:, :GRID_W]
                for r in range(rows):
                    i = dr - (NA_KH - 1) + r - win_start(r)
                    if 0 <= i < kh:
                        bias_ref[r, hh * GRID_W:(hh + 1) * GRID_W, i * GRID_W:(i + 1) * GRID_W] = tile

    scale = NA_HEAD_DIM ** -0.5
    kb = k_ref[...].astype(BF16)
    vb = v_ref[...].astype(BF16)
    kcb = kc_ref[...].astype(BF16)
    vcb = vc_ref[...].astype(BF16)
    group = 4
    for r0 in range(0, rows, group):
        rr = range(r0, r0 + group)
        qrows = [slice(r * GRID_W, (r + 1) * GRID_W) for r in rr]
        krows = [slice(win_start(r) * GRID_W, win_start(r) * GRID_W + win) for r in rr]
        qs = []
        for qr in qrows:
            q2 = q_ref[qr, :] * scale
            qs.append(jnp.concatenate([jnp.where(first_half, q2, 0.0), jnp.where(first_half, 0.0, q2)],
                                      axis=0).astype(BF16))
        s_loc = [_dot_nt(qi, kb[kr]) + bias_ref[r] for qi, kr, r in zip(qs, krows, rr)]
        s_ctx = [_dot_nt(qi, kcb) for qi in qs]
        mx = [jnp.maximum(jnp.max(sl, axis=-1, keepdims=True), jnp.max(sc, axis=-1, keepdims=True))
              for sl, sc in zip(s_loc, s_ctx)]
        e_loc = [jnp.exp(sl - m) for sl, m in zip(s_loc, mx)]
        e_ctx = [jnp.exp(sc - m) for sc, m in zip(s_ctx, mx)]
        den = [jnp.sum(el, axis=-1, keepdims=True) + jnp.sum(ec, axis=-1, keepdims=True) for el, ec in zip(e_loc, e_ctx)]
        pv = [(_dot(el.astype(BF16), vb[kr]) + _dot(ec.astype(BF16), vcb)) / dn
              for el, ec, kr, dn in zip(e_loc, e_ctx, krows, den)]
        for qr, o in zip(qrows, pv):
            o_ref[qr, :] = jnp.where(first_half, o[:GRID_W], o[GRID_W:]).astype(BF16)


def _lat_attn(q, k, v, kc, vc, rpb_pad, *, n_batch, n_tok, n_ctx):
    d = q.shape[1]
    rows = n_tok // GRID_W
    kh = min(NA_KH, rows)
    tok_spec = pl.BlockSpec((n_tok, LANES), lambda p, b: (b, p))
    ctx_spec = pl.BlockSpec((n_ctx, LANES), lambda p, b: (b, p))
    kern = functools.partial(_lat_attn_kernel, rows=rows)
    return pl.pallas_call(
        kern,
        grid=(HEAD_PAIRS, n_batch),
        in_specs=[tok_spec, tok_spec, tok_spec, ctx_spec, ctx_spec,
                  pl.BlockSpec((2, 2 * NA_KH, LANES), lambda p, b: (p, 0, 0))],
        out_specs=tok_spec,
        out_shape=jax.ShapeDtypeStruct((n_batch * n_tok, d), BF16),
        scratch_shapes=[pltpu.VMEM((rows, 2 * GRID_W, kh * GRID_W), F32)],
        compiler_params=_cparams(("arbitrary", "arbitrary")),
        name="latent_attention",
    )(q, k, v, kc, vc, rpb_pad)


def _pad_lanes(v):
    return jnp.pad(v, (0, LANES - v.shape[0]))


def _run_stream(x3, mod, row_base, W, *, is_ctx, state_ssd=None, cache_k=None, cache_v=None):
    b, s, d = x3.shape
    m = b * s
    x = x3.reshape(m, d)
    rpg = m if is_ctx else s
    kw = dict(row_base=row_base, rows_per_group=rpg)

    zs, x2, bca, yb, dtr = _in_proj(x, mod, W['norm_mix_g'][0], W['w_in'], W['w_in_tail'], W['ssd_conv_w'], W['ssd_conv_b'],
                                    W['sc_conv_w_rot'], seq_len=s, tm=max(s, 512), **kw)
    h0 = None
    if not is_ctx:
        h0 = state_ssd[:, 0].reshape(b, 2, HEAD_PAIRS, LANES, SSD_STATE)
    res = _ssd(zs.reshape(b, s, -1), x2.reshape(b, s, -1), bca.reshape(b, s, -1), dtr.reshape(b, s, LANES),
               W['ssd_prm'], W['ssd_norm_g'], h0, emit_state=is_ctx, n_elems=2 if is_ctx else 1)
    ya = res[0].reshape(m, SSD_INNER)
    x = _mix_ffn(x, mod, 0, [ya, yb], W['mix0_w_out'], W['norm_ffn_g'][0], *W['ffn'], W['final_norm_g'], seq_len=s,
                 final_norm=False, **kw)

    if is_ctx:
        o, new_k, new_v = _qkv_ctx_attn(x, mod, 1, W['norm_mix_g'][1], W['na_w_qkv'], seq_len=s, **kw)
    else:
        q, k, v = _nm_matmul(x, mod, 1, 0, 1, W['norm_mix_g'][1], W['na_w_qkv'], n_groups=3, group_width=D_MODEL, **kw)
        n_ctx = cache_k.shape[2]
        kc = cache_k[:, 0].reshape(b * n_ctx, d)
        vc = cache_v[:, 0].reshape(b * n_ctx, d)
        o = _lat_attn(q, k, v, kc, vc, W['rpb_pad'], n_batch=b, n_tok=s, n_ctx=n_ctx)
    x = _mix_ffn(x, mod, 1, [o], W['na_w_out'], W['norm_ffn_g'][1], *W['ffn'], W['final_norm_g'], seq_len=s,
                 final_norm=True, **kw)
    y = x.reshape(b, s, d)
    if is_ctx:
        new_state = res[1].reshape(b, 1, 2, SSD_HEADS, SSD_HEADDIM, SSD_STATE)
        return y, new_state, new_k, new_v
    return y


def kernel(x_prompt, x_sample, state_ssd, cache_k, cache_v, c, c_ctx, ada_w, ada_b, norm_mix_g, norm_ffn_g, ssd_w_in,
           ssd_conv_w, ssd_conv_b, ssd_dt_bias, ssd_a_log, ssd_d, ssd_norm_g, sc_conv_w, mix0_w_out, na_w_qkv, na_rpb,
           na_w_out, ffn_w_gate, ffn_w_up, ffn_conv_w, ffn_w_down, final_norm_g):
    n_lat = x_sample.shape[0]
    cvec = jnp.concatenate([c_ctx[None, :], c, jnp.zeros((8 - 1 - n_lat, D_MODEL), F32)], axis=0)
    mod = _ada(cvec, ada_w, ada_b)

    w_in = ssd_w_in[0].astype(BF16)
    w_in_tail = jnp.pad(ssd_w_in[0][:, TAIL_COL0:], ((0, 0), (0, LANES - MIX_SHIFT))).astype(BF16)
    w_out0 = mix0_w_out[0]
    w_out0 = jnp.concatenate([w_out0[:SSD_INNER], jnp.roll(w_out0[SSD_INNER:], MIX_SHIFT, axis=0)], axis=0).astype(BF16)
    prm =jnp.stack([_pad_lanes(ssd_dt_bias[0].reshape(-1)), _pad_lanes(ssd_a_log[0].reshape(-1)),
                     _pad_lanes(ssd_d[0, 0]), _pad_lanes(ssd_d[0, 1])] + [jnp.zeros((LANES,), F32)] * 4, axis=0)
    rpb = na_rpb[0]
    rpb_pad = jnp.pad(rpb, ((0, 0), (0, 1), (0, LANES - rpb.shape[2])))
    W = {
        'norm_mix_g': norm_mix_g, 'norm_ffn_g': norm_ffn_g, 'final_norm_g': final_norm_g,
        'w_in': w_in, 'w_in_tail': w_in_tail,
        'ssd_conv_w': ssd_conv_w[0], 'ssd_conv_b': ssd_conv_b[0].reshape(1, SSD_XBC), 'ssd_prm': prm,
        'ssd_norm_g': ssd_norm_g[0].reshape(1, SSD_INNER), 'sc_conv_w_rot': jnp.roll(sc_conv_w[0], MIX_SHIFT, axis=1),
        'mix0_w_out': w_out0,
        'na_w_qkv': na_w_qkv[0].astype(BF16), 'rpb_pad': rpb_pad, 'na_w_out': na_w_out[0].astype(BF16),
        'ffn': (ffn_w_gate.astype(BF16), ffn_w_up.astype(BF16), ffn_conv_w, ffn_w_down.astype(BF16)),
    }
    y_prompt, new_state, new_k, new_v = _run_stream(x_prompt, mod, 0, W, is_ctx=True)
    y_sample = _run_stream(x_sample, mod, 1, W, is_ctx=False, state_ssd=state_ssd, cache_k=cache_k, cache_v=cache_v)
    return (y_prompt, y_sample, new_state, new_k, new_v)
```

```python
import functools

import jax
import jax.numpy as jnp
from jax import lax
from jax.experimental import pallas as pl
from jax.experimental.pallas import tpu as pltpu

F32 = jnp.float32
BF16 = jnp.bfloat16

D_MODEL = 1024
EPS = 1e-6
GRID_W = 64
SSD_HEADDIM = 64
SSD_HEADS = 16
SSD_STATE = 128
SSD_GROUPS = 2
SSD_INNER = 1024
SSD_XBC = 1536
SSD_LC = 256
SC_WIDTH = 1024
NA_HEADS = 16
NA_HEAD_DIM = 64
NA_KH = 8
NA_KW = 16

LOG2E = 1.4426950408889634
LANES = 128
SUBLANES = 8
HEAD_PAIRS = 8
U_COLS = 5 * 1024 + 512

VMEM_LIMIT = 56 * 1024 * 1024

_RESIDENT = pl.BlockSpec(memory_space=pltpu.VMEM)


def _cparams(sem):
    return pltpu.CompilerParams(dimension_semantics=sem, vmem_limit_bytes=VMEM_LIMIT)


def _silu(x):
    return x * jax.nn.sigmoid(x)


def _dot(a, b):
    return jnp.dot(a, b, preferred_element_type=F32)


def _dot_nt(a, b):
    return lax.dot_general(a, b, (((1,), (1,)), ((), ())), preferred_element_type=F32)


def _split3(v):
    hi = v.astype(BF16)
    r1 = v - hi.astype(F32)
    mid = r1.astype(BF16)
    lo = (r1 - mid.astype(F32)).astype(BF16)
    return hi, mid, lo


def _sel_dot_left(sel, v):
    hi, mid, lo = _split3(v)
    return _dot(sel, hi) + _dot(sel, mid) + _dot(sel, lo)


def _norm_mod(x, g, shift, scale):
    ms = jnp.mean(x * x, axis=-1, keepdims=True)
    y = x * lax.rsqrt(ms + EPS) * g
    return y * (1.0 + scale) + shift


def _mod_row(i, tm, row_base, rows_per_group):
    return row_base + (i * tm) // rows_per_group


def _ada_kernel(c_ref, w_ref, b_ref, o_ref):
    s = _silu(c_ref[...]).astype(BF16)
    o_ref[0] = _dot(s, w_ref[0].astype(BF16)) + b_ref[0]


def _ada(cvec8, ada_w, ada_b):
    depth, d, n = ada_w.shape
    tn = 1536
    return pl.pallas_call(
        _ada_kernel,
        grid=(depth, n // tn),
        in_specs=[
            pl.BlockSpec((8, d), lambda l, j: (0, 0)),
            pl.BlockSpec((1, d, tn), lambda l, j: (l, 0, j)),
            pl.BlockSpec((1, 1, tn), lambda l, j: (l, 0, j)),
        ],
        out_specs=pl.BlockSpec((1, 8, tn), lambda l, j: (l, 0, j)),
        out_shape=jax.ShapeDtypeStruct((depth, 8, n), F32),
        compiler_params=_cparams(("arbitrary", "arbitrary")),
        name="ada_mod",
    )(cvec8, ada_w, ada_b.reshape(depth, 1, n))


def _nm_matmul_kernel(*refs, n_groups, group_width, has_extra, tm, tn, row_base, rows_per_group):
    x_ref, sh_ref, sc_ref, g_ref, w_ref = refs[:5]
    pos = 5
    if has_extra:
        we_ref = refs[pos]
        pos += 1
    o_refs = refs[pos:pos + n_groups]
    pos += n_groups
    if has_extra:
        oe_ref = refs[pos]
    r = _mod_row(pl.program_id(0), tm, row_base, rows_per_group)
    sh = sh_ref[0, pl.ds(r, 1), :]
    sc = sc_ref[0, pl.ds(r, 1), :]
    h = _norm_mod(x_ref[...], g_ref[...], sh, sc).astype(BF16)
    if has_extra:
        oe_ref[...] = _dot(h, we_ref[...])
    for gi, o_ref in enumerate(o_refs):
        for j in range(group_width // tn):
            res = _dot(h, w_ref[:, gi * group_width + j * tn:gi * group_width + (j + 1) * tn])
            o_ref[:, j * tn:(j + 1) * tn] = res.astype(o_ref.dtype)


def _nm_matmul(x, mod, layer, shift_idx, scale_idx, g, w, *, n_groups, group_width, row_base, rows_per_group,
               w_extra=None, out_dtype=F32, tm=512, tn=512):
    m, d = x.shape
    has_extra = w_extra is not None
    in_specs = [
        pl.BlockSpec((tm, d), lambda i: (i, 0)),
        pl.BlockSpec((1, 8, d), lambda i: (layer, 0, shift_idx)),
        pl.BlockSpec((1, 8, d), lambda i: (layer, 0, scale_idx)),
        pl.BlockSpec((1, d), lambda i: (0, 0)),
        _RESIDENT,
    ]
    args = [x, mod, mod, g.reshape(1, d), w]
    out_specs = [pl.BlockSpec((tm, group_width), lambda i: (i, 0)) for _ in range(n_groups)]
    out_shape = [jax.ShapeDtypeStruct((m, group_width), out_dtype) for _ in range(n_groups)]
    if has_extra:
        ne = w_extra.shape[1]
        in_specs.append(_RESIDENT)
        args.append(w_extra)
        out_specs.append(pl.BlockSpec((tm, ne), lambda i: (i, 0)))
        out_shape.append(jax.ShapeDtypeStruct((m, ne), F32))
    kern = functools.partial(_nm_matmul_kernel, n_groups=n_groups, group_width=group_width, has_extra=has_extra, tm=tm,
                             tn=tn, row_base=row_base, rows_per_group=rows_per_group)
    return pl.pallas_call(
        kern,
        grid=(m // tm,),
        in_specs=in_specs,
        out_specs=out_specs,
        out_shape=out_shape,
        compiler_params=_cparams(("arbitrary",)),
        name="norm_mod_matmul",
    )(*args)


def _conv_scratch(rows, cols, seq_len):
    return pltpu.VMEM((cols // LANES, rows + SUBLANES * (rows // seq_len + 3), LANES), F32)


def _dwconv3(v, w, seq_len, cv_ref, row0=0):
    rows, ch = v.shape
    dyn0 = pl.program_id(0) * 0
    cuts = [0] + [t for t in range(1, rows) if (row0 + t) % seq_len == 0] + [rows]
    segs = list(zip(cuts[:-1], cuts[1:]))
    zeros = jnp.zeros((SUBLANES, LANES), F32)
    slabs = []
    for j in range(ch // LANES):
        cl = slice(j * LANES, (j + 1) * LANES)
        base = SUBLANES
        cv_ref[j, 0:SUBLANES, :] = zeros
        bases = []
        for a, b in segs:
            cv_ref[j, base:base + (b - a), :] = v[a:b, cl]
            cv_ref[j, base + (b - a):base + (b - a) + SUBLANES, :] = zeros
            bases.append(base)
            base += (b - a) + SUBLANES
        parts = []
        for (a, b), bs in zip(segs, bases):
            prev = cv_ref[j, pl.ds(bs - 1 + dyn0, b - a), :]
            nxt = cv_ref[j, pl.ds(bs + 1 + dyn0, b - a), :]
            parts.append(prev * w[0:1, cl] + v[a:b, cl] * w[1:2, cl] + nxt * w[2:3, cl])
        slabs.append(jnp.concatenate(parts, axis=0) if len(parts) > 1 else parts[0])
    return jnp.concatenate(slabs, axis=1) if len(slabs) > 1 else slabs[0]


MIX_SHIFT = 2 * SSD_HEADS
DT_COL0 = SSD_INNER + SSD_XBC
TAIL_COL0 = DT_COL0 + 3 * SC_WIDTH
IN_PROJ_TN = 256


def _in_proj_kernel(x_ref, sh_ref, sc_ref, g_ref, w_ref, wt_ref, cw_ref, cb_ref, scw_ref, zs_ref, x2_ref, bc_ref, yb_ref,
                    dt_ref, cv_ref,
                    *, tm, row_base, rows_per_group, seq_len):
    r = _mod_row(pl.program_id(0), tm, row_base, rows_per_group)
    sh = sh_ref[0, pl.ds(r, 1), :]
    sc = sc_ref[0, pl.ds(r, 1), :]
    h = _norm_mod(x_ref[...], g_ref[...], sh, sc).astype(BF16)
    tn = IN_PROJ_TN

    def proj(c0, width=tn):
        return _dot(h, w_ref[:, c0:c0 + width])

    for j in range(SSD_INNER // tn):
        cols = slice(j * tn, (j + 1) * tn)
        zs_ref[:, cols] = _silu(proj(j * tn)).astype(BF16)
        xc = _dwconv3(proj(SSD_INNER + j * tn), cw_ref[:, cols], seq_len, cv_ref) + cb_ref[:, cols]
        x2_ref[:, cols] = _silu(xc).astype(BF16)
    for j in range((SSD_XBC - SSD_INNER) // tn):
        cols = slice(SSD_INNER + j * tn, SSD_INNER + (j + 1) * tn)
        bcc = _dwconv3(proj(SSD_INNER + cols.start), cw_ref[:, cols], seq_len, cv_ref) + cb_ref[:, cols]
        bc_ref[:, j * tn:(j + 1) * tn] = _silu(bcc).astype(BF16)
    dt_ref[...] = proj(DT_COL0, LANES)
    q3 = _dot(h, wt_ref[...])
    tail = lax.broadcasted_iota(jnp.int32, (1, LANES), 1) < MIX_SHIFT
    for j in range(SC_WIDTH // tn):
        c0 = DT_COL0 + j * tn
        cols = slice(j * tn, (j + 1) * tn)
        p0, p1, p2 = proj(c0), proj(c0 + SC_WIDTH), proj(c0 + 2 * SC_WIDTH)
        y = p0 * _dwconv3(p1 * p2, scw_ref[:, cols], seq_len, cv_ref)
        if j == 0:
            yt = p1[:, :LANES] * _dwconv3(p2[:, :LANES] * q3, scw_ref[:, :LANES], seq_len, cv_ref)
            y = jnp.concatenate([jnp.where(tail, yt, y[:, :LANES]), y[:, LANES:]], axis=1)
        yb_ref[:, cols] = y.astype(BF16)


def _in_proj(x, mod, g, w, w_tail, conv_w, conv_b, sc_conv_w_rot, *, seq_len, row_base, rows_per_group, tm):
    m, d = x.shape
    kern = functools.partial(_in_proj_kernel, tm=tm, row_base=row_base, rows_per_group=rows_per_group, seq_len=seq_len)
    widths = (SSD_INNER, SSD_INNER, SSD_XBC - SSD_INNER, SC_WIDTH, LANES)
    dtypes = (BF16, BF16, BF16, BF16, F32)
    return pl.pallas_call(
        kern,
        grid=(m // tm,),
        in_specs=[
            pl.BlockSpec((tm, d), lambda i: (i, 0)),
            pl.BlockSpec((1, 8, d), lambda i: (0, 0, 0)),
            pl.BlockSpec((1, 8, d), lambda i: (0, 0, 1)),
            pl.BlockSpec((1, d), lambda i: (0, 0)),
            _RESIDENT,
            _RESIDENT,
            pl.BlockSpec((3, SSD_XBC), lambda i: (0, 0)),
            pl.BlockSpec((1, SSD_XBC), lambda i: (0, 0)),
            pl.BlockSpec((3, SC_WIDTH), lambda i: (0, 0)),
        ],
        out_specs=[pl.BlockSpec((tm, wd), lambda i: (i, 0)) for wd in widths],
        out_shape=[jax.ShapeDtypeStruct((m, wd), dt) for wd, dt in zip(widths, dtypes)],
        scratch_shapes=[_conv_scratch(tm, IN_PROJ_TN, seq_len)],
        compiler_params=_cparams(("arbitrary",)),
        name="in_proj",
    )(x, mod, mod, g.reshape(1, d), w, w_tail, conv_w, conv_b, sc_conv_w_rot)


def _ffn_kernel(*refs, n_parts, tm, tf, row_base, rows_per_group, seq_len, final_norm):
    x_ref, gate1_ref, sh_ref, sc_ref, gate_ref, g_ref, wo_ref = refs[:7]
    a_refs = refs[7:7 + n_parts]
    wg_ref, wu_ref, cw_ref, wd_ref, fg_ref, o_ref, h_ref, act_ref, cv_ref = refs[7 + n_parts:]
    r = _mod_row(pl.program_id(0), tm, row_base, rows_per_group)
    sh = sh_ref[0, pl.ds(r, 1), :]
    sc = sc_ref[0, pl.ds(r, 1), :]
    gate = gate_ref[0, pl.ds(r, 1), :]
    kp = wo_ref.shape[0] // n_parts
    sub = tm // 4
    for q in range(tm // sub):
        rq = slice(q * sub, (q + 1) * sub)
        mix = _dot(a_refs[0][rq, :], wo_ref[0:kp, :])
        for k in range(1, n_parts):
            mix = mix + _dot(a_refs[k][rq, :], wo_ref[k * kp:(k + 1) * kp, :])
        x1 = x_ref[rq, :] + gate1_ref[0, pl.ds(r, 1), :] * mix
        o_ref[rq, :] = x1
        h_ref[rq, :] = _norm_mod(x1, g_ref[...], sh, sc).astype(BF16)
    dff = wg_ref.shape[1]
    half = tm // 2
    win = half + SUBLANES
    for hb in range(2):
        w0 = hb * (half - SUBLANES)
        v0 = hb * SUBLANES
        rows = slice(hb * half, (hb + 1) * half)
        hw = h_ref[w0:w0 + win, :]
        hv = h_ref[rows, :]
        for fc in range(dff // tf):
            cols = slice(fc * tf, (fc + 1) * tf)
            gpre = _dwconv3(_dot(hw, wg_ref[:, cols]), cw_ref[:, cols], seq_len, cv_ref, row0=w0)[v0:v0 + half]
            up = _dot(hv, wu_ref[:, cols])
            act_ref[:, cols] = (jax.nn.gelu(gpre, approximate=True) * up).astype(BF16)
        y = o_ref[rows, :] + gate * _dot(act_ref[...], wd_ref[...])
        if final_norm:
            ms = jnp.mean(y * y, axis=-1, keepdims=True)
            y = y * lax.rsqrt(ms + EPS) * fg_ref[...]
        o_ref[rows, :] = y


def _layer_weight(shape):
    return lambda layer: pl.BlockSpec((None,) + shape, lambda i: (layer,) + (0,) * len(shape),
                                      pipeline_mode=pl.Buffered(1))


def _mix_ffn(x, mod, layer, parts, wo, g, wg, wu, cw, wd, final_g, *, row_base, rows_per_group, seq_len, final_norm,
             tm=1024, tf=256):
    m, d = x.shape
    dff = wg.shape[2]
    n_parts = len(parts)
    kern = functools.partial(_ffn_kernel, n_parts=n_parts, tm=tm, tf=tf, row_base=row_base,
                             rows_per_group=rows_per_group, seq_len=seq_len, final_norm=final_norm)
    mod_spec = lambda k: pl.BlockSpec((1, 8, d), lambda i: (layer, 0, k))
    in_specs = [pl.BlockSpec((tm, d), lambda i: (i, 0)), mod_spec(2), mod_spec(3), mod_spec(4), mod_spec(5),
                pl.BlockSpec((1, d), lambda i: (0, 0)), _RESIDENT]
    in_specs += [pl.BlockSpec((tm, a.shape[1]), lambda i: (i, 0)) for a in parts]
    in_specs += [_layer_weight((d, dff))(layer), _layer_weight((d, dff))(layer), _layer_weight((3, dff))(layer),
                 _layer_weight((dff, d))(layer), pl.BlockSpec((1, d), lambda i: (0, 0))]
    return pl.pallas_call(
        kern,
        grid=(m // tm,),
        in_specs=in_specs,
        out_specs=pl.BlockSpec((tm, d), lambda i: (i, 0)),
        out_shape=jax.ShapeDtypeStruct((m, d), F32),
        scratch_shapes=[pltpu.VMEM((tm, d), BF16), pltpu.VMEM((tm // 2, dff), BF16),
                        _conv_scratch(tm // 2 + SUBLANES, tf, seq_len)],
        compiler_params=_cparams(("arbitrary",)),
        name="mix_ffn",
    )(x, mod, mod, mod, mod, g.reshape(1, d), wo, *parts, wg, wu, cw, wd, final_g.reshape(1, d))


def _pair_cols(first_half, arr, h0, h1):
    return jnp.where(first_half, arr[:, h0:h0 + 1], arr[:, h1:h1 + 1])


def _pair_rows(arr_t, h0, h1, width):
    return jnp.concatenate([jnp.broadcast_to(arr_t[h0:h0 + 1, :], (SSD_HEADDIM, width)),
                            jnp.broadcast_to(arr_t[h1:h1 + 1, :], (SSD_HEADDIM, width))], axis=0)


def _pair_scalars(row, h0, h1):
    return jnp.concatenate([jnp.broadcast_to(row[:, h0:h0 + 1], (SSD_HEADDIM, SSD_STATE)),
                            jnp.broadcast_to(row[:, h1:h1 + 1], (SSD_HEADDIM, SSD_STATE))], axis=0)


def _ssd_kernel(*refs, n_elems, **kw):
    for e in range(n_elems):
        _ssd_element(e, *refs, **kw)


def _ssd_element(e, *refs, seq_len, has_h0, emit_state):
    zs_ref, x2_ref, bc_ref, dt_ref, prm_ref, ng_ref = refs[:6]
    pos = 6
    if has_h0:
        h0_ref = refs[pos]
        pos += 1
    ya_ref = refs[pos]
    pos += 1
    if emit_state:
        st_ref = refs[pos]
        pos += 1
    yacc_ref = refs[pos]

    lc = SSD_LC
    nchunk = seq_len // lc
    nh = SSD_HEADS
    lane = lax.broadcasted_iota(jnp.int32, (1, LANES), 1)
    first_half = lane < SSD_HEADDIM
    fwd_lane = lane < nh

    bc = bc_ref[e]
    valid = lane < 2 * nh
    dt = jnp.where(valid, jax.nn.softplus(dt_ref[e] + prm_ref[0:1, :]), 0.0)
    a = -jnp.exp(prm_ref[1:2, :])
    dta = jnp.where(valid, dt * a, 0.0)
    dsum = prm_ref[2:3, :] + prm_ref[3:4, :]
    tt = lax.broadcasted_iota(jnp.int32, (lc, lc), 0)
    ss = lax.broadcasted_iota(jnp.int32, (lc, lc), 1)
    tril = jnp.where(ss <= tt, 1.0, 0.0).astype(BF16)
    triu = jnp.where(ss >= tt, 1.0, 0.0).astype(BF16)
    nq = lc // LANES
    td = lax.broadcasted_iota(jnp.int32, (LANES, LANES), 0)
    sd = lax.broadcasted_iota(jnp.int32, (LANES, LANES), 1)
    lower_d = td > sd
    upper_d = td < sd

    acs2_l, rowt_l, wts_l, eacs_l, edec_l, cbm_l, bm_l, cm_l = [], [], [], [], [], [], [], []
    for c in range(nchunk):
        rows = slice(c * lc, (c + 1) * lc)
        acs = jnp.where(fwd_lane, _sel_dot_left(tril, dta[rows]), _sel_dot_left(triu, dta[rows]))
        edge = jnp.where(fwd_lane, acs[lc - 1:lc, :], acs[0:1, :])
        acs2 = acs * LOG2E
        dtt = dt[rows].T
        rowt = acs2.T - jnp.log2(dtt)
        dgt = jnp.log2(dtt[0:nh, :] + dtt[nh:2 * nh, :])
        acs2_l.append(acs2)
        rowt_l.append((rowt, dgt))
        wts_l.append((dt[rows] * jnp.exp(edge - acs)).T)
        eacs_l.append(jnp.exp(acs))
        edec_l.append(jnp.exp(edge))
        cbm_g, bm_g, cm_g = [], [], []
        for g in range(SSD_GROUPS):
            bm = bc[rows, g * SSD_STATE:(g + 1) * SSD_STATE]
            cm = bc[rows, (SSD_GROUPS + g) * SSD_STATE:(SSD_GROUPS + g + 1) * SSD_STATE]
            bm_g.append(bm)
            cm_g.append(cm)
            cbm_g.append(_dot_nt(cm, bm))
        cbm_l.append(cbm_g)
        bm_l.append(bm_g)
        cm_l.append(cm_g)

    for p in range(HEAD_PAIRS):
        cols = slice(p * LANES, (p + 1) * LANES)
        g = p // (HEAD_PAIRS // SSD_GROUPS)
        hd = (2 * p, 2 * p + 1)
        x2b = x2_ref[e, :, cols]
        x2 = x2b.astype(F32)
        dsum2 = _pair_cols(first_half, dsum, hd[0], hd[1])
        if has_h0:
            hf = h0_ref[e, 0, p]
            hb = h0_ref[e, 1, p]
        else:
            hf = jnp.zeros((LANES, SSD_STATE), F32)
            hb = jnp.zeros((LANES, SSD_STATE), F32)
        ys = []
        stbs = []
        for c in range(nchunk):
            rows = slice(c * lc, (c + 1) * lc)
            x2c = x2[rows]
            x2cb = x2b[rows]
            acs2 = acs2_l[c]
            rowt, dgt = rowt_l[c]
            cbm = cbm_l[c][g]
            outs = []
            for h in hd:
                blocks = []
                for bi in range(nq):
                    ri = slice(bi * LANES, (bi + 1) * LANES)
                    cf = acs2[ri, h:h + 1]
                    cbk = acs2[ri, nh + h:nh + h + 1]
                    row_blocks = []
                    for bj in range(nq):
                        cj = slice(bj * LANES, (bj + 1) * LANES)
                        rf = rowt[h:h + 1, cj]
                        rb = rowt[nh + h:nh + h + 1, cj]
                        if bi > bj:
                            arg = cf - rf
                        elif bi < bj:
                            arg = cbk - rb
                        else:
                            arg = jnp.where(lower_d, cf - rf, jnp.where(upper_d, cbk - rb, dgt[h:h + 1, cj]))
                        row_blocks.append((cbm[ri, cj] * jnp.exp2(arg)).astype(BF16))
                    blocks.append(jnp.concatenate(row_blocks, axis=1))
                gm = jnp.concatenate(blocks, axis=0)
                outs.append(_dot(gm, x2cb))
            y = jnp.where(first_half, outs[0], outs[1]) + x2c * dsum2
            xt = x2c.T
            xwf = (xt * _pair_rows(wts_l[c], hd[0], hd[1], lc)).astype(BF16)
            xwb = (xt * _pair_rows(wts_l[c], nh + hd[0], nh + hd[1], lc)).astype(BF16)
            stf = _dot(xwf, bm_l[c][g])
            stbs.append(_dot(xwb, bm_l[c][g]))
            if has_h0 or c > 0:
                y = y + _dot_nt(cm_l[c][g], hf.astype(BF16)) * _pair_cols(first_half, eacs_l[c], hd[0], hd[1])
            hf = hf * _pair_scalars(edec_l[c], hd[0], hd[1]) + stf
            ys.append(y)
        for c in reversed(range(nchunk)):
            if has_h0 or c < nchunk - 1:
                ys[c] = ys[c] + (_dot_nt(cm_l[c][g], hb.astype(BF16))
                                 * _pair_cols(first_half, eacs_l[c], nh + hd[0], nh + hd[1]))
            hb = hb * _pair_scalars(edec_l[c], nh + hd[0], nh + hd[1]) + stbs[c]
        zg = zs_ref[e, :, cols].astype(F32)
        for c in range(nchunk):
            rows = slice(c * lc, (c + 1) * lc)
            yacc_ref[rows, cols] = ys[c] * zg[rows]
        if emit_state:
            st_ref[e, 0, p] = hf
            st_ref[e, 1, p] = hb

    y = yacc_ref[...]
    ms = jnp.mean(y * y, axis=-1, keepdims=True)
    ya_ref[e] = (y * lax.rsqrt(ms + EPS) * ng_ref[...]).astype(BF16)


def _ssd(zs3, x23, bc3, dt3, prm, norm_g, h0, *, emit_state, n_elems):
    b, s, _ = zs3.shape
    has_h0 = h0 is not None
    ne = n_elems
    in_specs = [
        pl.BlockSpec((ne, s, SSD_INNER), lambda i: (i, 0, 0)),
        pl.BlockSpec((ne, s, SSD_INNER), lambda i: (i, 0, 0)),
        pl.BlockSpec((ne, s, SSD_XBC - SSD_INNER), lambda i: (i, 0, 0)),
        pl.BlockSpec((ne, s, LANES), lambda i: (i, 0, 0)),
        pl.BlockSpec((8, LANES), lambda i: (0, 0)),
        pl.BlockSpec((1, SSD_INNER), lambda i: (0, 0)),
    ]
    args = [zs3, x23, bc3, dt3, prm, norm_g]
    state_block = (ne, 2, HEAD_PAIRS, LANES, SSD_STATE)
    if has_h0:
        in_specs.append(pl.BlockSpec(state_block, lambda i: (i, 0, 0, 0, 0)))
        args.append(h0)
    out_specs = [pl.BlockSpec((ne, s, SSD_INNER), lambda i: (i, 0, 0))]
    out_shape = [jax.ShapeDtypeStruct((b, s, SSD_INNER), BF16)]
    if emit_state:
        out_specs.append(pl.BlockSpec(state_block, lambda i: (i, 0, 0, 0, 0)))
        out_shape.append(jax.ShapeDtypeStruct((b, 2, HEAD_PAIRS, LANES, SSD_STATE), F32))
    kern = functools.partial(_ssd_kernel, n_elems=ne, seq_len=s, has_h0=has_h0, emit_state=emit_state)
    return pl.pallas_call(
        kern,
        grid=(b // ne,),
        in_specs=in_specs,
        out_specs=out_specs,
        out_shape=out_shape,
        scratch_shapes=[pltpu.VMEM((s, SSD_INNER), F32)],
        compiler_params=_cparams(("arbitrary",)),
        name="ssd",
    )(*args)


def _pair_masks():
    lane = lax.broadcasted_iota(jnp.int32, (1, LANES), 1)
    return lane < NA_HEAD_DIM


def _qkv_ctx_attn_kernel(x_ref, sh_ref, sc_ref, g_ref, w_ref, o_ref, nk_ref, nv_ref, qkv_ref,
                         *, tm, seq_len, row_base, rows_per_group):
    r = _mod_row(pl.program_id(0), tm, row_base, rows_per_group)
    sh = sh_ref[0, pl.ds(r, 1), :]
    sc = sc_ref[0, pl.ds(r, 1), :]
    h = _norm_mod(x_ref[...], g_ref[...], sh, sc).astype(BF16)
    d = x_ref.shape[1]
    tn = 512
    s = seq_len
    for part, out_ref in ((1, nk_ref), (2, nv_ref)):
        for j in range(d // tn):
            c0 = part * d + j * tn
            qkv_ref[:, c0:c0 + tn] = _dot(h, w_ref[:, c0:c0 + tn])
        for b in range(tm // s):
            rows = slice(b * s, (b + 1) * s)
            out_ref[b, 0] = qkv_ref[rows, part * d:(part + 1) * d].reshape(s, NA_HEADS, NA_HEAD_DIM)
    for j in range(d // tn):
        qkv_ref[:, j * tn:(j + 1) * tn] = _dot(h, w_ref[:, j * tn:(j + 1) * tn])
    first_half = _pair_masks()
    scale = NA_HEAD_DIM ** -0.5
    for b in range(tm // s):
        rows = slice(b * s, (b + 1) * s)
        for p in range(HEAD_PAIRS):
            cols = slice(p * LANES, (p + 1) * LANES)
            q2 = qkv_ref[rows, cols] * scale
            kb = qkv_ref[rows, d + p * LANES:d + (p + 1) * LANES].astype(BF16)
            vb = qkv_ref[rows, 2 * d + p * LANES:2 * d + (p + 1) * LANES].astype(BF16)
            qs = jnp.concatenate([jnp.where(first_half, q2, 0.0), jnp.where(first_half, 0.0, q2)], axis=0).astype(BF16)
            sco = _dot_nt(qs, kb)
            e = jnp.exp(sco - jnp.max(sco, axis=-1, keepdims=True))
            pv = _dot(e.astype(BF16), vb) / jnp.sum(e, axis=-1, keepdims=True)
            o_ref[rows, cols] = jnp.where(first_half, pv[:s], pv[s:]).astype(BF16)


def _qkv_ctx_attn(x, mod, layer, g, w, *, seq_len, row_base, rows_per_group, tm=512):
    m, d = x.shape
    nb = tm // seq_len
    cache_block = (nb, 1, seq_len, NA_HEADS, NA_HEAD_DIM)
    cache_shape = jax.ShapeDtypeStruct((m // seq_len, 1, seq_len, NA_HEADS, NA_HEAD_DIM), F32)
    kern = functools.partial(_qkv_ctx_attn_kernel, tm=tm, seq_len=seq_len, row_base=row_base,
                             rows_per_group=rows_per_group)
    return pl.pallas_call(
        kern,
        grid=(m // tm,),
        in_specs=[
            pl.BlockSpec((tm, d), lambda i: (i, 0)),
            pl.BlockSpec((1, 8, d), lambda i: (layer, 0, 0)),
            pl.BlockSpec((1, 8, d), lambda i: (layer, 0, 1)),
            pl.BlockSpec((1, d), lambda i: (0, 0)),
            _RESIDENT,
        ],
        out_specs=[
            pl.BlockSpec((tm, d), lambda i: (i, 0)),
            pl.BlockSpec(cache_block, lambda i: (i, 0, 0, 0, 0)),
            pl.BlockSpec(cache_block, lambda i: (i, 0, 0, 0, 0)),
        ],
        out_shape=[jax.ShapeDtypeStruct((m, d), BF16), cache_shape, cache_shape],
        scratch_shapes=[pltpu.VMEM((tm, 3 * d), F32)],
        compiler_params=_cparams(("arbitrary",)),
        name="qkv_ctx_attention",
    )(x, mod, mod, g.reshape(1, d), w)


def _lat_attn_kernel(q_ref, k_ref, v_ref, kc_ref, vc_ref, rpb_ref, o_ref, bias_ref, *, rows):
    b = pl.program_id(1)
    first_half = _pair_masks()
    kh = min(NA_KH, rows)
    win = kh * GRID_W
    neg_inf = -jnp.inf

    def win_start(r):
        return min(max(r - kh // 2, 0), rows - kh)

    @pl.when(b == 0)
    def _():
        qc = lax.broadcasted_iota(jnp.int32, (GRID_W, LANES), 0)
        kc = lax.broadcasted_iota(jnp.int32, (GRID_W, LANES), 1)
        cs = jnp.clip(qc - NA_KW // 2, 0, GRID_W - NA_KW)
        col_ok = (kc >= cs) & (kc < cs + NA_KW)
        for hh in range(2):
            for dr in range(2 * NA_KH - 1):
                v = jnp.broadcast_to(rpb_ref[hh, dr:dr + 1, :], (GRID_W, LANES))
                t = pltpu.roll(v, LANES - (NA_KW - 1), 1, stride=1, stride_axis=0)
                tile = jnp.where(col_ok, t, neg_inf)[:, :GRID_W]
                for r in range(rows):
                    i = dr - (NA_KH - 1) + r - win_start(r)
                    if 0 <= i < kh:
                        bias_ref[r, hh * GRID_W:(hh + 1) * GRID_W, i * GRID_W:(i + 1) * GRID_W] = tile

    scale = NA_HEAD_DIM ** -0.5
    kb = k_ref[...].astype(BF16)
    vb = v_ref[...].astype(BF16)
    kcb = kc_ref[...].astype(BF16)
    vcb = vc_ref[...].astype(BF16)
    group = 4
    for r0 in range(0, rows, group):
        rr = range(r0, r0 + group)
        qrows = [slice(r * GRID_W, (r + 1) * GRID_W) for r in rr]
        krows = [slice(win_start(r) * GRID_W, win_start(r) * GRID_W + win) for r in rr]
        qs = []
        for qr in qrows:
            q2 = q_ref[qr, :] * scale
            qs.append(jnp.concatenate([jnp.where(first_half, q2, 0.0), jnp.where(first_half, 0.0, q2)],
                                      axis=0).astype(BF16))
        s_loc = [_dot_nt(qi, kb[kr]) + bias_ref[r] for qi, kr, r in zip(qs, krows, rr)]
        s_ctx = [_dot_nt(qi, kcb) for qi in qs]
        mx = [jnp.maximum(jnp.max(sl, axis=-1, keepdims=True), jnp.max(sc, axis=-1, keepdims=True))
              for sl, sc in zip(s_loc, s_ctx)]
        e_loc = [jnp.exp(sl - m) for sl, m in zip(s_loc, mx)]
        e_ctx = [jnp.exp(sc - m) for sc, m in zip(s_ctx, mx)]
        den = [jnp.sum(el, axis=-1, keepdims=True) + jnp.sum(ec, axis=-1, keepdims=True) for el, ec in zip(e_loc, e_ctx)]
        pv = [(_dot(el.astype(BF16), vb[kr]) + _dot(ec.astype(BF16), vcb)) / dn
              for el, ec, kr, dn in zip(e_loc, e_ctx, krows, den)]
        for qr, o in zip(qrows, pv):
            o_ref[qr, :] = jnp.where(first_half, o[:GRID_W], o[GRID_W:]).astype(BF16)


def _lat_attn(q, k, v, kc, vc, rpb_pad, *, n_batch, n_tok, n_ctx):
    d = q.shape[1]
    rows = n_tok // GRID_W
    kh = min(NA_KH, rows)
    tok_spec = pl.BlockSpec((n_tok, LANES), lambda p, b: (b, p))
    ctx_spec = pl.BlockSpec((n_ctx, LANES), lambda p, b: (b, p))
    kern = functools.partial(_lat_attn_kernel, rows=rows)
    return pl.pallas_call(
        kern,
        grid=(HEAD_PAIRS, n_batch),
        in_specs=[tok_spec, tok_spec, tok_spec, ctx_spec, ctx_spec,
                  pl.BlockSpec((2, 2 * NA_KH, LANES), lambda p, b: (p, 0, 0))],
        out_specs=tok_spec,
        out_shape=jax.ShapeDtypeStruct((n_batch * n_tok, d), BF16),
        scratch_shapes=[pltpu.VMEM((rows, 2 * GRID_W, kh * GRID_W), F32)],
        compiler_params=_cparams(("arbitrary", "arbitrary")),
        name="latent_attention",
    )(q, k, v, kc, vc, rpb_pad)


def _pad_lanes(v):
    return jnp.pad(v, (0, LANES - v.shape[0]))


def _run_stream(x3, mod, row_base, W, *, is_ctx, state_ssd=None, cache_k=None, cache_v=None):
    b, s, d = x3.shape
    m = b * s
    x = x3.reshape(m, d)
    rpg = m if is_ctx else s
    kw = dict(row_base=row_base, rows_per_group=rpg)

    zs, x2, bca, yb, dtr = _in_proj(x, mod, W['norm_mix_g'][0], W['w_in'], W['w_in_tail'], W['ssd_conv_w'], W['ssd_conv_b'],
                                    W['sc_conv_w_rot'], seq_len=s, tm=max(s, 512), **kw)
    h0 = None
    if not is_ctx:
        h0 = state_ssd[:, 0].reshape(b, 2, HEAD_PAIRS, LANES, SSD_STATE)
    res = _ssd(zs.reshape(b, s, -1), x2.reshape(b, s, -1), bca.reshape(b, s, -1), dtr.reshape(b, s, LANES),
               W['ssd_prm'], W['ssd_norm_g'], h0, emit_state=is_ctx, n_elems=2 if is_ctx else 1)
    ya = res[0].reshape(m, SSD_INNER)
    x = _mix_ffn(x, mod, 0, [ya, yb], W['mix0_w_out'], W['norm_ffn_g'][0], *W['ffn'], W['final_norm_g'], seq_len=s,
                 final_norm=False, **kw)

    if is_ctx:
        o, new_k, new_v = _qkv_ctx_attn(x, mod, 1, W['norm_mix_g'][1], W['na_w_qkv'], seq_len=s, **kw)
    else:
        q, k, v = _nm_matmul(x, mod, 1, 0, 1, W['norm_mix_g'][1], W['na_w_qkv'], n_groups=3, group_width=D_MODEL, **kw)
        n_ctx = cache_k.shape[2]
        kc = cache_k[:, 0].reshape(b * n_ctx, d)
        vc = cache_v[:, 0].reshape(b * n_ctx, d)
        o = _lat_attn(q, k, v, kc, vc, W['rpb_pad'], n_batch=b, n_tok=s, n_ctx=n_ctx)
    x = _mix_ffn(x, mod, 1, [o], W['na_w_out'], W['norm_ffn_g'][1], *W['ffn'], W['final_norm_g'], seq_len=s,
                 final_norm=True, **kw)
    y = x.reshape(b, s, d)
    if is_ctx:
        new_state = res[1].reshape(b, 1, 2, SSD_HEADS, SSD_HEADDIM, SSD_STATE)
        return y, new_state, new_k, new_v
    return y


def kernel(x_prompt, x_sample, state_ssd, cache_k, cache_v, c, c_ctx, ada_w, ada_b, norm_mix_g, norm_ffn_g, ssd_w_in,
           ssd_conv_w, ssd_conv_b, ssd_dt_bias, ssd_a_log, ssd_d, ssd_norm_g, sc_conv_w, mix0_w_out, na_w_qkv, na_rpb,
           na_w_out, ffn_w_gate, ffn_w_up, ffn_conv_w, ffn_w_down, final_norm_g):
    n_lat = x_sample.shape[0]
    cvec = jnp.concatenate([c_ctx[None, :], c, jnp.zeros((8 - 1 - n_lat, D_MODEL), F32)], axis=0)
    mod = _ada(cvec, ada_w, ada_b)

    w_in = ssd_w_in[0].astype(BF16)
    w_in_tail = jnp.pad(ssd_w_in[0][:, TAIL_COL0:], ((0, 0), (0, LANES - MIX_SHIFT))).astype(BF16)
    w_out0 = mix0_w_out[0]
    w_out0 = jnp.concatenate([w_out0[:SSD_INNER], jnp.roll(w_out0[SSD_INNER:], MIX_SHIFT, axis=0)], axis=0).astype(BF16)
    prm =jnp.stack([_pad_lanes(ssd_dt_bias[0].reshape(-1)), _pad_lanes(ssd_a_log[0].reshape(-1)),
                     _pad_lanes(ssd_d[0, 0]), _pad_lanes(ssd_d[0, 1])] + [jnp.zeros((LANES,), F32)] * 4, axis=0)
    rpb = na_rpb[0]
    rpb_pad = jnp.pad(rpb, ((0, 0), (0, 1), (0, LANES - rpb.shape[2])))
    W = {
        'norm_mix_g': norm_mix_g, 'norm_ffn_g': norm_ffn_g, 'final_norm_g': final_norm_g,
        'w_in': w_in, 'w_in_tail': w_in_tail,
        'ssd_conv_w': ssd_conv_w[0], 'ssd_conv_b': ssd_conv_b[0].reshape(1, SSD_XBC), 'ssd_prm': prm,
        'ssd_norm_g': ssd_norm_g[0].reshape(1, SSD_INNER), 'sc_conv_w_rot': jnp.roll(sc_conv_w[0], MIX_SHIFT, axis=1),
        'mix0_w_out': w_out0,
        'na_w_qkv': na_w_qkv[0].astype(BF16), 'rpb_pad': rpb_pad, 'na_w_out': na_w_out[0].astype(BF16),
        'ffn': (ffn_w_gate.astype(BF16), ffn_w_up.astype(BF16), ffn_conv_w, ffn_w_down.astype(BF16)),
    }
    y_prompt, new_state, new_k, new_v = _run_stream(x_prompt, mod, 0, W, is_ctx=True)
    y_sample = _run_stream(x_sample, mod, 1, W, is_ctx=False, state_ssd=state_ssd, cache_k=cache_k, cache_v=cache_v)
    return (y_prompt, y_sample, new_state, new_k, new_v)
```

```python
import functools

import jax
import jax.numpy as jnp
from jax import lax
from jax.experimental import pallas as pl
from jax.experimental.pallas import tpu as pltpu

F32 = jnp.float32
BF16 = jnp.bfloat16

D_MODEL = 1024
EPS = 1e-6
GRID_W = 64
SSD_HEADDIM = 64
SSD_HEADS = 16
SSD_STATE = 128
SSD_GROUPS = 2
SSD_INNER = 1024
SSD_XBC = 1536
SSD_LC = 256
SC_WIDTH = 1024
NA_HEADS = 16
NA_HEAD_DIM = 64
NA_KH = 8
NA_KW = 16

LOG2E = 1.4426950408889634
LANES = 128
SUBLANES = 8
HEAD_PAIRS = 8
U_COLS = 5 * 1024 + 512

VMEM_LIMIT = 56 * 1024 * 1024

_RESIDENT = pl.BlockSpec(memory_space=pltpu.VMEM)


def _cparams(sem):
    return pltpu.CompilerParams(dimension_semantics=sem, vmem_limit_bytes=VMEM_LIMIT)


def _silu(x):
    return x * jax.nn.sigmoid(x)


def _dot(a, b):
    return jnp.dot(a, b, preferred_element_type=F32)


def _dot_nt(a, b):
    return lax.dot_general(a, b, (((1,), (1,)), ((), ())), preferred_element_type=F32)


def _split3(v):
    hi = v.astype(BF16)
    r1 = v - hi.astype(F32)
    mid = r1.astype(BF16)
    lo = (r1 - mid.astype(F32)).astype(BF16)
    return hi, mid, lo


def _sel_dot_left(sel, v):
    hi, mid, lo = _split3(v)
    return _dot(sel, hi) + _dot(sel, mid) + _dot(sel, lo)


def _norm_mod(x, g, shift, scale):
    ms = jnp.mean(x * x, axis=-1, keepdims=True)
    y = x * lax.rsqrt(ms + EPS) * g
    return y * (1.0 + scale) + shift


def _mod_row(i, tm, row_base, rows_per_group):
    return row_base + (i * tm) // rows_per_group


def _ada_kernel(c_ref, w_ref, b_ref, o_ref):
    s = _silu(c_ref[...]).astype(BF16)
    o_ref[0] = _dot(s, w_ref[0].astype(BF16)) + b_ref[0]


def _ada(cvec8, ada_w, ada_b):
    depth, d, n = ada_w.shape
    tn = 1536
    return pl.pallas_call(
        _ada_kernel,
        grid=(depth, n // tn),
        in_specs=[
            pl.BlockSpec((8, d), lambda l, j: (0, 0)),
            pl.BlockSpec((1, d, tn), lambda l, j: (l, 0, j)),
            pl.BlockSpec((1, 1, tn), lambda l, j: (l, 0, j)),
        ],
        out_specs=pl.BlockSpec((1, 8, tn), lambda l, j: (l, 0, j)),
        out_shape=jax.ShapeDtypeStruct((depth, 8, n), F32),
        compiler_params=_cparams(("arbitrary", "arbitrary")),
        name="ada_mod",
    )(cvec8, ada_w, ada_b.reshape(depth, 1, n))


def _nm_matmul_kernel(*refs, n_groups, group_width, has_extra, tm, tn, row_base, rows_per_group):
    x_ref, sh_ref, sc_ref, g_ref, w_ref = refs[:5]
    pos = 5
    if has_extra:
        we_ref = refs[pos]
        pos += 1
    o_refs = refs[pos:pos + n_groups]
    pos += n_groups
    if has_extra:
        oe_ref = refs[pos]
    r = _mod_row(pl.program_id(0), tm, row_base, rows_per_group)
    sh = sh_ref[0, pl.ds(r, 1), :]
    sc = sc_ref[0, pl.ds(r, 1), :]
    h = _norm_mod(x_ref[...], g_ref[...], sh, sc).astype(BF16)
    if has_extra:
        oe_ref[...] = _dot(h, we_ref[...])
    for gi, o_ref in enumerate(o_refs):
        for j in range(group_width // tn):
            res = _dot(h, w_ref[:, gi * group_width + j * tn:gi * group_width + (j + 1) * tn])
            o_ref[:, j * tn:(j + 1) * tn] = res.astype(o_ref.dtype)


def _nm_matmul(x, mod, layer, shift_idx, scale_idx, g, w, *, n_groups, group_width, row_base, rows_per_group,
               w_extra=None, out_dtype=F32, tm=512, tn=512):
    m, d = x.shape
    has_extra = w_extra is not None
    in_specs = [
        pl.BlockSpec((tm, d), lambda i: (i, 0)),
        pl.BlockSpec((1, 8, d), lambda i: (layer, 0, shift_idx)),
        pl.BlockSpec((1, 8, d), lambda i: (layer, 0, scale_idx)),
        pl.BlockSpec((1, d), lambda i: (0, 0)),
        _RESIDENT,
    ]
    args = [x, mod, mod, g.reshape(1, d), w]
    out_specs = [pl.BlockSpec((tm, group_width), lambda i: (i, 0)) for _ in range(n_groups)]
    out_shape = [jax.ShapeDtypeStruct((m, group_width), out_dtype) for _ in range(n_groups)]
    if has_extra:
        ne = w_extra.shape[1]
        in_specs.append(_RESIDENT)
        args.append(w_extra)
        out_specs.append(pl.BlockSpec((tm, ne), lambda i: (i, 0)))
        out_shape.append(jax.ShapeDtypeStruct((m, ne), F32))
    kern = functools.partial(_nm_matmul_kernel, n_groups=n_groups, group_width=group_width, has_extra=has_extra, tm=tm,
                             tn=tn, row_base=row_base, rows_per_group=rows_per_group)
    return pl.pallas_call(
        kern,
        grid=(m // tm,),
        in_specs=in_specs,
        out_specs=out_specs,
        out_shape=out_shape,
        compiler_params=_cparams(("arbitrary",)),
        name="norm_mod_matmul",
    )(*args)


def _conv_scratch(rows, cols, seq_len):
    return pltpu.VMEM((cols // LANES, rows + SUBLANES * (rows // seq_len + 3), LANES), F32)


def _dwconv3(v, w, seq_len, cv_ref, row0=0):
    rows, ch = v.shape
    dyn0 = pl.program_id(0) * 0
    cuts = [0] + [t for t in range(1, rows) if (row0 + t) % seq_len == 0] + [rows]
    segs = list(zip(cuts[:-1], cuts[1:]))
    zeros = jnp.zeros((SUBLANES, LANES), F32)
    slabs = []
    for j in range(ch // LANES):
        cl = slice(j * LANES, (j + 1) * LANES)
        base = SUBLANES
        cv_ref[j, 0:SUBLANES, :] = zeros
        bases = []
        for a, b in segs:
            cv_ref[j, base:base + (b - a), :] = v[a:b, cl]
            cv_ref[j, base + (b - a):base + (b - a) + SUBLANES, :] = zeros
            bases.append(base)
            base += (b - a) + SUBLANES
        parts = []
        for (a, b), bs in zip(segs, bases):
            prev = cv_ref[j, pl.ds(bs - 1 + dyn0, b - a), :]
            nxt = cv_ref[j, pl.ds(bs + 1 + dyn0, b - a), :]
            parts.append(prev * w[0:1, cl] + v[a:b, cl] * w[1:2, cl] + nxt * w[2:3, cl])
        slabs.append(jnp.concatenate(parts, axis=0) if len(parts) > 1 else parts[0])
    return jnp.concatenate(slabs, axis=1) if len(slabs) > 1 else slabs[0]


MIX_SHIFT = 2 * SSD_HEADS
DT_COL0 = SSD_INNER + SSD_XBC
TAIL_COL0 = DT_COL0 + 3 * SC_WIDTH
IN_PROJ_TN = 256


def _in_proj_kernel(x_ref, sh_ref, sc_ref, g_ref, w_ref, wt_ref, cw_ref, cb_ref, scw_ref, zs_ref, x2_ref, bc_ref, yb_ref,
                    dt_ref, cv_ref,
                    *, tm, row_base, rows_per_group, seq_len):
    r = _mod_row(pl.program_id(0), tm, row_base, rows_per_group)
    sh = sh_ref[0, pl.ds(r, 1), :]
    sc = sc_ref[0, pl.ds(r, 1), :]
    h = _norm_mod(x_ref[...], g_ref[...], sh, sc).astype(BF16)
    tn = IN_PROJ_TN

    def proj(c0, width=tn):
        return _dot(h, w_ref[:, c0:c0 + width].astype(BF16))

    for j in range(SSD_INNER // tn):
        cols = slice(j * tn, (j + 1) * tn)
        zs_ref[:, cols] = _silu(proj(j * tn)).astype(BF16)
        xc = _dwconv3(proj(SSD_INNER + j * tn), cw_ref[:, cols], seq_len, cv_ref) + cb_ref[:, cols]
        x2_ref[:, cols] = _silu(xc).astype(BF16)
    for j in range((SSD_XBC - SSD_INNER) // tn):
        cols = slice(SSD_INNER + j * tn, SSD_INNER + (j + 1) * tn)
        bcc = _dwconv3(proj(SSD_INNER + cols.start), cw_ref[:, cols], seq_len, cv_ref) + cb_ref[:, cols]
        bc_ref[:, j * tn:(j + 1) * tn] = _silu(bcc).astype(BF16)
    dt_ref[...] = proj(DT_COL0, LANES)
    q3 = _dot(h, wt_ref[...])
    tail = lax.broadcasted_iota(jnp.int32, (1, LANES), 1) < MIX_SHIFT
    for j in range(SC_WIDTH // tn):
        c0 = DT_COL0 + j * tn
        cols = slice(j * tn, (j + 1) * tn)
        p0, p1, p2 = proj(c0), proj(c0 + SC_WIDTH), proj(c0 + 2 * SC_WIDTH)
        y = p0 * _dwconv3(p1 * p2, scw_ref[:, cols], seq_len, cv_ref)
        if j == 0:
            yt = p1[:, :LANES] * _dwconv3(p2[:, :LANES] * q3, scw_ref[:, :LANES], seq_len, cv_ref)
            y = jnp.concatenate([jnp.where(tail, yt, y[:, :LANES]), y[:, LANES:]], axis=1)
        yb_ref[:, cols] = y.astype(BF16)


def _in_proj(x, mod, g, w, w_tail, conv_w, conv_b, sc_conv_w_rot, *, seq_len, row_base, rows_per_group, tm):
    m, d = x.shape
    kern = functools.partial(_in_proj_kernel, tm=tm, row_base=row_base, rows_per_group=rows_per_group, seq_len=seq_len)
    widths = (SSD_INNER, SSD_INNER, SSD_XBC - SSD_INNER, SC_WIDTH, LANES)
    dtypes = (BF16, BF16, BF16, BF16, F32)
    return pl.pallas_call(
        kern,
        grid=(m // tm,),
        in_specs=[
            pl.BlockSpec((tm, d), lambda i: (i, 0)),
            pl.BlockSpec((1, 8, d), lambda i: (0, 0, 0)),
            pl.BlockSpec((1, 8, d), lambda i: (0, 0, 1)),
            pl.BlockSpec((1, d), lambda i: (0, 0)),
            _RESIDENT,
            _RESIDENT,
            pl.BlockSpec((3, SSD_XBC), lambda i: (0, 0)),
            pl.BlockSpec((1, SSD_XBC), lambda i: (0, 0)),
            pl.BlockSpec((3, SC_WIDTH), lambda i: (0, 0)),
        ],
        out_specs=[pl.BlockSpec((tm, wd), lambda i: (i, 0)) for wd in widths],
        out_shape=[jax.ShapeDtypeStruct((m, wd), dt) for wd, dt in zip(widths, dtypes)],
        scratch_shapes=[_conv_scratch(tm, IN_PROJ_TN, seq_len)],
        compiler_params=_cparams(("arbitrary",)),
        name="in_proj",
    )(x, mod, mod, g.reshape(1, d), w, w_tail, conv_w, conv_b, sc_conv_w_rot)


def _ffn_kernel(*refs, n_parts, tm, tf, row_base, rows_per_group, seq_len, final_norm):
    x_ref, gate1_ref, sh_ref, sc_ref, gate_ref, g_ref, wo_ref = refs[:7]
    a_refs = refs[7:7 + n_parts]
    wg_ref, wu_ref, cw_ref, wd_ref, fg_ref, o_ref, h_ref, act_ref, cv_ref = refs[7 + n_parts:]
    r = _mod_row(pl.program_id(0), tm, row_base, rows_per_group)
    sh = sh_ref[0, pl.ds(r, 1), :]
    sc = sc_ref[0, pl.ds(r, 1), :]
    gate = gate_ref[0, pl.ds(r, 1), :]
    kp = wo_ref.shape[0] // n_parts
    sub = tm // 4
    for q in range(tm // sub):
        rq = slice(q * sub, (q + 1) * sub)
        mix = _dot(a_refs[0][rq, :], wo_ref[0:kp, :])
        for k in range(1, n_parts):
            mix = mix + _dot(a_refs[k][rq, :], wo_ref[k * kp:(k + 1) * kp, :])
        x1 = x_ref[rq, :] + gate1_ref[0, pl.ds(r, 1), :] * mix
        o_ref[rq, :] = x1
        h_ref[rq, :] = _norm_mod(x1, g_ref[...], sh, sc).astype(BF16)
    dff = wg_ref.shape[1]
    half = tm // 2
    win = half + SUBLANES
    for hb in range(2):
        w0 = hb * (half - SUBLANES)
        v0 = hb * SUBLANES
        rows = slice(hb * half, (hb + 1) * half)
        hw = h_ref[w0:w0 + win, :]
        hv = h_ref[rows, :]
        for fc in range(dff // tf):
            cols = slice(fc * tf, (fc + 1) * tf)
            gpre = _dwconv3(_dot(hw, wg_ref[:, cols]), cw_ref[:, cols], seq_len, cv_ref, row0=w0)[v0:v0 + half]
            up = _dot(hv, wu_ref[:, cols])
            act_ref[:, cols] = (jax.nn.gelu(gpre, approximate=True) * up).astype(BF16)
        y = o_ref[rows, :] + gate * _dot(act_ref[...], wd_ref[...])
        if final_norm:
            ms = jnp.mean(y * y, axis=-1, keepdims=True)
            y = y * lax.rsqrt(ms + EPS) * fg_ref[...]
        o_ref[rows, :] = y


def _layer_weight(shape):
    return lambda layer: pl.BlockSpec((None,) + shape, lambda i: (layer,) + (0,) * len(shape),
                                      pipeline_mode=pl.Buffered(1))


def _mix_ffn(x, mod, layer, parts, wo, g, wg, wu, cw, wd, final_g, *, row_base, rows_per_group, seq_len, final_norm,
             tm=1024, tf=256):
    m, d = x.shape
    dff = wg.shape[2]
    n_parts = len(parts)
    kern = functools.partial(_ffn_kernel, n_parts=n_parts, tm=tm, tf=tf, row_base=row_base,
                             rows_per_group=rows_per_group, seq_len=seq_len, final_norm=final_norm)
    mod_spec = lambda k: pl.BlockSpec((1, 8, d), lambda i: (layer, 0, k))
    in_specs = [pl.BlockSpec((tm, d), lambda i: (i, 0)), mod_spec(2), mod_spec(3), mod_spec(4), mod_spec(5),
                pl.BlockSpec((1, d), lambda i: (0, 0)), _RESIDENT]
    in_specs += [pl.BlockSpec((tm, a.shape[1]), lambda i: (i, 0)) for a in parts]
    in_specs += [_layer_weight((d, dff))(layer), _layer_weight((d, dff))(layer), _layer_weight((3, dff))(layer),
                 _layer_weight((dff, d))(layer), pl.BlockSpec((1, d), lambda i: (0, 0))]
    return pl.pallas_call(
        kern,
        grid=(m // tm,),
        in_specs=in_specs,
        out_specs=pl.BlockSpec((tm, d), lambda i: (i, 0)),
        out_shape=jax.ShapeDtypeStruct((m, d), F32),
        scratch_shapes=[pltpu.VMEM((tm, d), BF16), pltpu.VMEM((tm // 2, dff), BF16),
                        _conv_scratch(tm // 2 + SUBLANES, tf, seq_len)],
        compiler_params=_cparams(("arbitrary",)),
        name="mix_ffn",
    )(x, mod, mod, mod, mod, g.reshape(1, d), wo, *parts, wg, wu, cw, wd, final_g.reshape(1, d))


def _pair_cols(first_half, arr, h0, h1):
    return jnp.where(first_half, arr[:, h0:h0 + 1], arr[:, h1:h1 + 1])


def _pair_rows(arr_t, h0, h1, width):
    return jnp.concatenate([jnp.broadcast_to(arr_t[h0:h0 + 1, :], (SSD_HEADDIM, width)),
                            jnp.broadcast_to(arr_t[h1:h1 + 1, :], (SSD_HEADDIM, width))], axis=0)


def _pair_scalars(row, h0, h1):
    return jnp.concatenate([jnp.broadcast_to(row[:, h0:h0 + 1], (SSD_HEADDIM, SSD_STATE)),
                            jnp.broadcast_to(row[:, h1:h1 + 1], (SSD_HEADDIM, SSD_STATE))], axis=0)


def _ssd_kernel(*refs, n_elems, **kw):
    for e in range(n_elems):
        _ssd_element(e, *refs, **kw)


def _ssd_element(e, *refs, seq_len, has_h0, emit_state):
    zs_ref, x2_ref, bc_ref, dt_ref, prm_ref, ng_ref = refs[:6]
    pos = 6
    if has_h0:
        h0_ref = refs[pos]
        pos += 1
    ya_ref = refs[pos]
    pos += 1
    if emit_state:
        st_ref = refs[pos]
        pos += 1
    yacc_ref = refs[pos]

    lc = SSD_LC
    nchunk = seq_len // lc
    nh = SSD_HEADS
    lane = lax.broadcasted_iota(jnp.int32, (1, LANES), 1)
    first_half = lane < SSD_HEADDIM
    fwd_lane = lane < nh

    bc = bc_ref[e]
    valid = lane < 2 * nh
    dt = jnp.where(valid, jax.nn.softplus(dt_ref[e] + prm_ref[0:1, :]), 0.0)
    a = -jnp.exp(prm_ref[1:2, :])
    dta = jnp.where(valid, dt * a, 0.0)
    dsum = prm_ref[2:3, :] + prm_ref[3:4, :]
    tt = lax.broadcasted_iota(jnp.int32, (lc, lc), 0)
    ss = lax.broadcasted_iota(jnp.int32, (lc, lc), 1)
    tril = jnp.where(ss <= tt, 1.0, 0.0).astype(BF16)
    triu = jnp.where(ss >= tt, 1.0, 0.0).astype(BF16)
    nq = lc // LANES
    td = lax.broadcasted_iota(jnp.int32, (LANES, LANES), 0)
    sd = lax.broadcasted_iota(jnp.int32, (LANES, LANES), 1)
    lower_d = td > sd
    upper_d = td < sd

    acs2_l, rowt_l, wts_l, eacs_l, edec_l, cbm_l, bm_l, cm_l = [], [], [], [], [], [], [], []
    for c in range(nchunk):
        rows = slice(c * lc, (c + 1) * lc)
        acs = jnp.where(fwd_lane, _sel_dot_left(tril, dta[rows]), _sel_dot_left(triu, dta[rows]))
        edge = jnp.where(fwd_lane, acs[lc - 1:lc, :], acs[0:1, :])
        acs2 = acs * LOG2E
        dtt = dt[rows].T
        rowt = acs2.T - jnp.log2(dtt)
        dgt = jnp.log2(dtt[0:nh, :] + dtt[nh:2 * nh, :])
        acs2_l.append(acs2)
        rowt_l.append((rowt, dgt))
        wts_l.append((dt[rows] * jnp.exp(edge - acs)).T)
        eacs_l.append(jnp.exp(acs))
        edec_l.append(jnp.exp(edge))
        cbm_g, bm_g, cm_g = [], [], []
        for g in range(SSD_GROUPS):
            bm = bc[rows, g * SSD_STATE:(g + 1) * SSD_STATE]
            cm = bc[rows, (SSD_GROUPS + g) * SSD_STATE:(SSD_GROUPS + g + 1) * SSD_STATE]
            bm_g.append(bm)
            cm_g.append(cm)
            cbm_g.append(_dot_nt(cm, bm))
        cbm_l.append(cbm_g)
        bm_l.append(bm_g)
        cm_l.append(cm_g)

    for p in range(HEAD_PAIRS):
        cols = slice(p * LANES, (p + 1) * LANES)
        g = p // (HEAD_PAIRS // SSD_GROUPS)
        hd = (2 * p, 2 * p + 1)
        x2b = x2_ref[e, :, cols]
        x2 = x2b.astype(F32)
        dsum2 = _pair_cols(first_half, dsum, hd[0], hd[1])
        if has_h0:
            hf = h0_ref[e, 0, p]
            hb = h0_ref[e, 1, p]
        else:
            hf = jnp.zeros((LANES, SSD_STATE), F32)
            hb = jnp.zeros((LANES, SSD_STATE), F32)
        ys = []
        stbs = []
        for c in range(nchunk):
            rows = slice(c * lc, (c + 1) * lc)
            x2c = x2[rows]
            x2cb = x2b[rows]
            acs2 = acs2_l[c]
            rowt, dgt = rowt_l[c]
            cbm = cbm_l[c][g]
            outs = []
            for h in hd:
                blocks = []
                for bi in range(nq):
                    ri = slice(bi * LANES, (bi + 1) * LANES)
                    cf = acs2[ri, h:h + 1]
                    cbk = acs2[ri, nh + h:nh + h + 1]
                    row_blocks = []
                    for bj in range(nq):
                        cj = slice(bj * LANES, (bj + 1) * LANES)
                        rf = rowt[h:h + 1, cj]
                        rb = rowt[nh + h:nh + h + 1, cj]
                        if bi > bj:
                            arg = cf - rf
                        elif bi < bj:
                            arg = cbk - rb
                        else:
                            arg = jnp.where(lower_d, cf - rf, jnp.where(upper_d, cbk - rb, dgt[h:h + 1, cj]))
                        row_blocks.append((cbm[ri, cj] * jnp.exp2(arg)).astype(BF16))
                    blocks.append(jnp.concatenate(row_blocks, axis=1))
                gm = jnp.concatenate(blocks, axis=0)
                outs.append(_dot(gm, x2cb))
            y = jnp.where(first_half, outs[0], outs[1]) + x2c * dsum2
            xt = x2c.T
            xwf = (xt * _pair_rows(wts_l[c], hd[0], hd[1], lc)).astype(BF16)
            xwb = (xt * _pair_rows(wts_l[c], nh + hd[0], nh + hd[1], lc)).astype(BF16)
            stf = _dot(xwf, bm_l[c][g])
            stbs.append(_dot(xwb, bm_l[c][g]))
            if has_h0 or c > 0:
                y = y + _dot_nt(cm_l[c][g], hf.astype(BF16)) * _pair_cols(first_half, eacs_l[c], hd[0], hd[1])
            hf = hf * _pair_scalars(edec_l[c], hd[0], hd[1]) + stf
            ys.append(y)
        for c in reversed(range(nchunk)):
            if has_h0 or c < nchunk - 1:
                ys[c] = ys[c] + (_dot_nt(cm_l[c][g], hb.astype(BF16))
                                 * _pair_cols(first_half, eacs_l[c], nh + hd[0], nh + hd[1]))
            hb = hb * _pair_scalars(edec_l[c], nh + hd[0], nh + hd[1]) + stbs[c]
        zg = zs_ref[e, :, cols].astype(F32)
        for c in range(nchunk):
            rows = slice(c * lc, (c + 1) * lc)
            yacc_ref[rows, cols] = ys[c] * zg[rows]
        if emit_state:
            st_ref[e, 0, p] = hf
            st_ref[e, 1, p] = hb

    y = yacc_ref[...]
    ms = jnp.mean(y * y, axis=-1, keepdims=True)
    ya_ref[e] = (y * lax.rsqrt(ms + EPS) * ng_ref[...]).astype(BF16)


def _ssd(zs3, x23, bc3, dt3, prm, norm_g, h0, *, emit_state, n_elems):
    b, s, _ = zs3.shape
    has_h0 = h0 is not None
    ne = n_elems
    in_specs = [
        pl.BlockSpec((ne, s, SSD_INNER), lambda i: (i, 0, 0)),
        pl.BlockSpec((ne, s, SSD_INNER), lambda i: (i, 0, 0)),
        pl.BlockSpec((ne, s, SSD_XBC - SSD_INNER), lambda i: (i, 0, 0)),
        pl.BlockSpec((ne, s, LANES), lambda i: (i, 0, 0)),
        pl.BlockSpec((8, LANES), lambda i: (0, 0)),
        pl.BlockSpec((1, SSD_INNER), lambda i: (0, 0)),
    ]
    args = [zs3, x23, bc3, dt3, prm, norm_g]
    state_block = (ne, 2, HEAD_PAIRS, LANES, SSD_STATE)
    if has_h0:
        in_specs.append(pl.BlockSpec(state_block, lambda i: (i, 0, 0, 0, 0)))
        args.append(h0)
    out_specs = [pl.BlockSpec((ne, s, SSD_INNER), lambda i: (i, 0, 0))]
    out_shape = [jax.ShapeDtypeStruct((b, s, SSD_INNER), BF16)]
    if emit_state:
        out_specs.append(pl.BlockSpec(state_block, lambda i: (i, 0, 0, 0, 0)))
        out_shape.append(jax.ShapeDtypeStruct((b, 2, HEAD_PAIRS, LANES, SSD_STATE), F32))
    kern = functools.partial(_ssd_kernel, n_elems=ne, seq_len=s, has_h0=has_h0, emit_state=emit_state)
    return pl.pallas_call(
        kern,
        grid=(b // ne,),
        in_specs=in_specs,
        out_specs=out_specs,
        out_shape=out_shape,
        scratch_shapes=[pltpu.VMEM((s, SSD_INNER), F32)],
        compiler_params=_cparams(("arbitrary",)),
        name="ssd",
    )(*args)


def _pair_masks():
    lane = lax.broadcasted_iota(jnp.int32, (1, LANES), 1)
    return lane < NA_HEAD_DIM


def _qkv_ctx_attn_kernel(x_ref, sh_ref, sc_ref, g_ref, w_ref, o_ref, nk_ref, nv_ref, qkv_ref,
                         *, tm, seq_len, row_base, rows_per_group):
    r = _mod_row(pl.program_id(0), tm, row_base, rows_per_group)
    sh = sh_ref[0, pl.ds(r, 1), :]
    sc = sc_ref[0, pl.ds(r, 1), :]
    h = _norm_mod(x_ref[...], g_ref[...], sh, sc).astype(BF16)
    d = x_ref.shape[1]
    tn = 512
    s = seq_len
    for part, out_ref in ((1, nk_ref), (2, nv_ref)):
        for j in range(d // tn):
            c0 = part * d + j * tn
            qkv_ref[:, c0:c0 + tn] = _dot(h, w_ref[:, c0:c0 + tn])
        for b in range(tm // s):
            rows = slice(b * s, (b + 1) * s)
            out_ref[b, 0] = qkv_ref[rows, part * d:(part + 1) * d].reshape(s, NA_HEADS, NA_HEAD_DIM)
    for j in range(d // tn):
        qkv_ref[:, j * tn:(j + 1) * tn] = _dot(h, w_ref[:, j * tn:(j + 1) * tn])
    first_half = _pair_masks()
    scale = NA_HEAD_DIM ** -0.5
    for b in range(tm // s):
        rows = slice(b * s, (b + 1) * s)
        for p in range(HEAD_PAIRS):
            cols = slice(p * LANES, (p + 1) * LANES)
            q2 = qkv_ref[rows, cols] * scale
            kb = qkv_ref[rows, d + p * LANES:d + (p + 1) * LANES].astype(BF16)
            vb = qkv_ref[rows, 2 * d + p * LANES:2 * d + (p + 1) * LANES].astype(BF16)
            qs = jnp.concatenate([jnp.where(first_half, q2, 0.0), jnp.where(first_half, 0.0, q2)], axis=0).astype(BF16)
            sco = _dot_nt(qs, kb)
            e = jnp.exp(sco - jnp.max(sco, axis=-1, keepdims=True))
            pv = _dot(e.astype(BF16), vb) / jnp.sum(e, axis=-1, keepdims=True)
            o_ref[rows, cols] = jnp.where(first_half, pv[:s], pv[s:]).astype(BF16)


def _qkv_ctx_attn(x, mod, layer, g, w, *, seq_len, row_base, rows_per_group, tm=512):
    m, d = x.shape
    nb = tm // seq_len
    cache_block = (nb, 1, seq_len, NA_HEADS, NA_HEAD_DIM)
    cache_shape = jax.ShapeDtypeStruct((m // seq_len, 1, seq_len, NA_HEADS, NA_HEAD_DIM), F32)
    kern = functools.partial(_qkv_ctx_attn_kernel, tm=tm, seq_len=seq_len, row_base=row_base,
                             rows_per_group=rows_per_group)
    return pl.pallas_call(
        kern,
        grid=(m // tm,),
        in_specs=[
            pl.BlockSpec((tm, d), lambda i: (i, 0)),
            pl.BlockSpec((1, 8, d), lambda i: (layer, 0, 0)),
            pl.BlockSpec((1, 8, d), lambda i: (layer, 0, 1)),
            pl.BlockSpec((1, d), lambda i: (0, 0)),
            _RESIDENT,
        ],
        out_specs=[
            pl.BlockSpec((tm, d), lambda i: (i, 0)),
            pl.BlockSpec(cache_block, lambda i: (i, 0, 0, 0, 0)),
            pl.BlockSpec(cache_block, lambda i: (i, 0, 0, 0, 0)),
        ],
        out_shape=[jax.ShapeDtypeStruct((m, d), BF16), cache_shape, cache_shape],
        scratch_shapes=[pltpu.VMEM((tm, 3 * d), F32)],
        compiler_params=_cparams(("arbitrary",)),
        name="qkv_ctx_attention",
    )(x, mod, mod, g.reshape(1, d), w)


def _lat_attn_kernel(q_ref, k_ref, v_ref, kc_ref, vc_ref, rpb_ref, o_ref, bias_ref, *, rows):
    b = pl.program_id(1)
    first_half = _pair_masks()
    kh = min(NA_KH, rows)
    win = kh * GRID_W
    neg_inf = -jnp.inf

    def win_start(r):
        return min(max(r - kh // 2, 0), rows - kh)

    @pl.when(b == 0)
    def _():
        qc = lax.broadcasted_iota(jnp.int32, (GRID_W, LANES), 0)
        kc = lax.broadcasted_iota(jnp.int32, (GRID_W, LANES), 1)
        cs = jnp.clip(qc - NA_KW // 2, 0, GRID_W - NA_KW)
        col_ok = (kc >= cs) & (kc < cs + NA_KW)
        for hh in range(2):
            for dr in range(2 * NA_KH - 1):
                v = jnp.broadcast_to(rpb_ref[hh, dr:dr + 1, :], (GRID_W, LANES))
                t = pltpu.roll(v, LANES - (NA_KW - 1), 1, stride=1, stride_axis=0)
                tile = jnp.where(col_ok, t, neg_inf)[:, :GRID_W]
                for r in range(rows):
                    i = dr - (NA_KH - 1) + r - win_start(r)
                    if 0 <= i < kh:
                        bias_ref[r, hh * GRID_W:(hh + 1) * GRID_W, i * GRID_W:(i + 1) * GRID_W] = tile

    scale = NA_HEAD_DIM ** -0.5
    kb = k_ref[...].astype(BF16)
    vb = v_ref[...].astype(BF16)
    kcb = kc_ref[...].astype(BF16)
    vcb = vc_ref[...].astype(BF16)
    group = 4
    for r0 in range(0, rows, group):
        rr = range(r0, r0 + group)
        qrows = [slice(r * GRID_W, (r + 1) * GRID_W) for r in rr]
        krows = [slice(win_start(r) * GRID_W, win_start(r) * GRID_W + win) for r in rr]
        qs = []
        for qr in qrows:
            q2 = q_ref[qr, :] * scale
            qs.append(jnp.concatenate([jnp.where(first_half, q2, 0.0), jnp.where(first_half, 0.0, q2)],
                                      axis=0).astype(BF16))
        s_loc = [_dot_nt(qi, kb[kr]) + bias_ref[r] for qi, kr, r in zip(qs, krows, rr)]
        s_ctx = [_dot_nt(qi, kcb) for qi in qs]
        mx = [jnp.maximum(jnp.max(sl, axis=-1, keepdims=True), jnp.max(sc, axis=-1, keepdims=True))
              for sl, sc in zip(s_loc, s_ctx)]
        e_loc = [jnp.exp(sl - m) for sl, m in zip(s_loc, mx)]
        e_ctx = [jnp.exp(sc - m) for sc, m in zip(s_ctx, mx)]
        den = [jnp.sum(el, axis=-1, keepdims=True) + jnp.sum(ec, axis=-1, keepdims=True) for el, ec in zip(e_loc, e_ctx)]
        pv = [(_dot(el.astype(BF16), vb[kr]) + _dot(ec.astype(BF16), vcb)) / dn
              for el, ec, kr, dn in zip(e_loc, e_ctx, krows, den)]
        for qr, o in zip(qrows, pv):
            o_ref[qr, :] = jnp.where(first_half, o[:GRID_W], o[GRID_W:]).astype(BF16)


def _lat_attn(q, k, v, kc, vc, rpb_pad, *, n_batch, n_tok, n_ctx):
    d = q.shape[1]
    rows = n_tok // GRID_W
    kh = min(NA_KH, rows)
    tok_spec = pl.BlockSpec((n_tok, LANES), lambda p, b: (b, p))
    ctx_spec = pl.BlockSpec((n_ctx, LANES), lambda p, b: (b, p))
    kern = functools.partial(_lat_attn_kernel, rows=rows)
    return pl.pallas_call(
        kern,
        grid=(HEAD_PAIRS, n_batch),
        in_specs=[tok_spec, tok_spec, tok_spec, ctx_spec, ctx_spec,
                  pl.BlockSpec((2, 2 * NA_KH, LANES), lambda p, b: (p, 0, 0))],
        out_specs=tok_spec,
        out_shape=jax.ShapeDtypeStruct((n_batch * n_tok, d), BF16),
        scratch_shapes=[pltpu.VMEM((rows, 2 * GRID_W, kh * GRID_W), F32)],
        compiler_params=_cparams(("arbitrary", "arbitrary")),
        name="latent_attention",
    )(q, k, v, kc, vc, rpb_pad)


def _pad_lanes(v):
    return jnp.pad(v, (0, LANES - v.shape[0]))


def _run_stream(x3, mod, row_base, W, *, is_ctx, state_ssd=None, cache_k=None, cache_v=None):
    b, s, d = x3.shape
    m = b * s
    x = x3.reshape(m, d)
    rpg = m if is_ctx else s
    kw = dict(row_base=row_base, rows_per_group=rpg)

    zs, x2, bca, yb, dtr = _in_proj(x, mod, W['norm_mix_g'][0], W['w_in'], W['w_in_tail'], W['ssd_conv_w'], W['ssd_conv_b'],
                                    W['sc_conv_w_rot'], seq_len=s, tm=1024, **kw)
    h0 = None
    if not is_ctx:
        h0 = state_ssd[:, 0].reshape(b, 2, HEAD_PAIRS, LANES, SSD_STATE)
    res = _ssd(zs.reshape(b, s, -1), x2.reshape(b, s, -1), bca.reshape(b, s, -1), dtr.reshape(b, s, LANES),
               W['ssd_prm'], W['ssd_norm_g'], h0, emit_state=is_ctx, n_elems=2 if is_ctx else 1)
    ya = res[0].reshape(m, SSD_INNER)
    x = _mix_ffn(x, mod, 0, [ya, yb], W['mix0_w_out'], W['norm_ffn_g'][0], *W['ffn'], W['final_norm_g'], seq_len=s,
                 final_norm=False, **kw)

    if is_ctx:
        o, new_k, new_v = _qkv_ctx_attn(x, mod, 1, W['norm_mix_g'][1], W['na_w_qkv'], seq_len=s, **kw)
    else:
        q, k, v = _nm_matmul(x, mod, 1, 0, 1, W['norm_mix_g'][1], W['na_w_qkv'], n_groups=3, group_width=D_MODEL, **kw)
        n_ctx = cache_k.shape[2]
        kc = cache_k[:, 0].reshape(b * n_ctx, d)
        vc = cache_v[:, 0].reshape(b * n_ctx, d)
        o = _lat_attn(q, k, v, kc, vc, W['rpb_pad'], n_batch=b, n_tok=s, n_ctx=n_ctx)
    x = _mix_ffn(x, mod, 1, [o], W['na_w_out'], W['norm_ffn_g'][1], *W['ffn'], W['final_norm_g'], seq_len=s,
                 final_norm=True, **kw)
    y = x.reshape(b, s, d)
    if is_ctx:
        new_state = res[1].reshape(b, 1, 2, SSD_HEADS, SSD_HEADDIM, SSD_STATE)
        return y, new_state, new_k, new_v
    return y


def kernel(x_prompt, x_sample, state_ssd, cache_k, cache_v, c, c_ctx, ada_w, ada_b, norm_mix_g, norm_ffn_g, ssd_w_in,
           ssd_conv_w, ssd_conv_b, ssd_dt_bias, ssd_a_log, ssd_d, ssd_norm_g, sc_conv_w, mix0_w_out, na_w_qkv, na_rpb,
           na_w_out, ffn_w_gate, ffn_w_up, ffn_conv_w, ffn_w_down, final_norm_g):
    n_lat = x_sample.shape[0]
    cvec = jnp.concatenate([c_ctx[None, :], c, jnp.zeros((8 - 1 - n_lat, D_MODEL), F32)], axis=0)
    mod = _ada(cvec, ada_w, ada_b)

    w_in = ssd_w_in[0]
    w_in_tail = jnp.pad(ssd_w_in[0][:, TAIL_COL0:], ((0, 0), (0, LANES - MIX_SHIFT))).astype(BF16)
    w_out0 = mix0_w_out[0]
    w_out0 = jnp.concatenate([w_out0[:SSD_INNER], jnp.roll(w_out0[SSD_INNER:], MIX_SHIFT, axis=0)], axis=0).astype(BF16)
    prm =jnp.stack([_pad_lanes(ssd_dt_bias[0].reshape(-1)), _pad_lanes(ssd_a_log[0].reshape(-1)),
                     _pad_lanes(ssd_d[0, 0]), _pad_lanes(ssd_d[0, 1])] + [jnp.zeros((LANES,), F32)] * 4, axis=0)
    rpb = na_rpb[0]
    rpb_pad = jnp.pad(rpb, ((0, 0), (0, 1), (0, LANES - rpb.shape[2])))
    W = {
        'norm_mix_g': norm_mix_g, 'norm_ffn_g': norm_ffn_g, 'final_norm_g': final_norm_g,
        'w_in': w_in, 'w_in_tail': w_in_tail,
        'ssd_conv_w': ssd_conv_w[0], 'ssd_conv_b': ssd_conv_b[0].reshape(1, SSD_XBC), 'ssd_prm': prm,
        'ssd_norm_g': ssd_norm_g[0].reshape(1, SSD_INNER), 'sc_conv_w_rot': jnp.roll(sc_conv_w[0], MIX_SHIFT, axis=1),
        'mix0_w_out': w_out0,
        'na_w_qkv': na_w_qkv[0].astype(BF16), 'rpb_pad': rpb_pad, 'na_w_out': na_w_out[0].astype(BF16),
        'ffn': (ffn_w_gate.astype(BF16), ffn_w_up.astype(BF16), ffn_conv_w, ffn_w_down.astype(BF16)),
    }
    y_prompt, new_state, new_k, new_v = _run_stream(x_prompt, mod, 0, W, is_ctx=True)
    y_sample = _run_stream(x_sample, mod, 1, W, is_ctx=False, state_ssd=state_ssd, cache_k=cache_k, cache_v=cache_v)
    return (y_prompt, y_sample, new_state, new_k, new_v)
```

```python
import functools

import jax
import jax.numpy as jnp
from jax import lax
from jax.experimental import pallas as pl
from jax.experimental.pallas import tpu as pltpu

F32 = jnp.float32
BF16 = jnp.bfloat16

D_MODEL = 1024
EPS = 1e-6
GRID_W = 64
SSD_HEADDIM = 64
SSD_HEADS = 16
SSD_STATE = 128
SSD_GROUPS = 2
SSD_INNER = 1024
SSD_XBC = 1536
SSD_LC = 256
SC_WIDTH = 1024
NA_HEADS = 16
NA_HEAD_DIM = 64
NA_KH = 8
NA_KW = 16

LOG2E = 1.4426950408889634
LANES = 128
SUBLANES = 8
HEAD_PAIRS = 8

VMEM_LIMIT = 56 * 1024 * 1024

_RESIDENT = pl.BlockSpec(memory_space=pltpu.VMEM)


def _cparams(sem):
    return pltpu.CompilerParams(dimension_semantics=sem, vmem_limit_bytes=VMEM_LIMIT)


def _silu(x):
    return x * jax.nn.sigmoid(x)


def _dot(a, b):
    return jnp.dot(a, b, preferred_element_type=F32)


def _dot_nt(a, b):
    return lax.dot_general(a, b, (((1,), (1,)), ((), ())), preferred_element_type=F32)


def _split3(v):
    hi = v.astype(BF16)
    r1 = v - hi.astype(F32)
    mid = r1.astype(BF16)
    lo = (r1 - mid.astype(F32)).astype(BF16)
    return hi, mid, lo


def _sel_dot_left(sel, v):
    hi, mid, lo = _split3(v)
    return _dot(sel, hi) + _dot(sel, mid) + _dot(sel, lo)


def _norm_mod(x, g, shift, scale):
    ms = jnp.mean(x * x, axis=-1, keepdims=True)
    y = x * lax.rsqrt(ms + EPS) * g
    return y * (1.0 + scale) + shift


def _mod_row(i, tm, row_base, rows_per_group):
    return row_base + (i * tm) // rows_per_group


def _ada_kernel(c_ref, w_ref, b_ref, o_ref):
    s = _silu(c_ref[...]).astype(BF16)
    o_ref[0] = _dot(s, w_ref[0].astype(BF16)) + b_ref[0]


def _ada(cvec8, ada_w, ada_b):
    depth, d, n = ada_w.shape
    tn = 1536
    return pl.pallas_call(
        _ada_kernel,
        grid=(depth, n // tn),
        in_specs=[
            pl.BlockSpec((8, d), lambda l, j: (0, 0)),
            pl.BlockSpec((1, d, tn), lambda l, j: (l, 0, j)),
            pl.BlockSpec((1, 1, tn), lambda l, j: (l, 0, j)),
        ],
        out_specs=pl.BlockSpec((1, 8, tn), lambda l, j: (l, 0, j)),
        out_shape=jax.ShapeDtypeStruct((depth, 8, n), F32),
        compiler_params=_cparams(("arbitrary", "arbitrary")),
        name="ada_mod",
    )(cvec8, ada_w, ada_b.reshape(depth, 1, n))


def _nm_matmul_kernel(*refs, n_groups, group_width, tm, tn, row_base, rows_per_group):
    x_ref, sh_ref, sc_ref, g_ref, w_ref = refs[:5]
    o_refs = refs[5:5 + n_groups]
    r = _mod_row(pl.program_id(0), tm, row_base, rows_per_group)
    sh = sh_ref[0, pl.ds(r, 1), :]
    sc = sc_ref[0, pl.ds(r, 1), :]
    h = _norm_mod(x_ref[...], g_ref[...], sh, sc).astype(BF16)
    for gi, o_ref in enumerate(o_refs):
        for j in range(group_width // tn):
            o_ref[:, j * tn:(j + 1) * tn] = _dot(h, w_ref[:, gi * group_width + j * tn:gi * group_width + (j + 1) * tn])


def _nm_matmul(x, mod, layer, shift_idx, scale_idx, g, w, *, n_groups, group_width, row_base, rows_per_group,
               tm=512, tn=512):
    m, d = x.shape
    kern = functools.partial(_nm_matmul_kernel, n_groups=n_groups, group_width=group_width, tm=tm, tn=tn,
                             row_base=row_base, rows_per_group=rows_per_group)
    return pl.pallas_call(
        kern,
        grid=(m // tm,),
        in_specs=[
            pl.BlockSpec((tm, d), lambda i: (i, 0)),
            pl.BlockSpec((1, 8, d), lambda i: (layer, 0, shift_idx)),
            pl.BlockSpec((1, 8, d), lambda i: (layer, 0, scale_idx)),
            pl.BlockSpec((1, d), lambda i: (0, 0)),
            _RESIDENT,
        ],
        out_specs=[pl.BlockSpec((tm, group_width), lambda i: (i, 0)) for _ in range(n_groups)],
        out_shape=[jax.ShapeDtypeStruct((m, group_width), F32) for _ in range(n_groups)],
        compiler_params=_cparams(("arbitrary",)),
        name="norm_mod_matmul",
    )(x, mod, mod, g.reshape(1, d), w)


def _conv_scratch(rows, cols, seq_len):
    return pltpu.VMEM((cols // LANES, rows + SUBLANES * (rows // seq_len + 3), LANES), F32)


def _dwconv3(v, w, seq_len, cv_ref, row0=0):
    rows, ch = v.shape
    dyn0 = pl.program_id(0) * 0
    cuts = [0] + [t for t in range(1, rows) if (row0 + t) % seq_len == 0] + [rows]
    segs = list(zip(cuts[:-1], cuts[1:]))
    zeros = jnp.zeros((SUBLANES, LANES), F32)
    slabs = []
    for j in range(ch // LANES):
        cl = slice(j * LANES, (j + 1) * LANES)
        base = SUBLANES
        cv_ref[j, 0:SUBLANES, :] = zeros
        bases = []
        for a, b in segs:
            cv_ref[j, base:base + (b - a), :] = v[a:b, cl]
            cv_ref[j, base + (b - a):base + (b - a) + SUBLANES, :] = zeros
            bases.append(base)
            base += (b - a) + SUBLANES
        parts = []
        for (a, b), bs in zip(segs, bases):
            prev = cv_ref[j, pl.ds(bs - 1 + dyn0, b - a), :]
            nxt = cv_ref[j, pl.ds(bs + 1 + dyn0, b - a), :]
            parts.append(prev * w[0:1, cl] + v[a:b, cl] * w[1:2, cl] + nxt * w[2:3, cl])
        slabs.append(jnp.concatenate(parts, axis=0) if len(parts) > 1 else parts[0])
    return jnp.concatenate(slabs, axis=1) if len(slabs) > 1 else slabs[0]


MIX_SHIFT = 2 * SSD_HEADS
DT_COL0 = SSD_INNER + SSD_XBC
TAIL_COL0 = DT_COL0 + 3 * SC_WIDTH
IN_PROJ_TN = 256


def _in_proj_kernel(x_ref, sh_ref, sc_ref, g_ref, w_ref, wt_ref, cw_ref, cb_ref, scw_ref, zs_ref, x2_ref, bc_ref, yb_ref,
                    dt_ref, cv_ref,
                    *, tm, row_base, rows_per_group, seq_len):
    r = _mod_row(pl.program_id(0), tm, row_base, rows_per_group)
    sh = sh_ref[0, pl.ds(r, 1), :]
    sc = sc_ref[0, pl.ds(r, 1), :]
    h = _norm_mod(x_ref[...], g_ref[...], sh, sc).astype(BF16)
    tn = IN_PROJ_TN

    def proj(c0, width=tn):
        return _dot(h, w_ref[:, c0:c0 + width])

    for j in range(SSD_INNER // tn):
        cols = slice(j * tn, (j + 1) * tn)
        zs_ref[:, cols] = _silu(proj(j * tn)).astype(BF16)
        xc = _dwconv3(proj(SSD_INNER + j * tn), cw_ref[:, cols], seq_len, cv_ref) + cb_ref[:, cols]
        x2_ref[:, cols] = _silu(xc).astype(BF16)
    for j in range((SSD_XBC - SSD_INNER) // tn):
        cols = slice(SSD_INNER + j * tn, SSD_INNER + (j + 1) * tn)
        bcc = _dwconv3(proj(SSD_INNER + cols.start), cw_ref[:, cols], seq_len, cv_ref) + cb_ref[:, cols]
        bc_ref[:, j * tn:(j + 1) * tn] = _silu(bcc).astype(BF16)
    dt_ref[...] = proj(DT_COL0, LANES)
    q3 = _dot(h, wt_ref[...])
    tail = lax.broadcasted_iota(jnp.int32, (1, LANES), 1) < MIX_SHIFT
    for j in range(SC_WIDTH // tn):
        c0 = DT_COL0 + j * tn
        cols = slice(j * tn, (j + 1) * tn)
        p0, p1, p2 = proj(c0), proj(c0 + SC_WIDTH), proj(c0 + 2 * SC_WIDTH)
        y = p0 * _dwconv3(p1 * p2, scw_ref[:, cols], seq_len, cv_ref)
        if j == 0:
            yt = p1[:, :LANES] * _dwconv3(p2[:, :LANES] * q3, scw_ref[:, :LANES], seq_len, cv_ref)
            y = jnp.concatenate([jnp.where(tail, yt, y[:, :LANES]), y[:, LANES:]], axis=1)
        yb_ref[:, cols] = y.astype(BF16)


def _in_proj(x, mod, g, w, w_tail, conv_w, conv_b, sc_conv_w_rot, *, seq_len, row_base, rows_per_group, tm):
    m, d = x.shape
    kern = functools.partial(_in_proj_kernel, tm=tm, row_base=row_base, rows_per_group=rows_per_group, seq_len=seq_len)
    widths = (SSD_INNER, SSD_INNER, SSD_XBC - SSD_INNER, SC_WIDTH, LANES)
    dtypes = (BF16, BF16, BF16, BF16, F32)
    return pl.pallas_call(
        kern,
        grid=(m // tm,),
        in_specs=[
            pl.BlockSpec((tm, d), lambda i: (i, 0)),
            pl.BlockSpec((1, 8, d), lambda i: (0, 0, 0)),
            pl.BlockSpec((1, 8, d), lambda i: (0, 0, 1)),
            pl.BlockSpec((1, d), lambda i: (0, 0)),
            _RESIDENT,
            _RESIDENT,
            pl.BlockSpec((3, SSD_XBC), lambda i: (0, 0)),
            pl.BlockSpec((1, SSD_XBC), lambda i: (0, 0)),
            pl.BlockSpec((3, SC_WIDTH), lambda i: (0, 0)),
        ],
        out_specs=[pl.BlockSpec((tm, wd), lambda i: (i, 0)) for wd in widths],
        out_shape=[jax.ShapeDtypeStruct((m, wd), dt) for wd, dt in zip(widths, dtypes)],
        scratch_shapes=[_conv_scratch(tm, IN_PROJ_TN, seq_len)],
        compiler_params=_cparams(("arbitrary",)),
        name="in_proj",
    )(x, mod, mod, g.reshape(1, d), w, w_tail, conv_w, conv_b, sc_conv_w_rot)


def _ffn_kernel(*refs, n_parts, tm, tf, row_base, rows_per_group, seq_len, final_norm):
    x_ref, gate1_ref, sh_ref, sc_ref, gate_ref, g_ref, wo_ref = refs[:7]
    a_refs = refs[7:7 + n_parts]
    wg_ref, wu_ref, cw_ref, wd_ref, fg_ref, o_ref, h_ref, act_ref, cv_ref = refs[7 + n_parts:]
    r = _mod_row(pl.program_id(0), tm, row_base, rows_per_group)
    sh = sh_ref[0, pl.ds(r, 1), :]
    sc = sc_ref[0, pl.ds(r, 1), :]
    gate = gate_ref[0, pl.ds(r, 1), :]
    kp = wo_ref.shape[0] // n_parts
    sub = tm // 4
    for q in range(tm // sub):
        rq = slice(q * sub, (q + 1) * sub)
        mix = _dot(a_refs[0][rq, :], wo_ref[0:kp, :])
        for k in range(1, n_parts):
            mix = mix + _dot(a_refs[k][rq, :], wo_ref[k * kp:(k + 1) * kp, :])
        x1 = x_ref[rq, :] + gate1_ref[0, pl.ds(r, 1), :] * mix
        o_ref[rq, :] = x1
        h_ref[rq, :] = _norm_mod(x1, g_ref[...], sh, sc).astype(BF16)
    dff = wg_ref.shape[1]
    half = tm // 2
    win = half + SUBLANES
    for hb in range(2):
        w0 = hb * (half - SUBLANES)
        v0 = hb * SUBLANES
        rows = slice(hb * half, (hb + 1) * half)
        hw = h_ref[w0:w0 + win, :]
        hv = h_ref[rows, :]
        for fc in range(dff // tf):
            cols = slice(fc * tf, (fc + 1) * tf)
            gpre = _dwconv3(_dot(hw, wg_ref[:, cols]), cw_ref[:, cols], seq_len, cv_ref, row0=w0)[v0:v0 + half]
            up = _dot(hv, wu_ref[:, cols])
            act_ref[:, cols] = (jax.nn.gelu(gpre, approximate=True) * up).astype(BF16)
        y = o_ref[rows, :] + gate * _dot(act_ref[...], wd_ref[...])
        if final_norm:
            ms = jnp.mean(y * y, axis=-1, keepdims=True)
            y = y * lax.rsqrt(ms + EPS) * fg_ref[...]
        o_ref[rows, :] = y


def _layer_weight(shape):
    return lambda layer: pl.BlockSpec((None,) + shape, lambda i: (layer,) + (0,) * len(shape),
                                      pipeline_mode=pl.Buffered(1))


def _mix_ffn(x, mod, layer, parts, wo, g, wg, wu, cw, wd, final_g, *, row_base, rows_per_group, seq_len, final_norm,
             tm=1024, tf=256):
    m, d = x.shape
    dff = wg.shape[2]
    n_parts = len(parts)
    kern = functools.partial(_ffn_kernel, n_parts=n_parts, tm=tm, tf=tf, row_base=row_base,
                             rows_per_group=rows_per_group, seq_len=seq_len, final_norm=final_norm)
    mod_spec = lambda k: pl.BlockSpec((1, 8, d), lambda i: (layer, 0, k))
    in_specs = [pl.BlockSpec((tm, d), lambda i: (i, 0)), mod_spec(2), mod_spec(3), mod_spec(4), mod_spec(5),
                pl.BlockSpec((1, d), lambda i: (0, 0)), _RESIDENT]
    in_specs += [pl.BlockSpec((tm, a.shape[1]), lambda i: (i, 0)) for a in parts]
    in_specs += [_layer_weight((d, dff))(layer), _layer_weight((d, dff))(layer), _layer_weight((3, dff))(layer),
                 _layer_weight((dff, d))(layer), pl.BlockSpec((1, d), lambda i: (0, 0))]
    return pl.pallas_call(
        kern,
        grid=(m // tm,),
        in_specs=in_specs,
        out_specs=pl.BlockSpec((tm, d), lambda i: (i, 0)),
        out_shape=jax.ShapeDtypeStruct((m, d), F32),
        scratch_shapes=[pltpu.VMEM((tm, d), BF16), pltpu.VMEM((tm // 2, dff), BF16),
                        _conv_scratch(tm // 2 + SUBLANES, tf, seq_len)],
        compiler_params=_cparams(("arbitrary",)),
        name="mix_ffn",
    )(x, mod, mod, mod, mod, g.reshape(1, d), wo, *parts, wg, wu, cw, wd, final_g.reshape(1, d))


def _pair_cols(first_half, arr, h0, h1):
    return jnp.where(first_half, arr[:, h0:h0 + 1], arr[:, h1:h1 + 1])


def _pair_rows(arr_t, h0, h1, width):
    return jnp.concatenate([jnp.broadcast_to(arr_t[h0:h0 + 1, :], (SSD_HEADDIM, width)),
                            jnp.broadcast_to(arr_t[h1:h1 + 1, :], (SSD_HEADDIM, width))], axis=0)


def _pair_scalars(row, h0, h1):
    return jnp.concatenate([jnp.broadcast_to(row[:, h0:h0 + 1], (SSD_HEADDIM, SSD_STATE)),
                            jnp.broadcast_to(row[:, h1:h1 + 1], (SSD_HEADDIM, SSD_STATE))], axis=0)


def _ssd_kernel(*refs, n_elems, **kw):
    for e in range(n_elems):
        _ssd_element(e, *refs, **kw)


def _ssd_element(e, *refs, seq_len, has_h0, emit_state):
    zs_ref, x2_ref, bc_ref, dt_ref, prm_ref, ng_ref = refs[:6]
    pos = 6
    if has_h0:
        h0_ref = refs[pos]
        pos += 1
    ya_ref = refs[pos]
    pos += 1
    if emit_state:
        st_ref = refs[pos]
        pos += 1
    yacc_ref = refs[pos]

    lc = SSD_LC
    nchunk = seq_len // lc
    nh = SSD_HEADS
    lane = lax.broadcasted_iota(jnp.int32, (1, LANES), 1)
    first_half = lane < SSD_HEADDIM
    fwd_lane = lane < nh

    bc = bc_ref[e]
    valid = lane < 2 * nh
    dt = jnp.where(valid, jax.nn.softplus(dt_ref[e] + prm_ref[0:1, :]), 0.0)
    a = -jnp.exp(prm_ref[1:2, :])
    dta = jnp.where(valid, dt * a, 0.0)
    dsum = prm_ref[2:3, :] + prm_ref[3:4, :]
    tt = lax.broadcasted_iota(jnp.int32, (lc, lc), 0)
    ss = lax.broadcasted_iota(jnp.int32, (lc, lc), 1)
    tril = jnp.where(ss <= tt, 1.0, 0.0).astype(BF16)
    triu = jnp.where(ss >= tt, 1.0, 0.0).astype(BF16)
    nq = lc // LANES
    td = lax.broadcasted_iota(jnp.int32, (LANES, LANES), 0)
    sd = lax.broadcasted_iota(jnp.int32, (LANES, LANES), 1)
    lower_d = td > sd
    upper_d = td < sd

    acs2_l, rowt_l, wts_l, eacs_l, edec_l, cbm_l, bm_l, cm_l = [], [], [], [], [], [], [], []
    for c in range(nchunk):
        rows = slice(c * lc, (c + 1) * lc)
        acs = jnp.where(fwd_lane, _sel_dot_left(tril, dta[rows]), _sel_dot_left(triu, dta[rows]))
        edge = jnp.where(fwd_lane, acs[lc - 1:lc, :], acs[0:1, :])
        acs2 = acs * LOG2E
        dtt = dt[rows].T
        rowt = acs2.T - jnp.log2(dtt)
        dgt = jnp.log2(dtt[0:nh, :] + dtt[nh:2 * nh, :])
        acs2_l.append(acs2)
        rowt_l.append((rowt, dgt))
        wts_l.append((dt[rows] * jnp.exp(edge - acs)).T)
        eacs_l.append(jnp.exp(acs))
        edec_l.append(jnp.exp(edge))
        cbm_g, bm_g, cm_g = [], [], []
        for g in range(SSD_GROUPS):
            bm = bc[rows, g * SSD_STATE:(g + 1) * SSD_STATE]
            cm = bc[rows, (SSD_GROUPS + g) * SSD_STATE:(SSD_GROUPS + g + 1) * SSD_STATE]
            bm_g.append(bm)
            cm_g.append(cm)
            cbm_g.append(_dot_nt(cm, bm))
        cbm_l.append(cbm_g)
        bm_l.append(bm_g)
        cm_l.append(cm_g)

    for p in range(HEAD_PAIRS):
        cols = slice(p * LANES, (p + 1) * LANES)
        g = p // (HEAD_PAIRS // SSD_GROUPS)
        hd = (2 * p, 2 * p + 1)
        x2b = x2_ref[e, :, cols]
        x2 = x2b.astype(F32)
        dsum2 = _pair_cols(first_half, dsum, hd[0], hd[1])
        if has_h0:
            hf = h0_ref[e, 0, p]
            hb = h0_ref[e, 1, p]
        else:
            hf = jnp.zeros((LANES, SSD_STATE), F32)
            hb = jnp.zeros((LANES, SSD_STATE), F32)
        ys = []
        stbs = []
        for c in range(nchunk):
            rows = slice(c * lc, (c + 1) * lc)
            x2c = x2[rows]
            x2cb = x2b[rows]
            acs2 = acs2_l[c]
            rowt, dgt = rowt_l[c]
            cbm = cbm_l[c][g]
            outs = []
            for h in hd:
                blocks = []
                for bi in range(nq):
                    ri = slice(bi * LANES, (bi + 1) * LANES)
                    cf = acs2[ri, h:h + 1]
                    cbk = acs2[ri, nh + h:nh + h + 1]
                    row_blocks = []
                    for bj in range(nq):
                        cj = slice(bj * LANES, (bj + 1) * LANES)
                        rf = rowt[h:h + 1, cj]
                        rb = rowt[nh + h:nh + h + 1, cj]
                        if bi > bj:
                            arg = cf - rf
                        elif bi < bj:
                            arg = cbk - rb
                        else:
                            arg = jnp.where(lower_d, cf - rf, jnp.where(upper_d, cbk - rb, dgt[h:h + 1, cj]))
                        row_blocks.append((cbm[ri, cj] * jnp.exp2(arg)).astype(BF16))
                    blocks.append(jnp.concatenate(row_blocks, axis=1))
                gm = jnp.concatenate(blocks, axis=0)
                outs.append(_dot(gm, x2cb))
            y = jnp.where(first_half, outs[0], outs[1]) + x2c * dsum2
            xt = x2c.T
            xwf = (xt * _pair_rows(wts_l[c], hd[0], hd[1], lc)).astype(BF16)
            xwb = (xt * _pair_rows(wts_l[c], nh + hd[0], nh + hd[1], lc)).astype(BF16)
            stf = _dot(xwf, bm_l[c][g])
            stbs.append(_dot(xwb, bm_l[c][g]))
            if has_h0 or c > 0:
                y = y + _dot_nt(cm_l[c][g], hf.astype(BF16)) * _pair_cols(first_half, eacs_l[c], hd[0], hd[1])
            hf = hf * _pair_scalars(edec_l[c], hd[0], hd[1]) + stf
            ys.append(y)
        for c in reversed(range(nchunk)):
            if has_h0 or c < nchunk - 1:
                ys[c] = ys[c] + (_dot_nt(cm_l[c][g], hb.astype(BF16))
                                 * _pair_cols(first_half, eacs_l[c], nh + hd[0], nh + hd[1]))
            hb = hb * _pair_scalars(edec_l[c], nh + hd[0], nh + hd[1]) + stbs[c]
        zg = zs_ref[e, :, cols].astype(F32)
        for c in range(nchunk):
            rows = slice(c * lc, (c + 1) * lc)
            yacc_ref[rows, cols] = ys[c] * zg[rows]
        if emit_state:
            st_ref[e, 0, p] = hf
            st_ref[e, 1, p] = hb

    y = yacc_ref[...]
    ms = jnp.mean(y * y, axis=-1, keepdims=True)
    ya_ref[e] = (y * lax.rsqrt(ms + EPS) * ng_ref[...]).astype(BF16)


def _ssd(zs3, x23, bc3, dt3, prm, norm_g, h0, *, emit_state, n_elems):
    b, s, _ = zs3.shape
    has_h0 = h0 is not None
    ne = n_elems
    in_specs = [
        pl.BlockSpec((ne, s, SSD_INNER), lambda i: (i, 0, 0)),
        pl.BlockSpec((ne, s, SSD_INNER), lambda i: (i, 0, 0)),
        pl.BlockSpec((ne, s, SSD_XBC - SSD_INNER), lambda i: (i, 0, 0)),
        pl.BlockSpec((ne, s, LANES), lambda i: (i, 0, 0)),
        pl.BlockSpec((8, LANES), lambda i: (0, 0)),
        pl.BlockSpec((1, SSD_INNER), lambda i: (0, 0)),
    ]
    args = [zs3, x23, bc3, dt3, prm, norm_g]
    state_block = (ne, 2, HEAD_PAIRS, LANES, SSD_STATE)
    if has_h0:
        in_specs.append(pl.BlockSpec(state_block, lambda i: (i, 0, 0, 0, 0)))
        args.append(h0)
    out_specs = [pl.BlockSpec((ne, s, SSD_INNER), lambda i: (i, 0, 0))]
    out_shape = [jax.ShapeDtypeStruct((b, s, SSD_INNER), BF16)]
    if emit_state:
        out_specs.append(pl.BlockSpec(state_block, lambda i: (i, 0, 0, 0, 0)))
        out_shape.append(jax.ShapeDtypeStruct((b, 2, HEAD_PAIRS, LANES, SSD_STATE), F32))
    kern = functools.partial(_ssd_kernel, n_elems=ne, seq_len=s, has_h0=has_h0, emit_state=emit_state)
    return pl.pallas_call(
        kern,
        grid=(b // ne,),
        in_specs=in_specs,
        out_specs=out_specs,
        out_shape=out_shape,
        scratch_shapes=[pltpu.VMEM((s, SSD_INNER), F32)],
        compiler_params=_cparams(("arbitrary",)),
        name="ssd",
    )(*args)


def _pair_masks():
    lane = lax.broadcasted_iota(jnp.int32, (1, LANES), 1)
    return lane < NA_HEAD_DIM


def _qkv_ctx_attn_kernel(x_ref, sh_ref, sc_ref, g_ref, w_ref, o_ref, nk_ref, nv_ref, qkv_ref,
                         *, tm, seq_len, row_base, rows_per_group):
    r = _mod_row(pl.program_id(0), tm, row_base, rows_per_group)
    sh = sh_ref[0, pl.ds(r, 1), :]
    sc = sc_ref[0, pl.ds(r, 1), :]
    h = _norm_mod(x_ref[...], g_ref[...], sh, sc).astype(BF16)
    d = x_ref.shape[1]
    tn = 512
    s = seq_len
    for part, out_ref in ((1, nk_ref), (2, nv_ref)):
        for j in range(d // tn):
            c0 = part * d + j * tn
            qkv_ref[:, c0:c0 + tn] = _dot(h, w_ref[:, c0:c0 + tn])
        for b in range(tm // s):
            rows = slice(b * s, (b + 1) * s)
            out_ref[b, 0] = qkv_ref[rows, part * d:(part + 1) * d].reshape(s, NA_HEADS, NA_HEAD_DIM)
    for j in range(d // tn):
        qkv_ref[:, j * tn:(j + 1) * tn] = _dot(h, w_ref[:, j * tn:(j + 1) * tn])
    first_half = _pair_masks()
    scale = NA_HEAD_DIM ** -0.5
    for b in range(tm // s):
        rows = slice(b * s, (b + 1) * s)
        for p in range(HEAD_PAIRS):
            cols = slice(p * LANES, (p + 1) * LANES)
            q2 = qkv_ref[rows, cols] * scale
            kb = qkv_ref[rows, d + p * LANES:d + (p + 1) * LANES].astype(BF16)
            vb = qkv_ref[rows, 2 * d + p * LANES:2 * d + (p + 1) * LANES].astype(BF16)
            qs = jnp.concatenate([jnp.where(first_half, q2, 0.0), jnp.where(first_half, 0.0, q2)], axis=0).astype(BF16)
            sco = _dot_nt(qs, kb)
            e = jnp.exp(sco - jnp.max(sco, axis=-1, keepdims=True))
            pv = _dot(e.astype(BF16), vb) / jnp.sum(e, axis=-1, keepdims=True)
            o_ref[rows, cols] = jnp.where(first_half, pv[:s], pv[s:]).astype(BF16)


def _qkv_ctx_attn(x, mod, layer, g, w, *, seq_len, row_base, rows_per_group, tm=512):
    m, d = x.shape
    nb = tm // seq_len
    cache_block = (nb, 1, seq_len, NA_HEADS, NA_HEAD_DIM)
    cache_shape = jax.ShapeDtypeStruct((m // seq_len, 1, seq_len, NA_HEADS, NA_HEAD_DIM), F32)
    kern = functools.partial(_qkv_ctx_attn_kernel, tm=tm, seq_len=seq_len, row_base=row_base,
                             rows_per_group=rows_per_group)
    return pl.pallas_call(
        kern,
        grid=(m // tm,),
        in_specs=[
            pl.BlockSpec((tm, d), lambda i: (i, 0)),
            pl.BlockSpec((1, 8, d), lambda i: (layer, 0, 0)),
            pl.BlockSpec((1, 8, d), lambda i: (layer, 0, 1)),
            pl.BlockSpec((1, d), lambda i: (0, 0)),
            _RESIDENT,
        ],
        out_specs=[
            pl.BlockSpec((tm, d), lambda i: (i, 0)),
            pl.BlockSpec(cache_block, lambda i: (i, 0, 0, 0, 0)),
            pl.BlockSpec(cache_block, lambda i: (i, 0, 0, 0, 0)),
        ],
        out_shape=[jax.ShapeDtypeStruct((m, d), BF16), cache_shape, cache_shape],
        scratch_shapes=[pltpu.VMEM((tm, 3 * d), F32)],
        compiler_params=_cparams(("arbitrary",)),
        name="qkv_ctx_attention",
    )(x, mod, mod, g.reshape(1, d), w)


def _lat_attn_kernel(q_ref, k_ref, v_ref, kc_ref, vc_ref, rpb_ref, o_ref, bias_ref, *, rows):
    b = pl.program_id(1)
    first_half = _pair_masks()
    kh = min(NA_KH, rows)
    win = kh * GRID_W
    neg_inf = -jnp.inf

    def win_start(r):
        return min(max(r - kh // 2, 0), rows - kh)

    @pl.when(b == 0)
    def _():
        qc = lax.broadcasted_iota(jnp.int32, (GRID_W, LANES), 0)
        kc = lax.broadcasted_iota(jnp.int32, (GRID_W, LANES), 1)
        cs = jnp.clip(qc - NA_KW // 2, 0, GRID_W - NA_KW)
        col_ok = (kc >= cs) & (kc < cs + NA_KW)
        for hh in range(2):
            for dr in range(2 * NA_KH - 1):
                v = jnp.broadcast_to(rpb_ref[hh, dr:dr + 1, :], (GRID_W, LANES))
                t = pltpu.roll(v, LANES - (NA_KW - 1), 1, stride=1, stride_axis=0)
                tile = jnp.where(col_ok, t, neg_inf)[:, :GRID_W]
                for r in range(rows):
                    i = dr - (NA_KH - 1) + r - win_start(r)
                    if 0 <= i < kh:
                        bias_ref[r, hh * GRID_W:(hh + 1) * GRID_W, i * GRID_W:(i + 1) * GRID_W] = tile

    scale = NA_HEAD_DIM ** -0.5
    kb = k_ref[...].astype(BF16)
    vb = v_ref[...].astype(BF16)
    kcb = kc_ref[...].astype(BF16)
    vcb = vc_ref[...].astype(BF16)
    group = 4
    for r0 in range(0, rows, group):
        rr = range(r0, r0 + group)
        qrows = [slice(r * GRID_W, (r + 1) * GRID_W) for r in rr]
        krows = [slice(win_start(r) * GRID_W, win_start(r) * GRID_W + win) for r in rr]
        qs = []
        for qr in qrows:
            q2 = q_ref[qr, :] * scale
            qs.append(jnp.concatenate([jnp.where(first_half, q2, 0.0), jnp.where(first_half, 0.0, q2)],
                                      axis=0).astype(BF16))
        s_loc = [_dot_nt(qi, kb[kr]) + bias_ref[r] for qi, kr, r in zip(qs, krows, rr)]
        s_ctx = [_dot_nt(qi, kcb) for qi in qs]
        mx = [jnp.maximum(jnp.max(sl, axis=-1, keepdims=True), jnp.max(sc, axis=-1, keepdims=True))
              for sl, sc in zip(s_loc, s_ctx)]
        e_loc = [jnp.exp(sl - m) for sl, m in zip(s_loc, mx)]
        e_ctx = [jnp.exp(sc - m) for sc, m in zip(s_ctx, mx)]
        den = [jnp.sum(el, axis=-1, keepdims=True) + jnp.sum(ec, axis=-1, keepdims=True) for el, ec in zip(e_loc, e_ctx)]
        pv = [(_dot(el.astype(BF16), vb[kr]) + _dot(ec.astype(BF16), vcb)) / dn
              for el, ec, kr, dn in zip(e_loc, e_ctx, krows, den)]
        for qr, o in zip(qrows, pv):
            o_ref[qr, :] = jnp.where(first_half, o[:GRID_W], o[GRID_W:]).astype(BF16)


def _lat_attn(q, k, v, kc, vc, rpb_pad, *, n_batch, n_tok, n_ctx):
    d = q.shape[1]
    rows = n_tok // GRID_W
    kh = min(NA_KH, rows)
    tok_spec = pl.BlockSpec((n_tok, LANES), lambda p, b: (b, p))
    ctx_spec = pl.BlockSpec((n_ctx, LANES), lambda p, b: (b, p))
    kern = functools.partial(_lat_attn_kernel, rows=rows)
    return pl.pallas_call(
        kern,
        grid=(HEAD_PAIRS, n_batch),
        in_specs=[tok_spec, tok_spec, tok_spec, ctx_spec, ctx_spec,
                  pl.BlockSpec((2, 2 * NA_KH, LANES), lambda p, b: (p, 0, 0))],
        out_specs=tok_spec,
        out_shape=jax.ShapeDtypeStruct((n_batch * n_tok, d), BF16),
        scratch_shapes=[pltpu.VMEM((rows, 2 * GRID_W, kh * GRID_W), F32)],
        compiler_params=_cparams(("arbitrary", "arbitrary")),
        name="latent_attention",
    )(q, k, v, kc, vc, rpb_pad)


def _pad_lanes(v):
    return jnp.pad(v, (0, LANES - v.shape[0]))


def _run_stream(x3, mod, row_base, W, *, is_ctx, state_ssd=None, cache_k=None, cache_v=None):
    b, s, d = x3.shape
    m = b * s
    x = x3.reshape(m, d)
    rpg = m if is_ctx else s
    kw = dict(row_base=row_base, rows_per_group=rpg)

    zs, x2, bca, yb, dtr = _in_proj(x, mod, W['norm_mix_g'][0], W['w_in'], W['w_in_tail'], W['ssd_conv_w'], W['ssd_conv_b'],
                                    W['sc_conv_w_rot'], seq_len=s, tm=1024, **kw)
    h0 = None
    if not is_ctx:
        h0 = state_ssd[:, 0].reshape(b, 2, HEAD_PAIRS, LANES, SSD_STATE)
    res = _ssd(zs.reshape(b, s, -1), x2.reshape(b, s, -1), bca.reshape(b, s, -1), dtr.reshape(b, s, LANES),
               W['ssd_prm'], W['ssd_norm_g'], h0, emit_state=is_ctx, n_elems=2 if is_ctx else 1)
    ya = res[0].reshape(m, SSD_INNER)
    x = _mix_ffn(x, mod, 0, [ya, yb], W['mix0_w_out'], W['norm_ffn_g'][0], *W['ffn'], W['final_norm_g'], seq_len=s,
                 final_norm=False, **kw)

    if is_ctx:
        o, new_k, new_v = _qkv_ctx_attn(x, mod, 1, W['norm_mix_g'][1], W['na_w_qkv'], seq_len=s, **kw)
    else:
        q, k, v = _nm_matmul(x, mod, 1, 0, 1, W['norm_mix_g'][1], W['na_w_qkv'], n_groups=3, group_width=D_MODEL, **kw)
        n_ctx = cache_k.shape[2]
        kc = cache_k[:, 0].reshape(b * n_ctx, d)
        vc = cache_v[:, 0].reshape(b * n_ctx, d)
        o = _lat_attn(q, k, v, kc, vc, W['rpb_pad'], n_batch=b, n_tok=s, n_ctx=n_ctx)
    x = _mix_ffn(x, mod, 1, [o], W['na_w_out'], W['norm_ffn_g'][1], *W['ffn'], W['final_norm_g'], seq_len=s,
                 final_norm=True, **kw)
    y = x.reshape(b, s, d)
    if is_ctx:
        new_state = res[1].reshape(b, 1, 2, SSD_HEADS, SSD_HEADDIM, SSD_STATE)
        return y, new_state, new_k, new_v
    return y


def kernel(x_prompt, x_sample, state_ssd, cache_k, cache_v, c, c_ctx, ada_w, ada_b, norm_mix_g, norm_ffn_g, ssd_w_in,
           ssd_conv_w, ssd_conv_b, ssd_dt_bias, ssd_a_log, ssd_d, ssd_norm_g, sc_conv_w, mix0_w_out, na_w_qkv, na_rpb,
           na_w_out, ffn_w_gate, ffn_w_up, ffn_conv_w, ffn_w_down, final_norm_g):
    n_lat = x_sample.shape[0]
    cvec = jnp.concatenate([c_ctx[None, :], c, jnp.zeros((8 - 1 - n_lat, D_MODEL), F32)], axis=0)
    mod = _ada(cvec, ada_w, ada_b)

    w_in = ssd_w_in[0].astype(BF16)
    w_in_tail = jnp.pad(ssd_w_in[0][:, TAIL_COL0:], ((0, 0), (0, LANES - MIX_SHIFT))).astype(BF16)
    w_out0 = mix0_w_out[0]
    w_out0 = jnp.concatenate([w_out0[:SSD_INNER], jnp.roll(w_out0[SSD_INNER:], MIX_SHIFT, axis=0)], axis=0).astype(BF16)
    prm =jnp.stack([_pad_lanes(ssd_dt_bias[0].reshape(-1)), _pad_lanes(ssd_a_log[0].reshape(-1)),
                     _pad_lanes(ssd_d[0, 0]), _pad_lanes(ssd_d[0, 1])] + [jnp.zeros((LANES,), F32)] * 4, axis=0)
    rpb = na_rpb[0]
    rpb_pad = jnp.pad(rpb, ((0, 0), (0, 1), (0, LANES - rpb.shape[2])))
    W = {
        'norm_mix_g': norm_mix_g, 'norm_ffn_g': norm_ffn_g, 'final_norm_g': final_norm_g,
        'w_in': w_in, 'w_in_tail': w_in_tail,
        'ssd_conv_w': ssd_conv_w[0], 'ssd_conv_b': ssd_conv_b[0].reshape(1, SSD_XBC), 'ssd_prm': prm,
        'ssd_norm_g': ssd_norm_g[0].reshape(1, SSD_INNER), 'sc_conv_w_rot': jnp.roll(sc_conv_w[0], MIX_SHIFT, axis=1),
        'mix0_w_out': w_out0,
        'na_w_qkv': na_w_qkv[0].astype(BF16), 'rpb_pad': rpb_pad, 'na_w_out': na_w_out[0].astype(BF16),
        'ffn': (ffn_w_gate.astype(BF16), ffn_w_up.astype(BF16), ffn_conv_w, ffn_w_down.astype(BF16)),
    }
    y_prompt, new_state, new_k, new_v = _run_stream(x_prompt, mod, 0, W, is_ctx=True)
    y_sample = _run_stream(x_sample, mod, 1, W, is_ctx=False, state_ssd=state_ssd, cache_k=cache_k, cache_v=cache_v)
    return (y_prompt, y_sample, new_state, new_k, new_v)
```

```python
import functools

import jax
import jax.numpy as jnp
from jax import lax
from jax.experimental import pallas as pl
from jax.experimental.pallas import tpu as pltpu

F32 = jnp.float32
BF16 = jnp.bfloat16

D_MODEL = 1024
EPS = 1e-6
GRID_W = 64
SSD_HEADDIM = 64
SSD_HEADS = 16
SSD_STATE = 128
SSD_GROUPS = 2
SSD_INNER = 1024
SSD_XBC = 1536
SSD_LC = 256
SC_WIDTH = 1024
NA_HEADS = 16
NA_HEAD_DIM = 64
NA_KH = 8
NA_KW = 16

LOG2E = 1.4426950408889634
LANES = 128
SUBLANES = 8
HEAD_PAIRS = 8

VMEM_LIMIT = 56 * 1024 * 1024

_RESIDENT = pl.BlockSpec(memory_space=pltpu.VMEM)


def _cparams(sem):
    return pltpu.CompilerParams(dimension_semantics=sem, vmem_limit_bytes=VMEM_LIMIT)


def _silu(x):
    return x * jax.nn.sigmoid(x)


def _dot(a, b):
    return jnp.dot(a, b, preferred_element_type=F32)


def _dot_nt(a, b):
    return lax.dot_general(a, b, (((1,), (1,)), ((), ())), preferred_element_type=F32)


def _split3(v):
    hi = v.astype(BF16)
    r1 = v - hi.astype(F32)
    mid = r1.astype(BF16)
    lo = (r1 - mid.astype(F32)).astype(BF16)
    return hi, mid, lo


def _sel_dot_left(sel, v):
    hi, mid, lo = _split3(v)
    return _dot(sel, hi) + _dot(sel, mid) + _dot(sel, lo)


def _norm_mod(x, g, shift, scale):
    ms = jnp.mean(x * x, axis=-1, keepdims=True)
    y = x * lax.rsqrt(ms + EPS) * g
    return y * (1.0 + scale) + shift


def _mod_row(i, tm, row_base, rows_per_group):
    return row_base + (i * tm) // rows_per_group


def _ada_kernel(c_ref, w_ref, b_ref, o_ref):
    s = _silu(c_ref[...]).astype(BF16)
    o_ref[0] = _dot(s, w_ref[0].astype(BF16)) + b_ref[0]


def _ada(cvec8, ada_w, ada_b):
    depth, d, n = ada_w.shape
    tn = 1536
    return pl.pallas_call(
        _ada_kernel,
        grid=(depth, n // tn),
        in_specs=[
            pl.BlockSpec((8, d), lambda l, j: (0, 0)),
            pl.BlockSpec((1, d, tn), lambda l, j: (l, 0, j)),
            pl.BlockSpec((1, 1, tn), lambda l, j: (l, 0, j)),
        ],
        out_specs=pl.BlockSpec((1, 8, tn), lambda l, j: (l, 0, j)),
        out_shape=jax.ShapeDtypeStruct((depth, 8, n), F32),
        compiler_params=_cparams(("arbitrary", "arbitrary")),
        name="ada_mod",
    )(cvec8, ada_w, ada_b.reshape(depth, 1, n))


def _nm_matmul_kernel(*refs, n_groups, group_width, tm, tn, row_base, rows_per_group):
    x_ref, sh_ref, sc_ref, g_ref, w_ref = refs[:5]
    o_refs = refs[5:5 + n_groups]
    r = _mod_row(pl.program_id(0), tm, row_base, rows_per_group)
    sh = sh_ref[0, pl.ds(r, 1), :]
    sc = sc_ref[0, pl.ds(r, 1), :]
    h = _norm_mod(x_ref[...], g_ref[...], sh, sc).astype(BF16)
    for gi, o_ref in enumerate(o_refs):
        for j in range(group_width // tn):
            o_ref[:, j * tn:(j + 1) * tn] = _dot(h, w_ref[:, gi * group_width + j * tn:gi * group_width + (j + 1) * tn])


def _nm_matmul(x, mod, layer, shift_idx, scale_idx, g, w, *, n_groups, group_width, row_base, rows_per_group,
               tm=512, tn=512):
    m, d = x.shape
    kern = functools.partial(_nm_matmul_kernel, n_groups=n_groups, group_width=group_width, tm=tm, tn=tn,
                             row_base=row_base, rows_per_group=rows_per_group)
    return pl.pallas_call(
        kern,
        grid=(m // tm,),
        in_specs=[
            pl.BlockSpec((tm, d), lambda i: (i, 0)),
            pl.BlockSpec((1, 8, d), lambda i: (layer, 0, shift_idx)),
            pl.BlockSpec((1, 8, d), lambda i: (layer, 0, scale_idx)),
            pl.BlockSpec((1, d), lambda i: (0, 0)),
            _RESIDENT,
        ],
        out_specs=[pl.BlockSpec((tm, group_width), lambda i: (i, 0)) for _ in range(n_groups)],
        out_shape=[jax.ShapeDtypeStruct((m, group_width), F32) for _ in range(n_groups)],
        compiler_params=_cparams(("arbitrary",)),
        name="norm_mod_matmul",
    )(x, mod, mod, g.reshape(1, d), w)


def _conv_scratch(rows, cols, seq_len):
    return pltpu.VMEM((cols // LANES, rows + SUBLANES * (rows // seq_len + 3), LANES), F32)


def _dwconv3(v, w, seq_len, cv_ref, row0=0):
    rows, ch = v.shape
    dyn0 = pl.program_id(0) * 0
    cuts = [0] + [t for t in range(1, rows) if (row0 + t) % seq_len == 0] + [rows]
    segs = list(zip(cuts[:-1], cuts[1:]))
    zeros = jnp.zeros((SUBLANES, LANES), F32)
    slabs = []
    for j in range(ch // LANES):
        cl = slice(j * LANES, (j + 1) * LANES)
        base = SUBLANES
        cv_ref[j, 0:SUBLANES, :] = zeros
        bases = []
        for a, b in segs:
            cv_ref[j, base:base + (b - a), :] = v[a:b, cl]
            cv_ref[j, base + (b - a):base + (b - a) + SUBLANES, :] = zeros
            bases.append(base)
            base += (b - a) + SUBLANES
        parts = []
        for (a, b), bs in zip(segs, bases):
            prev = cv_ref[j, pl.ds(bs - 1 + dyn0, b - a), :]
            nxt = cv_ref[j, pl.ds(bs + 1 + dyn0, b - a), :]
            parts.append(prev * w[0:1, cl] + v[a:b, cl] * w[1:2, cl] + nxt * w[2:3, cl])
        slabs.append(jnp.concatenate(parts, axis=0) if len(parts) > 1 else parts[0])
    return jnp.concatenate(slabs, axis=1) if len(slabs) > 1 else slabs[0]


MIX_SHIFT = 2 * SSD_HEADS
DT_COL0 = SSD_INNER + SSD_XBC
TAIL_COL0 = DT_COL0 + 3 * SC_WIDTH
IN_PROJ_TN = 256


def _in_proj_kernel(x_ref, sh_ref, sc_ref, g_ref, w_ref, wt_ref, cw_ref, cb_ref, scw_ref, zs_ref, x2_ref, bc_ref, yb_ref,
                    dt_ref, cv_ref,
                    *, tm, row_base, rows_per_group, seq_len):
    r = _mod_row(pl.program_id(0), tm, row_base, rows_per_group)
    sh = sh_ref[0, pl.ds(r, 1), :]
    sc = sc_ref[0, pl.ds(r, 1), :]
    h = _norm_mod(x_ref[...], g_ref[...], sh, sc).astype(BF16)
    tn = IN_PROJ_TN

    def proj(c0, width=tn):
        return _dot(h, w_ref[:, c0:c0 + width])

    for j in range(SSD_INNER // tn):
        cols = slice(j * tn, (j + 1) * tn)
        zs_ref[:, cols] = _silu(proj(j * tn)).astype(BF16)
        xc = _dwconv3(proj(SSD_INNER + j * tn), cw_ref[:, cols], seq_len, cv_ref) + cb_ref[:, cols]
        x2_ref[:, cols] = _silu(xc).astype(BF16)
    for j in range((SSD_XBC - SSD_INNER) // tn):
        cols = slice(SSD_INNER + j * tn, SSD_INNER + (j + 1) * tn)
        bcc = _dwconv3(proj(SSD_INNER + cols.start), cw_ref[:, cols], seq_len, cv_ref) + cb_ref[:, cols]
        bc_ref[:, j * tn:(j + 1) * tn] = _silu(bcc).astype(BF16)
    dt_ref[...] = proj(DT_COL0, LANES)
    q3 = _dot(h, wt_ref[...])
    tail = lax.broadcasted_iota(jnp.int32, (1, LANES), 1) < MIX_SHIFT
    for j in range(SC_WIDTH // tn):
        c0 = DT_COL0 + j * tn
        cols = slice(j * tn, (j + 1) * tn)
        p0, p1, p2 = proj(c0), proj(c0 + SC_WIDTH), proj(c0 + 2 * SC_WIDTH)
        y = p0 * _dwconv3(p1 * p2, scw_ref[:, cols], seq_len, cv_ref)
        if j == 0:
            yt = p1[:, :LANES] * _dwconv3(p2[:, :LANES] * q3, scw_ref[:, :LANES], seq_len, cv_ref)
            y = jnp.concatenate([jnp.where(tail, yt, y[:, :LANES]), y[:, LANES:]], axis=1)
        yb_ref[:, cols] = y.astype(BF16)


def _in_proj(x, mod, g, w, w_tail, conv_w, conv_b, sc_conv_w_rot, *, seq_len, row_base, rows_per_group, tm):
    m, d = x.shape
    kern = functools.partial(_in_proj_kernel, tm=tm, row_base=row_base, rows_per_group=rows_per_group, seq_len=seq_len)
    widths = (SSD_INNER, SSD_INNER, SSD_XBC - SSD_INNER, SC_WIDTH, LANES)
    dtypes = (BF16, BF16, BF16, BF16, F32)
    return pl.pallas_call(
        kern,
        grid=(m // tm,),
        in_specs=[
            pl.BlockSpec((tm, d), lambda i: (i, 0)),
            pl.BlockSpec((1, 8, d), lambda i: (0, 0, 0)),
            pl.BlockSpec((1, 8, d), lambda i: (0, 0, 1)),
            pl.BlockSpec((1, d), lambda i: (0, 0)),
            _RESIDENT,
            _RESIDENT,
            pl.BlockSpec((3, SSD_XBC), lambda i: (0, 0)),
            pl.BlockSpec((1, SSD_XBC), lambda i: (0, 0)),
            pl.BlockSpec((3, SC_WIDTH), lambda i: (0, 0)),
        ],
        out_specs=[pl.BlockSpec((tm, wd), lambda i: (i, 0)) for wd in widths],
        out_shape=[jax.ShapeDtypeStruct((m, wd), dt) for wd, dt in zip(widths, dtypes)],
        scratch_shapes=[_conv_scratch(tm, IN_PROJ_TN, seq_len)],
        compiler_params=_cparams(("arbitrary",)),
        name="in_proj",
    )(x, mod, mod, g.reshape(1, d), w, w_tail, conv_w, conv_b, sc_conv_w_rot)


def _ffn_kernel(*refs, n_parts, tm, tf, row_base, rows_per_group, seq_len, final_norm):
    x_ref, gate1_ref, sh_ref, sc_ref, gate_ref, g_ref, wo_ref = refs[:7]
    a_refs = refs[7:7 + n_parts]
    wg_ref, wu_ref, cw_ref, wd_ref, fg_ref, o_ref, h_ref, act_ref, cv_ref = refs[7 + n_parts:]
    r = _mod_row(pl.program_id(0), tm, row_base, rows_per_group)
    sh = sh_ref[0, pl.ds(r, 1), :]
    sc = sc_ref[0, pl.ds(r, 1), :]
    gate = gate_ref[0, pl.ds(r, 1), :]
    kp = wo_ref.shape[0] // n_parts
    sub = tm // 4
    for q in range(tm // sub):
        rq = slice(q * sub, (q + 1) * sub)
        mix = _dot(a_refs[0][rq, :], wo_ref[0:kp, :])
        for k in range(1, n_parts):
            mix = mix + _dot(a_refs[k][rq, :], wo_ref[k * kp:(k + 1) * kp, :])
        x1 = x_ref[rq, :] + gate1_ref[0, pl.ds(r, 1), :] * mix
        o_ref[rq, :] = x1
        h_ref[rq, :] = _norm_mod(x1, g_ref[...], sh, sc).astype(BF16)
    dff = wg_ref.shape[1]
    half = tm // 2
    win = half + SUBLANES
    for hb in range(2):
        w0 = hb * (half - SUBLANES)
        v0 = hb * SUBLANES
        rows = slice(hb * half, (hb + 1) * half)
        hw = h_ref[w0:w0 + win, :]
        hv = h_ref[rows, :]
        for fc in range(dff // tf):
            cols = slice(fc * tf, (fc + 1) * tf)
            gpre = _dwconv3(_dot(hw, wg_ref[:, cols]), cw_ref[:, cols], seq_len, cv_ref, row0=w0)[v0:v0 + half]
            up = _dot(hv, wu_ref[:, cols])
            act_ref[:, cols] = (jax.nn.gelu(gpre, approximate=True) * up).astype(BF16)
        y = o_ref[rows, :] + gate * _dot(act_ref[...], wd_ref[...])
        if final_norm:
            ms = jnp.mean(y * y, axis=-1, keepdims=True)
            y = y * lax.rsqrt(ms + EPS) * fg_ref[...]
        o_ref[rows, :] = y


def _layer_weight(shape):
    return lambda layer: pl.BlockSpec((None,) + shape, lambda i: (layer,) + (0,) * len(shape),
                                      pipeline_mode=pl.Buffered(1))


def _mix_ffn(x, mod, layer, parts, wo, g, wg, wu, cw, wd, final_g, *, row_base, rows_per_group, seq_len, final_norm,
             tm=1024, tf=256):
    m, d = x.shape
    dff = wg.shape[2]
    n_parts = len(parts)
    kern = functools.partial(_ffn_kernel, n_parts=n_parts, tm=tm, tf=tf, row_base=row_base,
                             rows_per_group=rows_per_group, seq_len=seq_len, final_norm=final_norm)
    mod_spec = lambda k: pl.BlockSpec((1, 8, d), lambda i: (layer, 0, k))
    in_specs = [pl.BlockSpec((tm, d), lambda i: (i, 0)), mod_spec(2), mod_spec(3), mod_spec(4), mod_spec(5),
                pl.BlockSpec((1, d), lambda i: (0, 0)), _RESIDENT]
    in_specs += [pl.BlockSpec((tm, a.shape[1]), lambda i: (i, 0)) for a in parts]
    in_specs += [_layer_weight((d, dff))(layer), _layer_weight((d, dff))(layer), _layer_weight((3, dff))(layer),
                 _layer_weight((dff, d))(layer), pl.BlockSpec((1, d), lambda i: (0, 0))]
    return pl.pallas_call(
        kern,
        grid=(m // tm,),
        in_specs=in_specs,
        out_specs=pl.BlockSpec((tm, d), lambda i: (i, 0)),
        out_shape=jax.ShapeDtypeStruct((m, d), F32),
        scratch_shapes=[pltpu.VMEM((tm, d), BF16), pltpu.VMEM((tm // 2, dff), BF16),
                        _conv_scratch(tm // 2 + SUBLANES, tf, seq_len)],
        compiler_params=_cparams(("arbitrary",)),
        name="mix_ffn",
    )(x, mod, mod, mod, mod, g.reshape(1, d), wo, *parts, wg, wu, cw, wd, final_g.reshape(1, d))


def _pair_cols(first_half, arr, h0, h1):
    return jnp.where(first_half, arr[:, h0:h0 + 1], arr[:, h1:h1 + 1])


def _pair_rows(arr_t, h0, h1, width):
    return jnp.concatenate([jnp.broadcast_to(arr_t[h0:h0 + 1, :], (SSD_HEADDIM, width)),
                            jnp.broadcast_to(arr_t[h1:h1 + 1, :], (SSD_HEADDIM, width))], axis=0)


def _pair_scalars(row, h0, h1):
    return jnp.concatenate([jnp.broadcast_to(row[:, h0:h0 + 1], (SSD_HEADDIM, SSD_STATE)),
                            jnp.broadcast_to(row[:, h1:h1 + 1], (SSD_HEADDIM, SSD_STATE))], axis=0)


def _ssd_kernel(*refs, n_elems, **kw):
    for e in range(n_elems):
        _ssd_element(e, *refs, **kw)


def _ssd_element(e, *refs, seq_len, has_h0, emit_state):
    zs_ref, x2_ref, bc_ref, dt_ref, prm_ref, ng_ref = refs[:6]
    pos = 6
    if has_h0:
        h0_ref = refs[pos]
        pos += 1
    ya_ref = refs[pos]
    pos += 1
    if emit_state:
        st_ref = refs[pos]
        pos += 1
    yacc_ref = refs[pos]

    lc = SSD_LC
    nchunk = seq_len // lc
    nh = SSD_HEADS
    lane = lax.broadcasted_iota(jnp.int32, (1, LANES), 1)
    first_half = lane < SSD_HEADDIM
    fwd_lane = lane < nh

    bc = bc_ref[e]
    valid = lane < 2 * nh
    dt = jnp.where(valid, jax.nn.softplus(dt_ref[e] + prm_ref[0:1, :]), 0.0)
    a = -jnp.exp(prm_ref[1:2, :])
    dta = jnp.where(valid, dt * a, 0.0)
    dsum = prm_ref[2:3, :] + prm_ref[3:4, :]
    tt = lax.broadcasted_iota(jnp.int32, (lc, lc), 0)
    ss = lax.broadcasted_iota(jnp.int32, (lc, lc), 1)
    tril = jnp.where(ss <= tt, 1.0, 0.0).astype(BF16)
    triu = jnp.where(ss >= tt, 1.0, 0.0).astype(BF16)
    nq = lc // LANES
    td = lax.broadcasted_iota(jnp.int32, (LANES, LANES), 0)
    sd = lax.broadcasted_iota(jnp.int32, (LANES, LANES), 1)
    lower_d = td > sd
    upper_d = td < sd

    acs2_l, rowt_l, wts_l, eacs_l, edec_l, cbm_l, bm_l, cm_l = [], [], [], [], [], [], [], []
    for c in range(nchunk):
        rows = slice(c * lc, (c + 1) * lc)
        acs = jnp.where(fwd_lane, _sel_dot_left(tril, dta[rows]), _sel_dot_left(triu, dta[rows]))
        edge = jnp.where(fwd_lane, acs[lc - 1:lc, :], acs[0:1, :])
        acs2 = acs * LOG2E
        dtt = dt[rows].T
        rowt = acs2.T - jnp.log2(dtt)
        dgt = jnp.log2(dtt[0:nh, :] + dtt[nh:2 * nh, :])
        acs2_l.append(acs2)
        rowt_l.append((rowt, dgt))
        wts_l.append((dt[rows] * jnp.exp(edge - acs)).T)
        eacs_l.append(jnp.exp(acs))
        edec_l.append(jnp.exp(edge))
        cbm_g, bm_g, cm_g = [], [], []
        for g in range(SSD_GROUPS):
            bm = bc[rows, g * SSD_STATE:(g + 1) * SSD_STATE]
            cm = bc[rows, (SSD_GROUPS + g) * SSD_STATE:(SSD_GROUPS + g + 1) * SSD_STATE]
            bm_g.append(bm)
            cm_g.append(cm)
            cbm_g.append(_dot_nt(cm, bm))
        cbm_l.append(cbm_g)
        bm_l.append(bm_g)
        cm_l.append(cm_g)

    for p in range(HEAD_PAIRS):
        cols = slice(p * LANES, (p + 1) * LANES)
        g = p // (HEAD_PAIRS // SSD_GROUPS)
        hd = (2 * p, 2 * p + 1)
        x2b = x2_ref[e, :, cols]
        x2 = x2b.astype(F32)
        dsum2 = _pair_cols(first_half, dsum, hd[0], hd[1])
        if has_h0:
            hf = h0_ref[e, 0, p]
            hb = h0_ref[e, 1, p]
        else:
            hf = jnp.zeros((LANES, SSD_STATE), F32)
            hb = jnp.zeros((LANES, SSD_STATE), F32)
        ys = []
        stbs = []
        for c in range(nchunk):
            rows = slice(c * lc, (c + 1) * lc)
            x2c = x2[rows]
            x2cb = x2b[rows]
            acs2 = acs2_l[c]
            rowt, dgt = rowt_l[c]
            cbm = cbm_l[c][g]
            outs = []
            for h in hd:
                blocks = []
                for bi in range(nq):
                    ri = slice(bi * LANES, (bi + 1) * LANES)
                    cf = acs2[ri, h:h + 1]
                    cbk = acs2[ri, nh + h:nh + h + 1]
                    row_blocks = []
                    for bj in range(nq):
                        cj = slice(bj * LANES, (bj + 1) * LANES)
                        rf = rowt[h:h + 1, cj]
                        rb = rowt[nh + h:nh + h + 1, cj]
                        if bi > bj:
                            arg = cf - rf
                        elif bi < bj:
                            arg = cbk - rb
                        else:
                            arg = jnp.where(lower_d, cf - rf, jnp.where(upper_d, cbk - rb, dgt[h:h + 1, cj]))
                        row_blocks.append((cbm[ri, cj] * jnp.exp2(arg)).astype(BF16))
                    blocks.append(jnp.concatenate(row_blocks, axis=1))
                gm = jnp.concatenate(blocks, axis=0)
                outs.append(_dot(gm, x2cb))
            y = jnp.where(first_half, outs[0], outs[1]) + x2c * dsum2
            xt = x2c.T
            xwf = (xt * _pair_rows(wts_l[c], hd[0], hd[1], lc)).astype(BF16)
            xwb = (xt * _pair_rows(wts_l[c], nh + hd[0], nh + hd[1], lc)).astype(BF16)
            stf = _dot(xwf, bm_l[c][g])
            stbs.append(_dot(xwb, bm_l[c][g]))
            if has_h0 or c > 0:
                y = y + _dot_nt(cm_l[c][g], hf.astype(BF16)) * _pair_cols(first_half, eacs_l[c], hd[0], hd[1])
            hf = hf * _pair_scalars(edec_l[c], hd[0], hd[1]) + stf
            ys.append(y)
        for c in reversed(range(nchunk)):
            if has_h0 or c < nchunk - 1:
                ys[c] = ys[c] + (_dot_nt(cm_l[c][g], hb.astype(BF16))
                                 * _pair_cols(first_half, eacs_l[c], nh + hd[0], nh + hd[1]))
            hb = hb * _pair_scalars(edec_l[c], nh + hd[0], nh + hd[1]) + stbs[c]
        zg = zs_ref[e, :, cols].astype(F32)
        for c in range(nchunk):
            rows = slice(c * lc, (c + 1) * lc)
            yacc_ref[rows, cols] = ys[c] * zg[rows]
        if emit_state:
            st_ref[e, 0, p] = hf
            st_ref[e, 1, p] = hb

    y = yacc_ref[...]
    ms = jnp.mean(y * y, axis=-1, keepdims=True)
    ya_ref[e] = (y * lax.rsqrt(ms + EPS) * ng_ref[...]).astype(BF16)


def _ssd(zs3, x23, bc3, dt3, prm, norm_g, h0, *, emit_state, n_elems):
    b, s, _ = zs3.shape
    has_h0 = h0 is not None
    ne = n_elems
    in_specs = [
        pl.BlockSpec((ne, s, SSD_INNER), lambda i: (i, 0, 0)),
        pl.BlockSpec((ne, s, SSD_INNER), lambda i: (i, 0, 0)),
        pl.BlockSpec((ne, s, SSD_XBC - SSD_INNER), lambda i: (i, 0, 0)),
        pl.BlockSpec((ne, s, LANES), lambda i: (i, 0, 0)),
        pl.BlockSpec((8, LANES), lambda i: (0, 0)),
        pl.BlockSpec((1, SSD_INNER), lambda i: (0, 0)),
    ]
    args = [zs3, x23, bc3, dt3, prm, norm_g]
    state_block = (ne, 2, HEAD_PAIRS, LANES, SSD_STATE)
    if has_h0:
        in_specs.append(pl.BlockSpec(state_block, lambda i: (i, 0, 0, 0, 0)))
        args.append(h0)
    out_specs = [pl.BlockSpec((ne, s, SSD_INNER), lambda i: (i, 0, 0))]
    out_shape = [jax.ShapeDtypeStruct((b, s, SSD_INNER), BF16)]
    if emit_state:
        out_specs.append(pl.BlockSpec(state_block, lambda i: (i, 0, 0, 0, 0)))
        out_shape.append(jax.ShapeDtypeStruct((b, 2, HEAD_PAIRS, LANES, SSD_STATE), F32))
    kern = functools.partial(_ssd_kernel, n_elems=ne, seq_len=s, has_h0=has_h0, emit_state=emit_state)
    return pl.pallas_call(
        kern,
        grid=(b // ne,),
        in_specs=in_specs,
        out_specs=out_specs,
        out_shape=out_shape,
        scratch_shapes=[pltpu.VMEM((s, SSD_INNER), F32)],
        compiler_params=_cparams(("arbitrary",)),
        name="ssd",
    )(*args)


def _pair_masks():
    lane = lax.broadcasted_iota(jnp.int32, (1, LANES), 1)
    return lane < NA_HEAD_DIM


def _qkv_ctx_attn_kernel(x_ref, sh_ref, sc_ref, g_ref, w_ref, o_ref, nk_ref, nv_ref, qkv_ref,
                         *, tm, seq_len, row_base, rows_per_group):
    r = _mod_row(pl.program_id(0), tm, row_base, rows_per_group)
    sh = sh_ref[0, pl.ds(r, 1), :]
    sc = sc_ref[0, pl.ds(r, 1), :]
    h = _norm_mod(x_ref[...], g_ref[...], sh, sc).astype(BF16)
    d = x_ref.shape[1]
    tn = 512
    s = seq_len
    for part, out_ref in ((1, nk_ref), (2, nv_ref)):
        for j in range(d // tn):
            c0 = part * d + j * tn
            qkv_ref[:, c0:c0 + tn] = _dot(h, w_ref[:, c0:c0 + tn])
        for b in range(tm // s):
            rows = slice(b * s, (b + 1) * s)
            out_ref[b, 0] = qkv_ref[rows, part * d:(part + 1) * d].reshape(s, NA_HEADS, NA_HEAD_DIM)
    for j in range(d // tn):
        qkv_ref[:, j * tn:(j + 1) * tn] = _dot(h, w_ref[:, j * tn:(j + 1) * tn])
    first_half = _pair_masks()
    scale = NA_HEAD_DIM ** -0.5
    for b in range(tm // s):
        rows = slice(b * s, (b + 1) * s)
        for p in range(HEAD_PAIRS):
            cols = slice(p * LANES, (p + 1) * LANES)
            q2 = qkv_ref[rows, cols] * scale
            kb = qkv_ref[rows, d + p * LANES:d + (p + 1) * LANES].astype(BF16)
            vb = qkv_ref[rows, 2 * d + p * LANES:2 * d + (p + 1) * LANES].astype(BF16)
            qs = jnp.concatenate([jnp.where(first_half, q2, 0.0), jnp.where(first_half, 0.0, q2)], axis=0).astype(BF16)
            sco = _dot_nt(qs, kb)
            e = jnp.exp(sco - jnp.max(sco, axis=-1, keepdims=True))
            pv = _dot(e.astype(BF16), vb) / jnp.sum(e, axis=-1, keepdims=True)
            o_ref[rows, cols] = jnp.where(first_half, pv[:s], pv[s:]).astype(BF16)


def _qkv_ctx_attn(x, mod, layer, g, w, *, seq_len, row_base, rows_per_group, tm=512):
    m, d = x.shape
    nb = tm // seq_len
    cache_block = (nb, 1, seq_len, NA_HEADS, NA_HEAD_DIM)
    cache_shape = jax.ShapeDtypeStruct((m // seq_len, 1, seq_len, NA_HEADS, NA_HEAD_DIM), F32)
    kern = functools.partial(_qkv_ctx_attn_kernel, tm=tm, seq_len=seq_len, row_base=row_base,
                             rows_per_group=rows_per_group)
    return pl.pallas_call(
        kern,
        grid=(m // tm,),
        in_specs=[
            pl.BlockSpec((tm, d), lambda i: (i, 0)),
            pl.BlockSpec((1, 8, d), lambda i: (layer, 0, 0)),
            pl.BlockSpec((1, 8, d), lambda i: (layer, 0, 1)),
            pl.BlockSpec((1, d), lambda i: (0, 0)),
            _RESIDENT,
        ],
        out_specs=[
            pl.BlockSpec((tm, d), lambda i: (i, 0)),
            pl.BlockSpec(cache_block, lambda i: (i, 0, 0, 0, 0)),
            pl.BlockSpec(cache_block, lambda i: (i, 0, 0, 0, 0)),
        ],
        out_shape=[jax.ShapeDtypeStruct((m, d), BF16), cache_shape, cache_shape],
        scratch_shapes=[pltpu.VMEM((tm, 3 * d), F32)],
        compiler_params=_cparams(("arbitrary",)),
        name="qkv_ctx_attention",
    )(x, mod, mod, g.reshape(1, d), w)


def _lat_attn_kernel(q_ref, k_ref, v_ref, kc_ref, vc_ref, rpb_ref, o_ref, bias_ref, *, rows):
    b = pl.program_id(1)
    first_half = _pair_masks()
    kh = min(NA_KH, rows)
    win = kh * GRID_W
    neg_inf = -jnp.inf

    def win_start(r):
        return min(max(r - kh // 2, 0), rows - kh)

    @pl.when(b == 0)
    def _():
        qc = lax.broadcasted_iota(jnp.int32, (GRID_W, LANES), 0)
        kc = lax.broadcasted_iota(jnp.int32, (GRID_W, LANES), 1)
        cs = jnp.clip(qc - NA_KW // 2, 0, GRID_W - NA_KW)
        col_ok = (kc >= cs) & (kc < cs + NA_KW)
        for hh in range(2):
            for dr in range(2 * NA_KH - 1):
                v = jnp.broadcast_to(rpb_ref[hh, dr:dr + 1, :], (GRID_W, LANES))
                t = pltpu.roll(v, LANES - (NA_KW - 1), 1, stride=1, stride_axis=0)
                tile = jnp.where(col_ok, t, neg_inf)[:, :GRID_W]
                for r in range(rows):
                    i = dr - (NA_KH - 1) + r - win_start(r)
                    if 0 <= i < kh:
                        bias_ref[r, hh * GRID_W:(hh + 1) * GRID_W, i * GRID_W:(i + 1) * GRID_W] = tile

    scale = NA_HEAD_DIM ** -0.5
    kb = k_ref[...].astype(BF16)
    vb = v_ref[...].astype(BF16)
    kcb = kc_ref[...].astype(BF16)
    vcb = vc_ref[...].astype(BF16)
    group = 4
    for r0 in range(0, rows, group):
        rr = range(r0, r0 + group)
        qrows = [slice(r * GRID_W, (r + 1) * GRID_W) for r in rr]
        krows = [slice(win_start(r) * GRID_W, win_start(r) * GRID_W + win) for r in rr]
        qs = []
        for qr in qrows:
            q2 = q_ref[qr, :] * scale
            qs.append(jnp.concatenate([jnp.where(first_half, q2, 0.0), jnp.where(first_half, 0.0, q2)],
                                      axis=0).astype(BF16))
        s_loc = [_dot_nt(qi, kb[kr]) + bias_ref[r] for qi, kr, r in zip(qs, krows, rr)]
        s_ctx = [_dot_nt(qi, kcb) for qi in qs]
        mx = [jnp.maximum(jnp.max(sl, axis=-1, keepdims=True), jnp.max(sc, axis=-1, keepdims=True))
              for sl, sc in zip(s_loc, s_ctx)]
        e_loc = [jnp.exp(sl - m) for sl, m in zip(s_loc, mx)]
        e_ctx = [jnp.exp(sc - m) for sc, m in zip(s_ctx, mx)]
        den = [jnp.sum(el, axis=-1, keepdims=True) + jnp.sum(ec, axis=-1, keepdims=True) for el, ec in zip(e_loc, e_ctx)]
        pv = [(_dot(el.astype(BF16), vb[kr]) + _dot(ec.astype(BF16), vcb)) / dn
              for el, ec, kr, dn in zip(e_loc, e_ctx, krows, den)]
        for qr, o in zip(qrows, pv):
            o_ref[qr, :] = jnp.where(first_half, o[:GRID_W], o[GRID_W:]).astype(BF16)


def _lat_attn(q, k, v, kc, vc, rpb_pad, *, n_batch, n_tok, n_ctx):
    d = q.shape[1]
    rows = n_tok // GRID_W
    kh = min(NA_KH, rows)
    tok_spec = pl.BlockSpec((n_tok, LANES), lambda p, b: (b, p))
    ctx_spec = pl.BlockSpec((n_ctx, LANES), lambda p, b: (b, p))
    kern = functools.partial(_lat_attn_kernel, rows=rows)
    return pl.pallas_call(
        kern,
        grid=(HEAD_PAIRS, n_batch),
        in_specs=[tok_spec, tok_spec, tok_spec, ctx_spec, ctx_spec,
                  pl.BlockSpec((2, 2 * NA_KH, LANES), lambda p, b: (p, 0, 0))],
        out_specs=tok_spec,
        out_shape=jax.ShapeDtypeStruct((n_batch * n_tok, d), BF16),
        scratch_shapes=[pltpu.VMEM((rows, 2 * GRID_W, kh * GRID_W), F32)],
        compiler_params=_cparams(("arbitrary", "arbitrary")),
        name="latent_attention",
    )(q, k, v, kc, vc, rpb_pad)


def _pad_lanes(v):
    return jnp.pad(v, (0, LANES - v.shape[0]))


def _run_stream(x3, mod, row_base, W, *, is_ctx, state_ssd=None, cache_k=None, cache_v=None):
    b, s, d = x3.shape
    m = b * s
    x = x3.reshape(m, d)
    rpg = m if is_ctx else s
    kw = dict(row_base=row_base, rows_per_group=rpg)

    zs, x2, bca, yb, dtr = _in_proj(x, mod, W['norm_mix_g'][0], W['w_in'], W['w_in_tail'], W['ssd_conv_w'], W['ssd_conv_b'],
                                    W['sc_conv_w_rot'], seq_len=s, tm=max(s, 512), **kw)
    h0 = None
    if not is_ctx:
        h0 = state_ssd[:, 0].reshape(b, 2, HEAD_PAIRS, LANES, SSD_STATE)
    res = _ssd(zs.reshape(b, s, -1), x2.reshape(b, s, -1), bca.reshape(b, s, -1), dtr.reshape(b, s, LANES),
               W['ssd_prm'], W['ssd_norm_g'], h0, emit_state=is_ctx, n_elems=2 if is_ctx else 1)
    ya = res[0].reshape(m, SSD_INNER)
    x = _mix_ffn(x, mod, 0, [ya, yb], W['mix0_w_out'], W['norm_ffn_g'][0], *W['ffn'], W['final_norm_g'], seq_len=s,
                 final_norm=False, **kw)

    if is_ctx:
        o, new_k, new_v = _qkv_ctx_attn(x, mod, 1, W['norm_mix_g'][1], W['na_w_qkv'], seq_len=s, **kw)
    else:
        q, k, v = _nm_matmul(x, mod, 1, 0, 1, W['norm_mix_g'][1], W['na_w_qkv'], n_groups=3, group_width=D_MODEL, **kw)
        n_ctx = cache_k.shape[2]
        kc = cache_k[:, 0].reshape(b * n_ctx, d)
        vc = cache_v[:, 0].reshape(b * n_ctx, d)
        o = _lat_attn(q, k, v, kc, vc, W['rpb_pad'], n_batch=b, n_tok=s, n_ctx=n_ctx)
    x = _mix_ffn(x, mod, 1, [o], W['na_w_out'], W['norm_ffn_g'][1], *W['ffn'], W['final_norm_g'], seq_len=s,
                 final_norm=True, **kw)
    y = x.reshape(b, s, d)
    if is_ctx:
        new_state = res[1].reshape(b, 1, 2, SSD_HEADS, SSD_HEADDIM, SSD_STATE)
        return y, new_state, new_k, new_v
    return y


def kernel(x_prompt, x_sample, state_ssd, cache_k, cache_v, c, c_ctx, ada_w, ada_b, norm_mix_g, norm_ffn_g, ssd_w_in,
           ssd_conv_w, ssd_conv_b, ssd_dt_bias, ssd_a_log, ssd_d, ssd_norm_g, sc_conv_w, mix0_w_out, na_w_qkv, na_rpb,
           na_w_out, ffn_w_gate, ffn_w_up, ffn_conv_w, ffn_w_down, final_norm_g):
    n_lat = x_sample.shape[0]
    cvec = jnp.concatenate([c_ctx[None, :], c, jnp.zeros((8 - 1 - n_lat, D_MODEL), F32)], axis=0)
    mod = _ada(cvec, ada_w, ada_b)

    w_in = ssd_w_in[0].astype(BF16)
    w_in_tail = jnp.pad(ssd_w_in[0][:, TAIL_COL0:], ((0, 0), (0, LANES - MIX_SHIFT))).astype(BF16)
    w_out0 = mix0_w_out[0]
    w_out0 = jnp.concatenate([w_out0[:SSD_INNER], jnp.roll(w_out0[SSD_INNER:], MIX_SHIFT, axis=0)], axis=0).astype(BF16)
    prm =jnp.stack([_pad_lanes(ssd_dt_bias[0].reshape(-1)), _pad_lanes(ssd_a_log[0].reshape(-1)),
                     _pad_lanes(ssd_d[0, 0]), _pad_lanes(ssd_d[0, 1])] + [jnp.zeros((LANES,), F32)] * 4, axis=0)
    rpb = na_rpb[0]
    rpb_pad = jnp.pad(rpb, ((0, 0), (0, 1), (0, LANES - rpb.shape[2])))
    W = {
        'norm_mix_g': norm_mix_g, 'norm_ffn_g': norm_ffn_g, 'final_norm_g': final_norm_g,
        'w_in': w_in, 'w_in_tail': w_in_tail,
        'ssd_conv_w': ssd_conv_w[0], 'ssd_conv_b': ssd_conv_b[0].reshape(1, SSD_XBC), 'ssd_prm': prm,
        'ssd_norm_g': ssd_norm_g[0].reshape(1, SSD_INNER), 'sc_conv_w_rot': jnp.roll(sc_conv_w[0], MIX_SHIFT, axis=1),
        'mix0_w_out': w_out0,
        'na_w_qkv': na_w_qkv[0].astype(BF16), 'rpb_pad': rpb_pad, 'na_w_out': na_w_out[0].astype(BF16),
        'ffn': (ffn_w_gate.astype(BF16), ffn_w_up.astype(BF16), ffn_conv_w, ffn_w_down.astype(BF16)),
    }
    y_prompt, new_state, new_k, new_v = _run_stream(x_prompt, mod, 0, W, is_ctx=True)
    y_sample = _run_stream(x_sample, mod, 1, W, is_ctx=False, state_ssd=state_ssd, cache_k=cache_k, cache_v=cache_v)
    return (y_prompt, y_sample, new_state, new_k, new_v)
```

```python
import functools

import jax
import jax.numpy as jnp
from jax import lax
from jax.experimental import pallas as pl
from jax.experimental.pallas import tpu as pltpu

F32 = jnp.float32
BF16 = jnp.bfloat16

D_MODEL = 1024
EPS = 1e-6
GRID_W = 64
SSD_HEADDIM = 64
SSD_HEADS = 16
SSD_STATE = 128
SSD_GROUPS = 2
SSD_INNER = 1024
SSD_XBC = 1536
SSD_LC = 256
SC_WIDTH = 1024
NA_HEADS = 16
NA_HEAD_DIM = 64
NA_KH = 8
NA_KW = 16

LOG2E = 1.4426950408889634
LANES = 128
SUBLANES = 8
HEAD_PAIRS = 8

VMEM_LIMIT = 56 * 1024 * 1024

_RESIDENT = pl.BlockSpec(memory_space=pltpu.VMEM)


def _cparams(sem):
    return pltpu.CompilerParams(dimension_semantics=sem, vmem_limit_bytes=VMEM_LIMIT)


def _silu(x):
    return x * jax.nn.sigmoid(x)


def _dot(a, b):
    return jnp.dot(a, b, preferred_element_type=F32)


def _dot_nt(a, b):
    return lax.dot_general(a, b, (((1,), (1,)), ((), ())), preferred_element_type=F32)


def _split3(v):
    hi = v.astype(BF16)
    r1 = v - hi.astype(F32)
    mid = r1.astype(BF16)
    lo = (r1 - mid.astype(F32)).astype(BF16)
    return hi, mid, lo


def _sel_dot_left(sel, v):
    hi, mid, lo = _split3(v)
    return _dot(sel, hi) + _dot(sel, mid) + _dot(sel, lo)


def _norm_mod(x, g, shift, scale):
    ms = jnp.mean(x * x, axis=-1, keepdims=True)
    y = x * lax.rsqrt(ms + EPS) * g
    return y * (1.0 + scale) + shift


def _mod_row(i, tm, row_base, rows_per_group):
    return row_base + (i * tm) // rows_per_group


def _ada_kernel(c_ref, w_ref, b_ref, o_ref):
    s = _silu(c_ref[...]).astype(BF16)
    o_ref[0] = _dot(s, w_ref[0].astype(BF16)) + b_ref[0]


def _ada(cvec8, ada_w, ada_b):
    depth, d, n = ada_w.shape
    tn = 1536
    return pl.pallas_call(
        _ada_kernel,
        grid=(depth, n // tn),
        in_specs=[
            pl.BlockSpec((8, d), lambda l, j: (0, 0)),
            pl.BlockSpec((1, d, tn), lambda l, j: (l, 0, j)),
            pl.BlockSpec((1, 1, tn), lambda l, j: (l, 0, j)),
        ],
        out_specs=pl.BlockSpec((1, 8, tn), lambda l, j: (l, 0, j)),
        out_shape=jax.ShapeDtypeStruct((depth, 8, n), F32),
        compiler_params=_cparams(("arbitrary", "arbitrary")),
        name="ada_mod",
    )(cvec8, ada_w, ada_b.reshape(depth, 1, n))


def _nm_matmul_kernel(*refs, n_groups, group_width, tm, tn, row_base, rows_per_group):
    x_ref, sh_ref, sc_ref, g_ref, w_ref = refs[:5]
    o_refs = refs[5:5 + n_groups]
    r = _mod_row(pl.program_id(0), tm, row_base, rows_per_group)
    sh = sh_ref[0, pl.ds(r, 1), :]
    sc = sc_ref[0, pl.ds(r, 1), :]
    h = _norm_mod(x_ref[...], g_ref[...], sh, sc).astype(BF16)
    for gi, o_ref in enumerate(o_refs):
        for j in range(group_width // tn):
            o_ref[:, j * tn:(j + 1) * tn] = _dot(h, w_ref[:, gi * group_width + j * tn:gi * group_width + (j + 1) * tn])


def _nm_matmul(x, mod, layer, shift_idx, scale_idx, g, w, *, n_groups, group_width, row_base, rows_per_group,
               tm=512, tn=512):
    m, d = x.shape
    kern = functools.partial(_nm_matmul_kernel, n_groups=n_groups, group_width=group_width, tm=tm, tn=tn,
                             row_base=row_base, rows_per_group=rows_per_group)
    return pl.pallas_call(
        kern,
        grid=(m // tm,),
        in_specs=[
            pl.BlockSpec((tm, d), lambda i: (i, 0)),
            pl.BlockSpec((1, 8, d), lambda i: (layer, 0, shift_idx)),
            pl.BlockSpec((1, 8, d), lambda i: (layer, 0, scale_idx)),
            pl.BlockSpec((1, d), lambda i: (0, 0)),
            _RESIDENT,
        ],
        out_specs=[pl.BlockSpec((tm, group_width), lambda i: (i, 0)) for _ in range(n_groups)],
        out_shape=[jax.ShapeDtypeStruct((m, group_width), F32) for _ in range(n_groups)],
        compiler_params=_cparams(("arbitrary",)),
        name="norm_mod_matmul",
    )(x, mod, mod, g.reshape(1, d), w)


def _conv_scratch(rows, cols, seq_len):
    return pltpu.VMEM((cols // LANES, rows + SUBLANES * (rows // seq_len + 3), LANES), F32)


def _dwconv3(v, w, seq_len, cv_ref, row0=0):
    rows, ch = v.shape
    dyn0 = pl.program_id(0) * 0
    cuts = [0] + [t for t in range(1, rows) if (row0 + t) % seq_len == 0] + [rows]
    segs = list(zip(cuts[:-1], cuts[1:]))
    zeros = jnp.zeros((SUBLANES, LANES), F32)
    slabs = []
    for j in range(ch // LANES):
        cl = slice(j * LANES, (j + 1) * LANES)
        base = SUBLANES
        cv_ref[j, 0:SUBLANES, :] = zeros
        bases = []
        for a, b in segs:
            cv_ref[j, base:base + (b - a), :] = v[a:b, cl]
            cv_ref[j, base + (b - a):base + (b - a) + SUBLANES, :] = zeros
            bases.append(base)
            base += (b - a) + SUBLANES
        parts = []
        for (a, b), bs in zip(segs, bases):
            prev = cv_ref[j, pl.ds(bs - 1 + dyn0, b - a), :]
            nxt = cv_ref[j, pl.ds(bs + 1 + dyn0, b - a), :]
            parts.append(prev * w[0:1, cl] + v[a:b, cl] * w[1:2, cl] + nxt * w[2:3, cl])
        slabs.append(jnp.concatenate(parts, axis=0) if len(parts) > 1 else parts[0])
    return jnp.concatenate(slabs, axis=1) if len(slabs) > 1 else slabs[0]


MIX_SHIFT = 2 * SSD_HEADS
DT_COL0 = SSD_INNER + SSD_XBC
TAIL_COL0 = DT_COL0 + 3 * SC_WIDTH
IN_PROJ_TN = 256


def _in_proj_kernel(x_ref, sh_ref, sc_ref, g_ref, w_ref, wt_ref, cw_ref, cb_ref, scw_ref, zs_ref, x2_ref, bc_ref, yb_ref,
                    dt_ref, cv_ref,
                    *, tm, row_base, rows_per_group, seq_len):
    r = _mod_row(pl.program_id(0), tm, row_base, rows_per_group)
    sh = sh_ref[0, pl.ds(r, 1), :]
    sc = sc_ref[0, pl.ds(r, 1), :]
    h = _norm_mod(x_ref[...], g_ref[...], sh, sc).astype(BF16)
    tn = IN_PROJ_TN

    def proj(c0, width=tn):
        return _dot(h, w_ref[:, c0:c0 + width])

    for j in range(SSD_INNER // tn):
        cols = slice(j * tn, (j + 1) * tn)
        zs_ref[:, cols] = _silu(proj(j * tn)).astype(BF16)
        xc = _dwconv3(proj(SSD_INNER + j * tn), cw_ref[:, cols], seq_len, cv_ref) + cb_ref[:, cols]
        x2_ref[:, cols] = _silu(xc).astype(BF16)
    for j in range((SSD_XBC - SSD_INNER) // tn):
        cols = slice(SSD_INNER + j * tn, SSD_INNER + (j + 1) * tn)
        bcc = _dwconv3(proj(SSD_INNER + cols.start), cw_ref[:, cols], seq_len, cv_ref) + cb_ref[:, cols]
        bc_ref[:, j * tn:(j + 1) * tn] = _silu(bcc).astype(BF16)
    dt_ref[...] = proj(DT_COL0, LANES)
    q3 = _dot(h, wt_ref[...])
    tail = lax.broadcasted_iota(jnp.int32, (1, LANES), 1) < MIX_SHIFT
    for j in range(SC_WIDTH // tn):
        c0 = DT_COL0 + j * tn
        cols = slice(j * tn, (j + 1) * tn)
        p0, p1, p2 = proj(c0), proj(c0 + SC_WIDTH), proj(c0 + 2 * SC_WIDTH)
        y = p0 * _dwconv3(p1 * p2, scw_ref[:, cols], seq_len, cv_ref)
        if j == 0:
            yt = p1[:, :LANES] * _dwconv3(p2[:, :LANES] * q3, scw_ref[:, :LANES], seq_len, cv_ref)
            y = jnp.concatenate([jnp.where(tail, yt, y[:, :LANES]), y[:, LANES:]], axis=1)
        yb_ref[:, cols] = y.astype(BF16)


def _in_proj(x, mod, g, w, w_tail, conv_w, conv_b, sc_conv_w_rot, *, seq_len, row_base, rows_per_group, tm):
    m, d = x.shape
    kern = functools.partial(_in_proj_kernel, tm=tm, row_base=row_base, rows_per_group=rows_per_group, seq_len=seq_len)
    widths = (SSD_INNER, SSD_INNER, SSD_XBC - SSD_INNER, SC_WIDTH, LANES)
    dtypes = (BF16, BF16, BF16, BF16, F32)
    return pl.pallas_call(
        kern,
        grid=(m // tm,),
        in_specs=[
            pl.BlockSpec((tm, d), lambda i: (i, 0)),
            pl.BlockSpec((1, 8, d), lambda i: (0, 0, 0)),
            pl.BlockSpec((1, 8, d), lambda i: (0, 0, 1)),
            pl.BlockSpec((1, d), lambda i: (0, 0)),
            _RESIDENT,
            _RESIDENT,
            pl.BlockSpec((3, SSD_XBC), lambda i: (0, 0)),
            pl.BlockSpec((1, SSD_XBC), lambda i: (0, 0)),
            pl.BlockSpec((3, SC_WIDTH), lambda i: (0, 0)),
        ],
        out_specs=[pl.BlockSpec((tm, wd), lambda i: (i, 0)) for wd in widths],
        out_shape=[jax.ShapeDtypeStruct((m, wd), dt) for wd, dt in zip(widths, dtypes)],
        scratch_shapes=[_conv_scratch(tm, IN_PROJ_TN, seq_len)],
        compiler_params=_cparams(("arbitrary",)),
        name="in_proj",
    )(x, mod, mod, g.reshape(1, d), w, w_tail, conv_w, conv_b, sc_conv_w_rot)


def _ffn_kernel(*refs, n_parts, layer, tm, tf, row_base, rows_per_group, seq_len, final_norm):
    x_ref, gate1_ref, sh_ref, sc_ref, gate_ref, g_ref, wo_ref = refs[:7]
    a_refs = refs[7:7 + n_parts]
    (wg_hbm, wu_hbm, cw_ref, wd_hbm, fg_ref, o_ref, h_ref, act_ref, cv_ref, wg_ref, wu_ref, wd_ref,
     w_sem) = refs[7 + n_parts:]

    def weight_copies():
        pairs = ((wg_hbm, wg_ref), (wu_hbm, wu_ref), (wd_hbm, wd_ref))
        return [pltpu.make_async_copy(src.at[layer], dst, w_sem.at[k]) for k, (src, dst) in enumerate(pairs)]

    first_step = pl.program_id(0) == 0

    @pl.when(first_step)
    def _():
        for cp in weight_copies():
            cp.start()

    r = _mod_row(pl.program_id(0), tm, row_base, rows_per_group)
    sh = sh_ref[0, pl.ds(r, 1), :]
    sc = sc_ref[0, pl.ds(r, 1), :]
    gate = gate_ref[0, pl.ds(r, 1), :]
    kp = wo_ref.shape[0] // n_parts
    sub = tm // 4
    for q in range(tm // sub):
        rq = slice(q * sub, (q + 1) * sub)
        mix = _dot(a_refs[0][rq, :], wo_ref[0:kp, :])
        for k in range(1, n_parts):
            mix = mix + _dot(a_refs[k][rq, :], wo_ref[k * kp:(k + 1) * kp, :])
        x1 = x_ref[rq, :] + gate1_ref[0, pl.ds(r, 1), :] * mix
        o_ref[rq, :] = x1
        h_ref[rq, :] = _norm_mod(x1, g_ref[...], sh, sc).astype(BF16)

    @pl.when(first_step)
    def _():
        for cp in weight_copies():
            cp.wait()

    dff = wg_ref.shape[1]
    half = tm // 2
    win = half + SUBLANES
    for hb in range(2):
        w0 = hb * (half - SUBLANES)
        v0 = hb * SUBLANES
        rows = slice(hb * half, (hb + 1) * half)
        hw = h_ref[w0:w0 + win, :]
        hv = h_ref[rows, :]
        for fc in range(dff // tf):
            cols = slice(fc * tf, (fc + 1) * tf)
            gpre = _dwconv3(_dot(hw, wg_ref[:, cols]), cw_ref[:, cols], seq_len, cv_ref, row0=w0)[v0:v0 + half]
            up = _dot(hv, wu_ref[:, cols])
            act_ref[:, cols] = (jax.nn.gelu(gpre, approximate=True) * up).astype(BF16)
        y = o_ref[rows, :] + gate * _dot(act_ref[...], wd_ref[...])
        if final_norm:
            ms = jnp.mean(y * y, axis=-1, keepdims=True)
            y = y * lax.rsqrt(ms + EPS) * fg_ref[...]
        o_ref[rows, :] = y


def _layer_weight(shape):
    return lambda layer: pl.BlockSpec((None,) + shape, lambda i: (layer,) + (0,) * len(shape),
                                      pipeline_mode=pl.Buffered(1))


def _mix_ffn(x, mod, layer, parts, wo, g, wg, wu, cw, wd, final_g, *, row_base, rows_per_group, seq_len, final_norm,
             tm=1024, tf=256):
    m, d = x.shape
    dff = wg.shape[2]
    n_parts = len(parts)
    kern = functools.partial(_ffn_kernel, n_parts=n_parts, layer=layer, tm=tm, tf=tf, row_base=row_base,
                             rows_per_group=rows_per_group, seq_len=seq_len, final_norm=final_norm)
    mod_spec = lambda k: pl.BlockSpec((1, 8, d), lambda i: (layer, 0, k))
    in_specs = [pl.BlockSpec((tm, d), lambda i: (i, 0)), mod_spec(2), mod_spec(3), mod_spec(4), mod_spec(5),
                pl.BlockSpec((1, d), lambda i: (0, 0)), _RESIDENT]
    in_specs += [pl.BlockSpec((tm, a.shape[1]), lambda i: (i, 0)) for a in parts]
    hbm = pl.BlockSpec(memory_space=pl.ANY)
    in_specs += [hbm, hbm, _layer_weight((3, dff))(layer), hbm, pl.BlockSpec((1, d), lambda i: (0, 0))]
    return pl.pallas_call(
        kern,
        grid=(m // tm,),
        in_specs=in_specs,
        out_specs=pl.BlockSpec((tm, d), lambda i: (i, 0)),
        out_shape=jax.ShapeDtypeStruct((m, d), F32),
        scratch_shapes=[pltpu.VMEM((tm, d), BF16), pltpu.VMEM((tm // 2, dff), BF16),
                        _conv_scratch(tm // 2 + SUBLANES, tf, seq_len),
                        pltpu.VMEM((d, dff), BF16), pltpu.VMEM((d, dff), BF16), pltpu.VMEM((dff, d), BF16),
                        pltpu.SemaphoreType.DMA((3,))],
        compiler_params=_cparams(("arbitrary",)),
        name="mix_ffn",
    )(x, mod, mod, mod, mod, g.reshape(1, d), wo, *parts, wg, wu, cw, wd, final_g.reshape(1, d))


def _pair_cols(first_half, arr, h0, h1):
    return jnp.where(first_half, arr[:, h0:h0 + 1], arr[:, h1:h1 + 1])


def _pair_rows(arr_t, h0, h1, width):
    return jnp.concatenate([jnp.broadcast_to(arr_t[h0:h0 + 1, :], (SSD_HEADDIM, width)),
                            jnp.broadcast_to(arr_t[h1:h1 + 1, :], (SSD_HEADDIM, width))], axis=0)


def _pair_scalars(row, h0, h1):
    return jnp.concatenate([jnp.broadcast_to(row[:, h0:h0 + 1], (SSD_HEADDIM, SSD_STATE)),
                            jnp.broadcast_to(row[:, h1:h1 + 1], (SSD_HEADDIM, SSD_STATE))], axis=0)


def _ssd_kernel(*refs, n_elems, **kw):
    for e in range(n_elems):
        _ssd_element(e, *refs, **kw)


def _ssd_element(e, *refs, seq_len, has_h0, emit_state):
    zs_ref, x2_ref, bc_ref, dt_ref, prm_ref, ng_ref = refs[:6]
    pos = 6
    if has_h0:
        h0_ref = refs[pos]
        pos += 1
    ya_ref = refs[pos]
    pos += 1
    if emit_state:
        st_ref = refs[pos]
        pos += 1
    yacc_ref = refs[pos]

    lc = SSD_LC
    nchunk = seq_len // lc
    nh = SSD_HEADS
    lane = lax.broadcasted_iota(jnp.int32, (1, LANES), 1)
    first_half = lane < SSD_HEADDIM
    fwd_lane = lane < nh

    bc = bc_ref[e]
    valid = lane < 2 * nh
    dt = jnp.where(valid, jax.nn.softplus(dt_ref[e] + prm_ref[0:1, :]), 0.0)
    a = -jnp.exp(prm_ref[1:2, :])
    dta = jnp.where(valid, dt * a, 0.0)
    dsum = prm_ref[2:3, :] + prm_ref[3:4, :]
    tt = lax.broadcasted_iota(jnp.int32, (lc, lc), 0)
    ss = lax.broadcasted_iota(jnp.int32, (lc, lc), 1)
    tril = jnp.where(ss <= tt, 1.0, 0.0).astype(BF16)
    triu = jnp.where(ss >= tt, 1.0, 0.0).astype(BF16)
    nq = lc // LANES
    td = lax.broadcasted_iota(jnp.int32, (LANES, LANES), 0)
    sd = lax.broadcasted_iota(jnp.int32, (LANES, LANES), 1)
    lower_d = td > sd
    upper_d = td < sd

    acs2_l, rowt_l, wts_l, eacs_l, edec_l, cbm_l, bm_l, cm_l = [], [], [], [], [], [], [], []
    for c in range(nchunk):
        rows = slice(c * lc, (c + 1) * lc)
        acs = jnp.where(fwd_lane, _sel_dot_left(tril, dta[rows]), _sel_dot_left(triu, dta[rows]))
        edge = jnp.where(fwd_lane, acs[lc - 1:lc, :], acs[0:1, :])
        acs2 = acs * LOG2E
        dtt = dt[rows].T
        rowt = acs2.T - jnp.log2(dtt)
        dgt = jnp.log2(dtt[0:nh, :] + dtt[nh:2 * nh, :])
        acs2_l.append(acs2)
        rowt_l.append((rowt, dgt))
        wts_l.append((dt[rows] * jnp.exp(edge - acs)).T)
        eacs_l.append(jnp.exp(acs))
        edec_l.append(jnp.exp(edge))
        cbm_g, bm_g, cm_g = [], [], []
        for g in range(SSD_GROUPS):
            bm = bc[rows, g * SSD_STATE:(g + 1) * SSD_STATE]
            cm = bc[rows, (SSD_GROUPS + g) * SSD_STATE:(SSD_GROUPS + g + 1) * SSD_STATE]
            bm_g.append(bm)
            cm_g.append(cm)
            cbm_g.append(_dot_nt(cm, bm))
        cbm_l.append(cbm_g)
        bm_l.append(bm_g)
        cm_l.append(cm_g)

    for p in range(HEAD_PAIRS):
        cols = slice(p * LANES, (p + 1) * LANES)
        g = p // (HEAD_PAIRS // SSD_GROUPS)
        hd = (2 * p, 2 * p + 1)
        x2b = x2_ref[e, :, cols]
        x2 = x2b.astype(F32)
        dsum2 = _pair_cols(first_half, dsum, hd[0], hd[1])
        if has_h0:
            hf = h0_ref[e, 0, p]
            hb = h0_ref[e, 1, p]
        else:
            hf = jnp.zeros((LANES, SSD_STATE), F32)
            hb = jnp.zeros((LANES, SSD_STATE), F32)
        ys = []
        stbs = []
        for c in range(nchunk):
            rows = slice(c * lc, (c + 1) * lc)
            x2c = x2[rows]
            x2cb = x2b[rows]
            acs2 = acs2_l[c]
            rowt, dgt = rowt_l[c]
            cbm = cbm_l[c][g]
            outs = []
            for h in hd:
                blocks = []
                for bi in range(nq):
                    ri = slice(bi * LANES, (bi + 1) * LANES)
                    cf = acs2[ri, h:h + 1]
                    cbk = acs2[ri, nh + h:nh + h + 1]
                    row_blocks = []
                    for bj in range(nq):
                        cj = slice(bj * LANES, (bj + 1) * LANES)
                        rf = rowt[h:h + 1, cj]
                        rb = rowt[nh + h:nh + h + 1, cj]
                        if bi > bj:
                            arg = cf - rf
                        elif bi < bj:
                            arg = cbk - rb
                        else:
                            arg = jnp.where(lower_d, cf - rf, jnp.where(upper_d, cbk - rb, dgt[h:h + 1, cj]))
                        row_blocks.append((cbm[ri, cj] * jnp.exp2(arg)).astype(BF16))
                    blocks.append(jnp.concatenate(row_blocks, axis=1))
                gm = jnp.concatenate(blocks, axis=0)
                outs.append(_dot(gm, x2cb))
            y = jnp.where(first_half, outs[0], outs[1]) + x2c * dsum2
            xt = x2c.T
            xwf = (xt * _pair_rows(wts_l[c], hd[0], hd[1], lc)).astype(BF16)
            xwb = (xt * _pair_rows(wts_l[c], nh + hd[0], nh + hd[1], lc)).astype(BF16)
            stf = _dot(xwf, bm_l[c][g])
            stbs.append(_dot(xwb, bm_l[c][g]))
            if has_h0 or c > 0:
                y = y + _dot_nt(cm_l[c][g], hf.astype(BF16)) * _pair_cols(first_half, eacs_l[c], hd[0], hd[1])
            hf = hf * _pair_scalars(edec_l[c], hd[0], hd[1]) + stf
            ys.append(y)
        for c in reversed(range(nchunk)):
            if has_h0 or c < nchunk - 1:
                ys[c] = ys[c] + (_dot_nt(cm_l[c][g], hb.astype(BF16))
                                 * _pair_cols(first_half, eacs_l[c], nh + hd[0], nh + hd[1]))
            hb = hb * _pair_scalars(edec_l[c], nh + hd[0], nh + hd[1]) + stbs[c]
        zg = zs_ref[e, :, cols].astype(F32)
        for c in range(nchunk):
            rows = slice(c * lc, (c + 1) * lc)
            yacc_ref[rows, cols] = ys[c] * zg[rows]
        if emit_state:
            st_ref[e, 0, p] = hf
            st_ref[e, 1, p] = hb

    y = yacc_ref[...]
    ms = jnp.mean(y * y, axis=-1, keepdims=True)
    ya_ref[e] = (y * lax.rsqrt(ms + EPS) * ng_ref[...]).astype(BF16)


def _ssd(zs3, x23, bc3, dt3, prm, norm_g, h0, *, emit_state, n_elems):
    b, s, _ = zs3.shape
    has_h0 = h0 is not None
    ne = n_elems
    in_specs = [
        pl.BlockSpec((ne, s, SSD_INNER), lambda i: (i, 0, 0)),
        pl.BlockSpec((ne, s, SSD_INNER), lambda i: (i, 0, 0)),
        pl.BlockSpec((ne, s, SSD_XBC - SSD_INNER), lambda i: (i, 0, 0)),
        pl.BlockSpec((ne, s, LANES), lambda i: (i, 0, 0)),
        pl.BlockSpec((8, LANES), lambda i: (0, 0)),
        pl.BlockSpec((1, SSD_INNER), lambda i: (0, 0)),
    ]
    args = [zs3, x23, bc3, dt3, prm, norm_g]
    state_block = (ne, 2, HEAD_PAIRS, LANES, SSD_STATE)
    if has_h0:
        in_specs.append(pl.BlockSpec(state_block, lambda i: (i, 0, 0, 0, 0)))
        args.append(h0)
    out_specs = [pl.BlockSpec((ne, s, SSD_INNER), lambda i: (i, 0, 0))]
    out_shape = [jax.ShapeDtypeStruct((b, s, SSD_INNER), BF16)]
    if emit_state:
        out_specs.append(pl.BlockSpec(state_block, lambda i: (i, 0, 0, 0, 0)))
        out_shape.append(jax.ShapeDtypeStruct((b, 2, HEAD_PAIRS, LANES, SSD_STATE), F32))
    kern = functools.partial(_ssd_kernel, n_elems=ne, seq_len=s, has_h0=has_h0, emit_state=emit_state)
    return pl.pallas_call(
        kern,
        grid=(b // ne,),
        in_specs=in_specs,
        out_specs=out_specs,
        out_shape=out_shape,
        scratch_shapes=[pltpu.VMEM((s, SSD_INNER), F32)],
        compiler_params=_cparams(("arbitrary",)),
        name="ssd",
    )(*args)


def _pair_masks():
    lane = lax.broadcasted_iota(jnp.int32, (1, LANES), 1)
    return lane < NA_HEAD_DIM


def _qkv_ctx_attn_kernel(x_ref, sh_ref, sc_ref, g_ref, w_ref, o_ref, nk_ref, nv_ref, qkv_ref,
                         *, tm, seq_len, row_base, rows_per_group):
    r = _mod_row(pl.program_id(0), tm, row_base, rows_per_group)
    sh = sh_ref[0, pl.ds(r, 1), :]
    sc = sc_ref[0, pl.ds(r, 1), :]
    h = _norm_mod(x_ref[...], g_ref[...], sh, sc).astype(BF16)
    d = x_ref.shape[1]
    tn = 512
    s = seq_len
    for part, out_ref in ((1, nk_ref), (2, nv_ref)):
        for j in range(d // tn):
            c0 = part * d + j * tn
            qkv_ref[:, c0:c0 + tn] = _dot(h, w_ref[:, c0:c0 + tn])
        for b in range(tm // s):
            rows = slice(b * s, (b + 1) * s)
            out_ref[b, 0] = qkv_ref[rows, part * d:(part + 1) * d].reshape(s, NA_HEADS, NA_HEAD_DIM)
    for j in range(d // tn):
        qkv_ref[:, j * tn:(j + 1) * tn] = _dot(h, w_ref[:, j * tn:(j + 1) * tn])
    first_half = _pair_masks()
    scale = NA_HEAD_DIM ** -0.5
    for b in range(tm // s):
        rows = slice(b * s, (b + 1) * s)
        for p in range(HEAD_PAIRS):
            cols = slice(p * LANES, (p + 1) * LANES)
            q2 = qkv_ref[rows, cols] * scale
            kb = qkv_ref[rows, d + p * LANES:d + (p + 1) * LANES].astype(BF16)
            vb = qkv_ref[rows, 2 * d + p * LANES:2 * d + (p + 1) * LANES].astype(BF16)
            qs = jnp.concatenate([jnp.where(first_half, q2, 0.0), jnp.where(first_half, 0.0, q2)], axis=0).astype(BF16)
            sco = _dot_nt(qs, kb)
            e = jnp.exp(sco - jnp.max(sco, axis=-1, keepdims=True))
            pv = _dot(e.astype(BF16), vb) / jnp.sum(e, axis=-1, keepdims=True)
            o_ref[rows, cols] = jnp.where(first_half, pv[:s], pv[s:]).astype(BF16)


def _qkv_ctx_attn(x, mod, layer, g, w, *, seq_len, row_base, rows_per_group, tm=512):
    m, d = x.shape
    nb = tm // seq_len
    cache_block = (nb, 1, seq_len, NA_HEADS, NA_HEAD_DIM)
    cache_shape = jax.ShapeDtypeStruct((m // seq_len, 1, seq_len, NA_HEADS, NA_HEAD_DIM), F32)
    kern = functools.partial(_qkv_ctx_attn_kernel, tm=tm, seq_len=seq_len, row_base=row_base,
                             rows_per_group=rows_per_group)
    return pl.pallas_call(
        kern,
        grid=(m // tm,),
        in_specs=[
            pl.BlockSpec((tm, d), lambda i: (i, 0)),
            pl.BlockSpec((1, 8, d), lambda i: (layer, 0, 0)),
            pl.BlockSpec((1, 8, d), lambda i: (layer, 0, 1)),
            pl.BlockSpec((1, d), lambda i: (0, 0)),
            _RESIDENT,
        ],
        out_specs=[
            pl.BlockSpec((tm, d), lambda i: (i, 0)),
            pl.BlockSpec(cache_block, lambda i: (i, 0, 0, 0, 0)),
            pl.BlockSpec(cache_block, lambda i: (i, 0, 0, 0, 0)),
        ],
        out_shape=[jax.ShapeDtypeStruct((m, d), BF16), cache_shape, cache_shape],
        scratch_shapes=[pltpu.VMEM((tm, 3 * d), F32)],
        compiler_params=_cparams(("arbitrary",)),
        name="qkv_ctx_attention",
    )(x, mod, mod, g.reshape(1, d), w)


def _lat_attn_kernel(q_ref, k_ref, v_ref, kc_ref, vc_ref, rpb_ref, o_ref, bias_ref, *, rows):
    b = pl.program_id(1)
    first_half = _pair_masks()
    kh = min(NA_KH, rows)
    win = kh * GRID_W
    neg_inf = -jnp.inf

    def win_start(r):
        return min(max(r - kh // 2, 0), rows - kh)

    @pl.when(b == 0)
    def _():
        qc = lax.broadcasted_iota(jnp.int32, (GRID_W, LANES), 0)
        kc = lax.broadcasted_iota(jnp.int32, (GRID_W, LANES), 1)
        cs = jnp.clip(qc - NA_KW // 2, 0, GRID_W - NA_KW)
        col_ok = (kc >= cs) & (kc < cs + NA_KW)
        for hh in range(2):
            for dr in range(2 * NA_KH - 1):
                v = jnp.broadcast_to(rpb_ref[hh, dr:dr + 1, :], (GRID_W, LANES))
                t = pltpu.roll(v, LANES - (NA_KW - 1), 1, stride=1, stride_axis=0)
                tile = jnp.where(col_ok, t, neg_inf)[:, :GRID_W]
                for r in range(rows):
                    i = dr - (NA_KH - 1) + r - win_start(r)
                    if 0 <= i < kh:
                        bias_ref[r, hh * GRID_W:(hh + 1) * GRID_W, i * GRID_W:(i + 1) * GRID_W] = tile

    scale = NA_HEAD_DIM ** -0.5
    kb = k_ref[...].astype(BF16)
    vb = v_ref[...].astype(BF16)
    kcb = kc_ref[...].astype(BF16)
    vcb = vc_ref[...].astype(BF16)
    group = 4
    for r0 in range(0, rows, group):
        rr = range(r0, r0 + group)
        qrows = [slice(r * GRID_W, (r + 1) * GRID_W) for r in rr]
        krows = [slice(win_start(r) * GRID_W, win_start(r) * GRID_W + win) for r in rr]
        qs = []
        for qr in qrows:
            q2 = q_ref[qr, :] * scale
            qs.append(jnp.concatenate([jnp.where(first_half, q2, 0.0), jnp.where(first_half, 0.0, q2)],
                                      axis=0).astype(BF16))
        s_loc = [_dot_nt(qi, kb[kr]) + bias_ref[r] for qi, kr, r in zip(qs, krows, rr)]
        s_ctx = [_dot_nt(qi, kcb) for qi in qs]
        mx = [jnp.maximum(jnp.max(sl, axis=-1, keepdims=True), jnp.max(sc, axis=-1, keepdims=True))
              for sl, sc in zip(s_loc, s_ctx)]
        e_loc = [jnp.exp(sl - m) for sl, m in zip(s_loc, mx)]
        e_ctx = [jnp.exp(sc - m) for sc, m in zip(s_ctx, mx)]
        den = [jnp.sum(el, axis=-1, keepdims=True) + jnp.sum(ec, axis=-1, keepdims=True) for el, ec in zip(e_loc, e_ctx)]
        pv = [(_dot(el.astype(BF16), vb[kr]) + _dot(ec.astype(BF16), vcb)) / dn
              for el, ec, kr, dn in zip(e_loc, e_ctx, krows, den)]
        for qr, o in zip(qrows, pv):
            o_ref[qr, :] = jnp.where(first_half, o[:GRID_W], o[GRID_W:]).astype(BF16)


def _lat_attn(q, k, v, kc, vc, rpb_pad, *, n_batch, n_tok, n_ctx):
    d = q.shape[1]
    rows = n_tok // GRID_W
    kh = min(NA_KH, rows)
    tok_spec = pl.BlockSpec((n_tok, LANES), lambda p, b: (b, p))
    ctx_spec = pl.BlockSpec((n_ctx, LANES), lambda p, b: (b, p))
    kern = functools.partial(_lat_attn_kernel, rows=rows)
    return pl.pallas_call(
        kern,
        grid=(HEAD_PAIRS, n_batch),
        in_specs=[tok_spec, tok_spec, tok_spec, ctx_spec, ctx_spec,
                  pl.BlockSpec((2, 2 * NA_KH, LANES), lambda p, b: (p, 0, 0))],
        out_specs=tok_spec,
        out_shape=jax.ShapeDtypeStruct((n_batch * n_tok, d), BF16),
        scratch_shapes=[pltpu.VMEM((rows, 2 * GRID_W, kh * GRID_W), F32)],
        compiler_params=_cparams(("arbitrary", "arbitrary")),
        name="latent_attention",
    )(q, k, v, kc, vc, rpb_pad)


def _pad_lanes(v):
    return jnp.pad(v, (0, LANES - v.shape[0]))


def _run_stream(x3, mod, row_base, W, *, is_ctx, state_ssd=None, cache_k=None, cache_v=None):
    b, s, d = x3.shape
    m = b * s
    x = x3.reshape(m, d)
    rpg = m if is_ctx else s
    kw = dict(row_base=row_base, rows_per_group=rpg)

    zs, x2, bca, yb, dtr = _in_proj(x, mod, W['norm_mix_g'][0], W['w_in'], W['w_in_tail'], W['ssd_conv_w'], W['ssd_conv_b'],
                                    W['sc_conv_w_rot'], seq_len=s, tm=max(s, 512), **kw)
    h0 = None
    if not is_ctx:
        h0 = state_ssd[:, 0].reshape(b, 2, HEAD_PAIRS, LANES, SSD_STATE)
    res = _ssd(zs.reshape(b, s, -1), x2.reshape(b, s, -1), bca.reshape(b, s, -1), dtr.reshape(b, s, LANES),
               W['ssd_prm'], W['ssd_norm_g'], h0, emit_state=is_ctx, n_elems=2 if is_ctx else 1)
    ya = res[0].reshape(m, SSD_INNER)
    x = _mix_ffn(x, mod, 0, [ya, yb], W['mix0_w_out'], W['norm_ffn_g'][0], *W['ffn'], W['final_norm_g'], seq_len=s,
                 final_norm=False, **kw)

    if is_ctx:
        o, new_k, new_v = _qkv_ctx_attn(x, mod, 1, W['norm_mix_g'][1], W['na_w_qkv'], seq_len=s, **kw)
    else:
        q, k, v = _nm_matmul(x, mod, 1, 0, 1, W['norm_mix_g'][1], W['na_w_qkv'], n_groups=3, group_width=D_MODEL, **kw)
        n_ctx = cache_k.shape[2]
        kc = cache_k[:, 0].reshape(b * n_ctx, d)
        vc = cache_v[:, 0].reshape(b * n_ctx, d)
        o = _lat_attn(q, k, v, kc, vc, W['rpb_pad'], n_batch=b, n_tok=s, n_ctx=n_ctx)
    x = _mix_ffn(x, mod, 1, [o], W['na_w_out'], W['norm_ffn_g'][1], *W['ffn'], W['final_norm_g'], seq_len=s,
                 final_norm=True, **kw)
    y = x.reshape(b, s, d)
    if is_ctx:
        new_state = res[1].reshape(b, 1, 2, SSD_HEADS, SSD_HEADDIM, SSD_STATE)
        return y, new_state, new_k, new_v
    return y


def kernel(x_prompt, x_sample, state_ssd, cache_k, cache_v, c, c_ctx, ada_w, ada_b, norm_mix_g, norm_ffn_g, ssd_w_in,
           ssd_conv_w, ssd_conv_b, ssd_dt_bias, ssd_a_log, ssd_d, ssd_norm_g, sc_conv_w, mix0_w_out, na_w_qkv, na_rpb,
           na_w_out, ffn_w_gate, ffn_w_up, ffn_conv_w, ffn_w_down, final_norm_g):
    n_lat = x_sample.shape[0]
    cvec = jnp.concatenate([c_ctx[None, :], c, jnp.zeros((8 - 1 - n_lat, D_MODEL), F32)], axis=0)
    mod = _ada(cvec, ada_w, ada_b)

    w_in = ssd_w_in[0].astype(BF16)
    w_in_tail = jnp.pad(ssd_w_in[0][:, TAIL_COL0:], ((0, 0), (0, LANES - MIX_SHIFT))).astype(BF16)
    w_out0 = mix0_w_out[0]
    w_out0 = jnp.concatenate([w_out0[:SSD_INNER], jnp.roll(w_out0[SSD_INNER:], MIX_SHIFT, axis=0)], axis=0).astype(BF16)
    prm =jnp.stack([_pad_lanes(ssd_dt_bias[0].reshape(-1)), _pad_lanes(ssd_a_log[0].reshape(-1)),
                     _pad_lanes(ssd_d[0, 0]), _pad_lanes(ssd_d[0, 1])] + [jnp.zeros((LANES,), F32)] * 4, axis=0)
    rpb = na_rpb[0]
    rpb_pad = jnp.pad(rpb, ((0, 0), (0, 1), (0, LANES - rpb.shape[2])))
    W = {
        'norm_mix_g': norm_mix_g, 'norm_ffn_g': norm_ffn_g, 'final_norm_g': final_norm_g,
        'w_in': w_in, 'w_in_tail': w_in_tail,
        'ssd_conv_w': ssd_conv_w[0], 'ssd_conv_b': ssd_conv_b[0].reshape(1, SSD_XBC), 'ssd_prm': prm,
        'ssd_norm_g': ssd_norm_g[0].reshape(1, SSD_INNER), 'sc_conv_w_rot': jnp.roll(sc_conv_w[0], MIX_SHIFT, axis=1),
        'mix0_w_out': w_out0,
        'na_w_qkv': na_w_qkv[0].astype(BF16), 'rpb_pad': rpb_pad, 'na_w_out': na_w_out[0].astype(BF16),
        'ffn': (ffn_w_gate.astype(BF16), ffn_w_up.astype(BF16), ffn_conv_w, ffn_w_down.astype(BF16)),
    }
    y_prompt, new_state, new_k, new_v = _run_stream(x_prompt, mod, 0, W, is_ctx=True)
    y_sample = _run_stream(x_sample, mod, 1, W, is_ctx=False, state_ssd=state_ssd, cache_k=cache_k, cache_v=cache_v)
    return (y_prompt, y_sample, new_state, new_k, new_v)
```

```python
import functools

import jax
import jax.numpy as jnp
from jax import lax
from jax.experimental import pallas as pl
from jax.experimental.pallas import tpu as pltpu

F32 = jnp.float32
BF16 = jnp.bfloat16

D_MODEL = 1024
EPS = 1e-6
GRID_W = 64
SSD_HEADDIM = 64
SSD_HEADS = 16
SSD_STATE = 128
SSD_GROUPS = 2
SSD_INNER = 1024
SSD_XBC = 1536
SSD_LC = 256
SC_WIDTH = 1024
NA_HEADS = 16
NA_HEAD_DIM = 64
NA_KH = 8
NA_KW = 16

LOG2E = 1.4426950408889634
LANES = 128
SUBLANES = 8
HEAD_PAIRS = 8

VMEM_LIMIT = 56 * 1024 * 1024

_RESIDENT = pl.BlockSpec(memory_space=pltpu.VMEM)


def _cparams(sem):
    return pltpu.CompilerParams(dimension_semantics=sem, vmem_limit_bytes=VMEM_LIMIT)


def _silu(x):
    return x * jax.nn.sigmoid(x)


def _dot(a, b):
    return jnp.dot(a, b, preferred_element_type=F32)


def _dot_nt(a, b):
    return lax.dot_general(a, b, (((1,), (1,)), ((), ())), preferred_element_type=F32)


def _split3(v):
    hi = v.astype(BF16)
    r1 = v - hi.astype(F32)
    mid = r1.astype(BF16)
    lo = (r1 - mid.astype(F32)).astype(BF16)
    return hi, mid, lo


def _sel_dot_left(sel, v):
    hi, mid, lo = _split3(v)
    return _dot(sel, hi) + _dot(sel, mid) + _dot(sel, lo)


def _norm_mod(x, g, shift, scale):
    ms = jnp.mean(x * x, axis=-1, keepdims=True)
    y = x * lax.rsqrt(ms + EPS) * g
    return y * (1.0 + scale) + shift


def _mod_row(i, tm, row_base, rows_per_group):
    return row_base + (i * tm) // rows_per_group


def _ada_kernel(c_ref, w_ref, b_ref, o_ref):
    s = _silu(c_ref[...]).astype(BF16)
    o_ref[0] = _dot(s, w_ref[0].astype(BF16)) + b_ref[0]


def _ada(cvec8, ada_w, ada_b):
    depth, d, n = ada_w.shape
    tn = 3072
    return pl.pallas_call(
        _ada_kernel,
        grid=(depth, n // tn),
        in_specs=[
            pl.BlockSpec((8, d), lambda l, j: (0, 0)),
            pl.BlockSpec((1, d, tn), lambda l, j: (l, 0, j)),
            pl.BlockSpec((1, 1, tn), lambda l, j: (l, 0, j)),
        ],
        out_specs=pl.BlockSpec((1, 8, tn), lambda l, j: (l, 0, j)),
        out_shape=jax.ShapeDtypeStruct((depth, 8, n), F32),
        compiler_params=_cparams(("arbitrary", "arbitrary")),
        name="ada_mod",
    )(cvec8, ada_w, ada_b.reshape(depth, 1, n))


def _nm_matmul_kernel(*refs, n_groups, group_width, tm, tn, row_base, rows_per_group):
    x_ref, sh_ref, sc_ref, g_ref, w_ref = refs[:5]
    o_refs = refs[5:5 + n_groups]
    r = _mod_row(pl.program_id(0), tm, row_base, rows_per_group)
    sh = sh_ref[0, pl.ds(r, 1), :]
    sc = sc_ref[0, pl.ds(r, 1), :]
    h = _norm_mod(x_ref[...], g_ref[...], sh, sc).astype(BF16)
    for gi, o_ref in enumerate(o_refs):
        for j in range(group_width // tn):
            o_ref[:, j * tn:(j + 1) * tn] = _dot(h, w_ref[:, gi * group_width + j * tn:gi * group_width + (j + 1) * tn])


def _nm_matmul(x, mod, layer, shift_idx, scale_idx, g, w, *, n_groups, group_width, row_base, rows_per_group,
               tm=512, tn=512):
    m, d = x.shape
    kern = functools.partial(_nm_matmul_kernel, n_groups=n_groups, group_width=group_width, tm=tm, tn=tn,
                             row_base=row_base, rows_per_group=rows_per_group)
    return pl.pallas_call(
        kern,
        grid=(m // tm,),
        in_specs=[
            pl.BlockSpec((tm, d), lambda i: (i, 0)),
            pl.BlockSpec((1, 8, d), lambda i: (layer, 0, shift_idx)),
            pl.BlockSpec((1, 8, d), lambda i: (layer, 0, scale_idx)),
            pl.BlockSpec((1, d), lambda i: (0, 0)),
            _RESIDENT,
        ],
        out_specs=[pl.BlockSpec((tm, group_width), lambda i: (i, 0)) for _ in range(n_groups)],
        out_shape=[jax.ShapeDtypeStruct((m, group_width), F32) for _ in range(n_groups)],
        compiler_params=_cparams(("arbitrary",)),
        name="norm_mod_matmul",
    )(x, mod, mod, g.reshape(1, d), w)


def _conv_scratch(rows, cols, seq_len):
    return pltpu.VMEM((cols // LANES, rows + SUBLANES * (rows // seq_len + 3), LANES), F32)


def _dwconv3(v, w, seq_len, cv_ref, row0=0):
    rows, ch = v.shape
    dyn0 = pl.program_id(0) * 0
    cuts = [0] + [t for t in range(1, rows) if (row0 + t) % seq_len == 0] + [rows]
    segs = list(zip(cuts[:-1], cuts[1:]))
    zeros = jnp.zeros((SUBLANES, LANES), F32)
    slabs = []
    for j in range(ch // LANES):
        cl = slice(j * LANES, (j + 1) * LANES)
        base = SUBLANES
        cv_ref[j, 0:SUBLANES, :] = zeros
        bases = []
        for a, b in segs:
            cv_ref[j, base:base + (b - a), :] = v[a:b, cl]
            cv_ref[j, base + (b - a):base + (b - a) + SUBLANES, :] = zeros
            bases.append(base)
            base += (b - a) + SUBLANES
        parts = []
        for (a, b), bs in zip(segs, bases):
            prev = cv_ref[j, pl.ds(bs - 1 + dyn0, b - a), :]
            nxt = cv_ref[j, pl.ds(bs + 1 + dyn0, b - a), :]
            parts.append(prev * w[0:1, cl] + v[a:b, cl] * w[1:2, cl] + nxt * w[2:3, cl])
        slabs.append(jnp.concatenate(parts, axis=0) if len(parts) > 1 else parts[0])
    return jnp.concatenate(slabs, axis=1) if len(slabs) > 1 else slabs[0]


MIX_SHIFT = 2 * SSD_HEADS
DT_COL0 = SSD_INNER + SSD_XBC
TAIL_COL0 = DT_COL0 + 3 * SC_WIDTH
IN_PROJ_TN = 256


def _in_proj_kernel(x_ref, sh_ref, sc_ref, g_ref, w_ref, wt_ref, cw_ref, cb_ref, scw_ref, zs_ref, x2_ref, bc_ref, yb_ref,
                    dt_ref, cv_ref,
                    *, tm, row_base, rows_per_group, seq_len):
    r = _mod_row(pl.program_id(0), tm, row_base, rows_per_group)
    sh = sh_ref[0, pl.ds(r, 1), :]
    sc = sc_ref[0, pl.ds(r, 1), :]
    h = _norm_mod(x_ref[...], g_ref[...], sh, sc).astype(BF16)
    tn = IN_PROJ_TN

    def proj(c0, width=tn):
        return _dot(h, w_ref[:, c0:c0 + width])

    for j in range(SSD_INNER // tn):
        cols = slice(j * tn, (j + 1) * tn)
        zs_ref[:, cols] = _silu(proj(j * tn)).astype(BF16)
        xc = _dwconv3(proj(SSD_INNER + j * tn), cw_ref[:, cols], seq_len, cv_ref) + cb_ref[:, cols]
        x2_ref[:, cols] = _silu(xc).astype(BF16)
    for j in range((SSD_XBC - SSD_INNER) // tn):
        cols = slice(SSD_INNER + j * tn, SSD_INNER + (j + 1) * tn)
        bcc = _dwconv3(proj(SSD_INNER + cols.start), cw_ref[:, cols], seq_len, cv_ref) + cb_ref[:, cols]
        bc_ref[:, j * tn:(j + 1) * tn] = _silu(bcc).astype(BF16)
    dt_ref[...] = proj(DT_COL0, LANES)
    q3 = _dot(h, wt_ref[...])
    tail = lax.broadcasted_iota(jnp.int32, (1, LANES), 1) < MIX_SHIFT
    for j in range(SC_WIDTH // tn):
        c0 = DT_COL0 + j * tn
        cols = slice(j * tn, (j + 1) * tn)
        p0, p1, p2 = proj(c0), proj(c0 + SC_WIDTH), proj(c0 + 2 * SC_WIDTH)
        y = p0 * _dwconv3(p1 * p2, scw_ref[:, cols], seq_len, cv_ref)
        if j == 0:
            yt = p1[:, :LANES] * _dwconv3(p2[:, :LANES] * q3, scw_ref[:, :LANES], seq_len, cv_ref)
            y = jnp.concatenate([jnp.where(tail, yt, y[:, :LANES]), y[:, LANES:]], axis=1)
        yb_ref[:, cols] = y.astype(BF16)


def _in_proj(x, mod, g, w, w_tail, conv_w, conv_b, sc_conv_w_rot, *, seq_len, row_base, rows_per_group, tm):
    m, d = x.shape
    kern = functools.partial(_in_proj_kernel, tm=tm, row_base=row_base, rows_per_group=rows_per_group, seq_len=seq_len)
    widths = (SSD_INNER, SSD_INNER, SSD_XBC - SSD_INNER, SC_WIDTH, LANES)
    dtypes = (BF16, BF16, BF16, BF16, F32)
    return pl.pallas_call(
        kern,
        grid=(m // tm,),
        in_specs=[
            pl.BlockSpec((tm, d), lambda i: (i, 0)),
            pl.BlockSpec((1, 8, d), lambda i: (0, 0, 0)),
            pl.BlockSpec((1, 8, d), lambda i: (0, 0, 1)),
            pl.BlockSpec((1, d), lambda i: (0, 0)),
            _RESIDENT,
            _RESIDENT,
            pl.BlockSpec((3, SSD_XBC), lambda i: (0, 0)),
            pl.BlockSpec((1, SSD_XBC), lambda i: (0, 0)),
            pl.BlockSpec((3, SC_WIDTH), lambda i: (0, 0)),
        ],
        out_specs=[pl.BlockSpec((tm, wd), lambda i: (i, 0)) for wd in widths],
        out_shape=[jax.ShapeDtypeStruct((m, wd), dt) for wd, dt in zip(widths, dtypes)],
        scratch_shapes=[_conv_scratch(tm, IN_PROJ_TN, seq_len)],
        compiler_params=_cparams(("arbitrary",)),
        name="in_proj",
    )(x, mod, mod, g.reshape(1, d), w, w_tail, conv_w, conv_b, sc_conv_w_rot)


def _ffn_kernel(*refs, n_parts, tm, tf, row_base, rows_per_group, seq_len, final_norm):
    x_ref, gate1_ref, sh_ref, sc_ref, gate_ref, g_ref, wo_ref = refs[:7]
    a_refs = refs[7:7 + n_parts]
    wg_ref, wu_ref, cw_ref, wd_ref, fg_ref, o_ref, h_ref, act_ref, cv_ref = refs[7 + n_parts:]
    r = _mod_row(pl.program_id(0), tm, row_base, rows_per_group)
    sh = sh_ref[0, pl.ds(r, 1), :]
    sc = sc_ref[0, pl.ds(r, 1), :]
    gate = gate_ref[0, pl.ds(r, 1), :]
    kp = wo_ref.shape[0] // n_parts
    sub = tm // 4
    for q in range(tm // sub):
        rq = slice(q * sub, (q + 1) * sub)
        mix = _dot(a_refs[0][rq, :], wo_ref[0:kp, :])
        for k in range(1, n_parts):
            mix = mix + _dot(a_refs[k][rq, :], wo_ref[k * kp:(k + 1) * kp, :])
        x1 = x_ref[rq, :] + gate1_ref[0, pl.ds(r, 1), :] * mix
        o_ref[rq, :] = x1
        h_ref[rq, :] = _norm_mod(x1, g_ref[...], sh, sc).astype(BF16)
    dff = wg_ref.shape[1]
    half = tm // 2
    win = half + SUBLANES
    for hb in range(2):
        w0 = hb * (half - SUBLANES)
        v0 = hb * SUBLANES
        rows = slice(hb * half, (hb + 1) * half)
        hw = h_ref[w0:w0 + win, :]
        hv = h_ref[rows, :]
        for fc in range(dff // tf):
            cols = slice(fc * tf, (fc + 1) * tf)
            gpre = _dwconv3(_dot(hw, wg_ref[:, cols]), cw_ref[:, cols], seq_len, cv_ref, row0=w0)[v0:v0 + half]
            up = _dot(hv, wu_ref[:, cols])
            act_ref[:, cols] = (jax.nn.gelu(gpre, approximate=True) * up).astype(BF16)
        y = o_ref[rows, :] + gate * _dot(act_ref[...], wd_ref[...])
        if final_norm:
            ms = jnp.mean(y * y, axis=-1, keepdims=True)
            y = y * lax.rsqrt(ms + EPS) * fg_ref[...]
        o_ref[rows, :] = y


def _layer_weight(shape):
    return lambda layer: pl.BlockSpec((None,) + shape, lambda i: (layer,) + (0,) * len(shape),
                                      pipeline_mode=pl.Buffered(1))


def _mix_ffn(x, mod, layer, parts, wo, g, wg, wu, cw, wd, final_g, *, row_base, rows_per_group, seq_len, final_norm,
             tm=1024, tf=256):
    m, d = x.shape
    dff = wg.shape[2]
    n_parts = len(parts)
    kern = functools.partial(_ffn_kernel, n_parts=n_parts, tm=tm, tf=tf, row_base=row_base,
                             rows_per_group=rows_per_group, seq_len=seq_len, final_norm=final_norm)
    mod_spec = lambda k: pl.BlockSpec((1, 8, d), lambda i: (layer, 0, k))
    in_specs = [pl.BlockSpec((tm, d), lambda i: (i, 0)), mod_spec(2), mod_spec(3), mod_spec(4), mod_spec(5),
                pl.BlockSpec((1, d), lambda i: (0, 0)), _RESIDENT]
    in_specs += [pl.BlockSpec((tm, a.shape[1]), lambda i: (i, 0)) for a in parts]
    in_specs += [_layer_weight((d, dff))(layer), _layer_weight((d, dff))(layer), _layer_weight((3, dff))(layer),
                 _layer_weight((dff, d))(layer), pl.BlockSpec((1, d), lambda i: (0, 0))]
    return pl.pallas_call(
        kern,
        grid=(m // tm,),
        in_specs=in_specs,
        out_specs=pl.BlockSpec((tm, d), lambda i: (i, 0)),
        out_shape=jax.ShapeDtypeStruct((m, d), F32),
        scratch_shapes=[pltpu.VMEM((tm, d), BF16), pltpu.VMEM((tm // 2, dff), BF16),
                        _conv_scratch(tm // 2 + SUBLANES, tf, seq_len)],
        compiler_params=_cparams(("arbitrary",)),
        name="mix_ffn",
    )(x, mod, mod, mod, mod, g.reshape(1, d), wo, *parts, wg, wu, cw, wd, final_g.reshape(1, d))


def _pair_cols(first_half, arr, h0, h1):
    return jnp.where(first_half, arr[:, h0:h0 + 1], arr[:, h1:h1 + 1])


def _pair_rows(arr_t, h0, h1, width):
    return jnp.concatenate([jnp.broadcast_to(arr_t[h0:h0 + 1, :], (SSD_HEADDIM, width)),
                            jnp.broadcast_to(arr_t[h1:h1 + 1, :], (SSD_HEADDIM, width))], axis=0)


def _pair_scalars(row, h0, h1):
    return jnp.concatenate([jnp.broadcast_to(row[:, h0:h0 + 1], (SSD_HEADDIM, SSD_STATE)),
                            jnp.broadcast_to(row[:, h1:h1 + 1], (SSD_HEADDIM, SSD_STATE))], axis=0)


def _ssd_kernel(*refs, n_elems, **kw):
    for e in range(n_elems):
        _ssd_element(e, *refs, **kw)


def _ssd_element(e, *refs, seq_len, has_h0, emit_state):
    zs_ref, x2_ref, bc_ref, dt_ref, prm_ref, ng_ref = refs[:6]
    pos = 6
    if has_h0:
        h0_ref = refs[pos]
        pos += 1
    ya_ref = refs[pos]
    pos += 1
    if emit_state:
        st_ref = refs[pos]
        pos += 1
    yacc_ref = refs[pos]

    lc = SSD_LC
    nchunk = seq_len // lc
    nh = SSD_HEADS
    lane = lax.broadcasted_iota(jnp.int32, (1, LANES), 1)
    first_half = lane < SSD_HEADDIM
    fwd_lane = lane < nh

    bc = bc_ref[e]
    valid = lane < 2 * nh
    dt = jnp.where(valid, jax.nn.softplus(dt_ref[e] + prm_ref[0:1, :]), 0.0)
    a = -jnp.exp(prm_ref[1:2, :])
    dta = jnp.where(valid, dt * a, 0.0)
    dsum = prm_ref[2:3, :] + prm_ref[3:4, :]
    tt = lax.broadcasted_iota(jnp.int32, (lc, lc), 0)
    ss = lax.broadcasted_iota(jnp.int32, (lc, lc), 1)
    tril = jnp.where(ss <= tt, 1.0, 0.0).astype(BF16)
    triu = jnp.where(ss >= tt, 1.0, 0.0).astype(BF16)
    nq = lc // LANES
    td = lax.broadcasted_iota(jnp.int32, (LANES, LANES), 0)
    sd = lax.broadcasted_iota(jnp.int32, (LANES, LANES), 1)
    lower_d = td > sd
    upper_d = td < sd

    acs2_l, rowt_l, wts_l, eacs_l, edec_l, cbm_l, bm_l, cm_l = [], [], [], [], [], [], [], []
    for c in range(nchunk):
        rows = slice(c * lc, (c + 1) * lc)
        acs = jnp.where(fwd_lane, _sel_dot_left(tril, dta[rows]), _sel_dot_left(triu, dta[rows]))
        edge = jnp.where(fwd_lane, acs[lc - 1:lc, :], acs[0:1, :])
        acs2 = acs * LOG2E
        dtt = dt[rows].T
        rowt = acs2.T - jnp.log2(dtt)
        dgt = jnp.log2(dtt[0:nh, :] + dtt[nh:2 * nh, :])
        acs2_l.append(acs2)
        rowt_l.append((rowt, dgt))
        wts_l.append((dt[rows] * jnp.exp(edge - acs)).T)
        eacs_l.append(jnp.exp(acs))
        edec_l.append(jnp.exp(edge))
        cbm_g, bm_g, cm_g = [], [], []
        for g in range(SSD_GROUPS):
            bm = bc[rows, g * SSD_STATE:(g + 1) * SSD_STATE]
            cm = bc[rows, (SSD_GROUPS + g) * SSD_STATE:(SSD_GROUPS + g + 1) * SSD_STATE]
            bm_g.append(bm)
            cm_g.append(cm)
            cbm_g.append(_dot_nt(cm, bm))
        cbm_l.append(cbm_g)
        bm_l.append(bm_g)
        cm_l.append(cm_g)

    for p in range(HEAD_PAIRS):
        cols = slice(p * LANES, (p + 1) * LANES)
        g = p // (HEAD_PAIRS // SSD_GROUPS)
        hd = (2 * p, 2 * p + 1)
        x2b = x2_ref[e, :, cols]
        x2 = x2b.astype(F32)
        dsum2 = _pair_cols(first_half, dsum, hd[0], hd[1])
        if has_h0:
            hf = h0_ref[e, 0, p]
            hb = h0_ref[e, 1, p]
        else:
            hf = jnp.zeros((LANES, SSD_STATE), F32)
            hb = jnp.zeros((LANES, SSD_STATE), F32)
        ys = []
        stbs = []
        for c in range(nchunk):
            rows = slice(c * lc, (c + 1) * lc)
            x2c = x2[rows]
            x2cb = x2b[rows]
            acs2 = acs2_l[c]
            rowt, dgt = rowt_l[c]
            cbm = cbm_l[c][g]
            outs = []
            for h in hd:
                blocks = []
                for bi in range(nq):
                    ri = slice(bi * LANES, (bi + 1) * LANES)
                    cf = acs2[ri, h:h + 1]
                    cbk = acs2[ri, nh + h:nh + h + 1]
                    row_blocks = []
                    for bj in range(nq):
                        cj = slice(bj * LANES, (bj + 1) * LANES)
                        rf = rowt[h:h + 1, cj]
                        rb = rowt[nh + h:nh + h + 1, cj]
                        if bi > bj:
                            arg = cf - rf
                        elif bi < bj:
                            arg = cbk - rb
                        else:
                            arg = jnp.where(lower_d, cf - rf, jnp.where(upper_d, cbk - rb, dgt[h:h + 1, cj]))
                        row_blocks.append((cbm[ri, cj] * jnp.exp2(arg)).astype(BF16))
                    blocks.append(jnp.concatenate(row_blocks, axis=1))
                gm = jnp.concatenate(blocks, axis=0)
                outs.append(_dot(gm, x2cb))
            y = jnp.where(first_half, outs[0], outs[1]) + x2c * dsum2
            xt = x2c.T
            xwf = (xt * _pair_rows(wts_l[c], hd[0], hd[1], lc)).astype(BF16)
            xwb = (xt * _pair_rows(wts_l[c], nh + hd[0], nh + hd[1], lc)).astype(BF16)
            stf = _dot(xwf, bm_l[c][g])
            stbs.append(_dot(xwb, bm_l[c][g]))
            if has_h0 or c > 0:
                y = y + _dot_nt(cm_l[c][g], hf.astype(BF16)) * _pair_cols(first_half, eacs_l[c], hd[0], hd[1])
            hf = hf * _pair_scalars(edec_l[c], hd[0], hd[1]) + stf
            ys.append(y)
        for c in reversed(range(nchunk)):
            if has_h0 or c < nchunk - 1:
                ys[c] = ys[c] + (_dot_nt(cm_l[c][g], hb.astype(BF16))
                                 * _pair_cols(first_half, eacs_l[c], nh + hd[0], nh + hd[1]))
            hb = hb * _pair_scalars(edec_l[c], nh + hd[0], nh + hd[1]) + stbs[c]
        zg = zs_ref[e, :, cols].astype(F32)
        for c in range(nchunk):
            rows = slice(c * lc, (c + 1) * lc)
            yacc_ref[rows, cols] = ys[c] * zg[rows]
        if emit_state:
            st_ref[e, 0, p] = hf
            st_ref[e, 1, p] = hb

    y = yacc_ref[...]
    ms = jnp.mean(y * y, axis=-1, keepdims=True)
    ya_ref[e] = (y * lax.rsqrt(ms + EPS) * ng_ref[...]).astype(BF16)


def _ssd(zs3, x23, bc3, dt3, prm, norm_g, h0, *, emit_state, n_elems):
    b, s, _ = zs3.shape
    has_h0 = h0 is not None
    ne = n_elems
    in_specs = [
        pl.BlockSpec((ne, s, SSD_INNER), lambda i: (i, 0, 0)),
        pl.BlockSpec((ne, s, SSD_INNER), lambda i: (i, 0, 0)),
        pl.BlockSpec((ne, s, SSD_XBC - SSD_INNER), lambda i: (i, 0, 0)),
        pl.BlockSpec((ne, s, LANES), lambda i: (i, 0, 0)),
        pl.BlockSpec((8, LANES), lambda i: (0, 0)),
        pl.BlockSpec((1, SSD_INNER), lambda i: (0, 0)),
    ]
    args = [zs3, x23, bc3, dt3, prm, norm_g]
    state_block = (ne, 2, HEAD_PAIRS, LANES, SSD_STATE)
    if has_h0:
        in_specs.append(pl.BlockSpec(state_block, lambda i: (i, 0, 0, 0, 0)))
        args.append(h0)
    out_specs = [pl.BlockSpec((ne, s, SSD_INNER), lambda i: (i, 0, 0))]
    out_shape = [jax.ShapeDtypeStruct((b, s, SSD_INNER), BF16)]
    if emit_state:
        out_specs.append(pl.BlockSpec(state_block, lambda i: (i, 0, 0, 0, 0)))
        out_shape.append(jax.ShapeDtypeStruct((b, 2, HEAD_PAIRS, LANES, SSD_STATE), F32))
    kern = functools.partial(_ssd_kernel, n_elems=ne, seq_len=s, has_h0=has_h0, emit_state=emit_state)
    return pl.pallas_call(
        kern,
        grid=(b // ne,),
        in_specs=in_specs,
        out_specs=out_specs,
        out_shape=out_shape,
        scratch_shapes=[pltpu.VMEM((s, SSD_INNER), F32)],
        compiler_params=_cparams(("arbitrary",)),
        name="ssd",
    )(*args)


def _pair_masks():
    lane = lax.broadcasted_iota(jnp.int32, (1, LANES), 1)
    return lane < NA_HEAD_DIM


def _qkv_ctx_attn_kernel(x_ref, sh_ref, sc_ref, g_ref, w_ref, o_ref, nk_ref, nv_ref, qkv_ref,
                         *, tm, seq_len, row_base, rows_per_group):
    r = _mod_row(pl.program_id(0), tm, row_base, rows_per_group)
    sh = sh_ref[0, pl.ds(r, 1), :]
    sc = sc_ref[0, pl.ds(r, 1), :]
    h = _norm_mod(x_ref[...], g_ref[...], sh, sc).astype(BF16)
    d = x_ref.shape[1]
    tn = 512
    s = seq_len
    for part, out_ref in ((1, nk_ref), (2, nv_ref)):
        for j in range(d // tn):
            c0 = part * d + j * tn
            qkv_ref[:, c0:c0 + tn] = _dot(h, w_ref[:, c0:c0 + tn])
        for b in range(tm // s):
            rows = slice(b * s, (b + 1) * s)
            out_ref[b, 0] = qkv_ref[rows, part * d:(part + 1) * d].reshape(s, NA_HEADS, NA_HEAD_DIM)
    for j in range(d // tn):
        qkv_ref[:, j * tn:(j + 1) * tn] = _dot(h, w_ref[:, j * tn:(j + 1) * tn])
    first_half = _pair_masks()
    scale = NA_HEAD_DIM ** -0.5
    for b in range(tm // s):
        rows = slice(b * s, (b + 1) * s)
        for p in range(HEAD_PAIRS):
            cols = slice(p * LANES, (p + 1) * LANES)
            q2 = qkv_ref[rows, cols] * scale
            kb = qkv_ref[rows, d + p * LANES:d + (p + 1) * LANES].astype(BF16)
            vb = qkv_ref[rows, 2 * d + p * LANES:2 * d + (p + 1) * LANES].astype(BF16)
            qs = jnp.concatenate([jnp.where(first_half, q2, 0.0), jnp.where(first_half, 0.0, q2)], axis=0).astype(BF16)
            sco = _dot_nt(qs, kb)
            e = jnp.exp(sco - jnp.max(sco, axis=-1, keepdims=True))
            pv = _dot(e.astype(BF16), vb) / jnp.sum(e, axis=-1, keepdims=True)
            o_ref[rows, cols] = jnp.where(first_half, pv[:s], pv[s:]).astype(BF16)


def _qkv_ctx_attn(x, mod, layer, g, w, *, seq_len, row_base, rows_per_group, tm=512):
    m, d = x.shape
    nb = tm // seq_len
    cache_block = (nb, 1, seq_len, NA_HEADS, NA_HEAD_DIM)
    cache_shape = jax.ShapeDtypeStruct((m // seq_len, 1, seq_len, NA_HEADS, NA_HEAD_DIM), F32)
    kern = functools.partial(_qkv_ctx_attn_kernel, tm=tm, seq_len=seq_len, row_base=row_base,
                             rows_per_group=rows_per_group)
    return pl.pallas_call(
        kern,
        grid=(m // tm,),
        in_specs=[
            pl.BlockSpec((tm, d), lambda i: (i, 0)),
            pl.BlockSpec((1, 8, d), lambda i: (layer, 0, 0)),
            pl.BlockSpec((1, 8, d), lambda i: (layer, 0, 1)),
            pl.BlockSpec((1, d), lambda i: (0, 0)),
            _RESIDENT,
        ],
        out_specs=[
            pl.BlockSpec((tm, d), lambda i: (i, 0)),
            pl.BlockSpec(cache_block, lambda i: (i, 0, 0, 0, 0)),
            pl.BlockSpec(cache_block, lambda i: (i, 0, 0, 0, 0)),
        ],
        out_shape=[jax.ShapeDtypeStruct((m, d), BF16), cache_shape, cache_shape],
        scratch_shapes=[pltpu.VMEM((tm, 3 * d), F32)],
        compiler_params=_cparams(("arbitrary",)),
        name="qkv_ctx_attention",
    )(x, mod, mod, g.reshape(1, d), w)


def _lat_attn_kernel(q_ref, k_ref, v_ref, kc_ref, vc_ref, rpb_ref, o_ref, bias_ref, *, rows):
    b = pl.program_id(1)
    first_half = _pair_masks()
    kh = min(NA_KH, rows)
    win = kh * GRID_W
    neg_inf = -jnp.inf

    def win_start(r):
        return min(max(r - kh // 2, 0), rows - kh)

    @pl.when(b == 0)
    def _():
        qc = lax.broadcasted_iota(jnp.int32, (GRID_W, LANES), 0)
        kc = lax.broadcasted_iota(jnp.int32, (GRID_W, LANES), 1)
        cs = jnp.clip(qc - NA_KW // 2, 0, GRID_W - NA_KW)
        col_ok = (kc >= cs) & (kc < cs + NA_KW)
        for hh in range(2):
            for dr in range(2 * NA_KH - 1):
                v = jnp.broadcast_to(rpb_ref[hh, dr:dr + 1, :], (GRID_W, LANES))
                t = pltpu.roll(v, LANES - (NA_KW - 1), 1, stride=1, stride_axis=0)
                tile = jnp.where(col_ok, t, neg_inf)[:, :GRID_W]
                for r in range(rows):
                    i = dr - (NA_KH - 1) + r - win_start(r)
                    if 0 <= i < kh:
                        bias_ref[r, hh * GRID_W:(hh + 1) * GRID_W, i * GRID_W:(i + 1) * GRID_W] = tile

    scale = NA_HEAD_DIM ** -0.5
    kb = k_ref[...].astype(BF16)
    vb = v_ref[...].astype(BF16)
    kcb = kc_ref[...].astype(BF16)
    vcb = vc_ref[...].astype(BF16)
    group = 16
    for r0 in range(0, rows, group):
        rr = range(r0, r0 + group)
        qrows = [slice(r * GRID_W, (r + 1) * GRID_W) for r in rr]
        krows = [slice(win_start(r) * GRID_W, win_start(r) * GRID_W + win) for r in rr]
        qs = []
        for qr in qrows:
            q2 = q_ref[qr, :] * scale
            qs.append(jnp.concatenate([jnp.where(first_half, q2, 0.0), jnp.where(first_half, 0.0, q2)],
                                      axis=0).astype(BF16))
        s_loc = [_dot_nt(qi, kb[kr]) + bias_ref[r] for qi, kr, r in zip(qs, krows, rr)]
        s_ctx = [_dot_nt(qi, kcb) for qi in qs]
        mx = [jnp.maximum(jnp.max(sl, axis=-1, keepdims=True), jnp.max(sc, axis=-1, keepdims=True))
              for sl, sc in zip(s_loc, s_ctx)]
        e_loc = [jnp.exp(sl - m) for sl, m in zip(s_loc, mx)]
        e_ctx = [jnp.exp(sc - m) for sc, m in zip(s_ctx, mx)]
        den = [jnp.sum(el, axis=-1, keepdims=True) + jnp.sum(ec, axis=-1, keepdims=True) for el, ec in zip(e_loc, e_ctx)]
        pv = [(_dot(el.astype(BF16), vb[kr]) + _dot(ec.astype(BF16), vcb)) / dn
              for el, ec, kr, dn in zip(e_loc, e_ctx, krows, den)]
        for qr, o in zip(qrows, pv):
            o_ref[qr, :] = jnp.where(first_half, o[:GRID_W], o[GRID_W:]).astype(BF16)


def _lat_attn(q, k, v, kc, vc, rpb_pad, *, n_batch, n_tok, n_ctx):
    d = q.shape[1]
    rows = n_tok // GRID_W
    kh = min(NA_KH, rows)
    tok_spec = pl.BlockSpec((n_tok, LANES), lambda p, b: (b, p))
    ctx_spec = pl.BlockSpec((n_ctx, LANES), lambda p, b: (b, p))
    kern = functools.partial(_lat_attn_kernel, rows=rows)
    return pl.pallas_call(
        kern,
        grid=(HEAD_PAIRS, n_batch),
        in_specs=[tok_spec, tok_spec, tok_spec, ctx_spec, ctx_spec,
                  pl.BlockSpec((2, 2 * NA_KH, LANES), lambda p, b: (p, 0, 0))],
        out_specs=tok_spec,
        out_shape=jax.ShapeDtypeStruct((n_batch * n_tok, d), BF16),
        scratch_shapes=[pltpu.VMEM((rows, 2 * GRID_W, kh * GRID_W), F32)],
        compiler_params=_cparams(("arbitrary", "arbitrary")),
        name="latent_attention",
    )(q, k, v, kc, vc, rpb_pad)


def _pad_lanes(v):
    return jnp.pad(v, (0, LANES - v.shape[0]))


def _run_stream(x3, mod, row_base, W, *, is_ctx, state_ssd=None, cache_k=None, cache_v=None):
    b, s, d = x3.shape
    m = b * s
    x = x3.reshape(m, d)
    rpg = m if is_ctx else s
    kw = dict(row_base=row_base, rows_per_group=rpg)

    zs, x2, bca, yb, dtr = _in_proj(x, mod, W['norm_mix_g'][0], W['w_in'], W['w_in_tail'], W['ssd_conv_w'], W['ssd_conv_b'],
                                    W['sc_conv_w_rot'], seq_len=s, tm=max(s, 512), **kw)
    h0 = None
    if not is_ctx:
        h0 = state_ssd[:, 0].reshape(b, 2, HEAD_PAIRS, LANES, SSD_STATE)
    res = _ssd(zs.reshape(b, s, -1), x2.reshape(b, s, -1), bca.reshape(b, s, -1), dtr.reshape(b, s, LANES),
               W['ssd_prm'], W['ssd_norm_g'], h0, emit_state=is_ctx, n_elems=2 if is_ctx else 1)
    ya = res[0].reshape(m, SSD_INNER)
    x = _mix_ffn(x, mod, 0, [ya, yb], W['mix0_w_out'], W['norm_ffn_g'][0], *W['ffn'], W['final_norm_g'], seq_len=s,
                 final_norm=False, **kw)

    if is_ctx:
        o, new_k, new_v = _qkv_ctx_attn(x, mod, 1, W['norm_mix_g'][1], W['na_w_qkv'], seq_len=s, **kw)
    else:
        q, k, v = _nm_matmul(x, mod, 1, 0, 1, W['norm_mix_g'][1], W['na_w_qkv'], n_groups=3, group_width=D_MODEL, **kw)
        n_ctx = cache_k.shape[2]
        kc = cache_k[:, 0].reshape(b * n_ctx, d)
        vc = cache_v[:, 0].reshape(b * n_ctx, d)
        o = _lat_attn(q, k, v, kc, vc, W['rpb_pad'], n_batch=b, n_tok=s, n_ctx=n_ctx)
    x = _mix_ffn(x, mod, 1, [o], W['na_w_out'], W['norm_ffn_g'][1], *W['ffn'], W['final_norm_g'], seq_len=s,
                 final_norm=True, **kw)
    y = x.reshape(b, s, d)
    if is_ctx:
        new_state = res[1].reshape(b, 1, 2, SSD_HEADS, SSD_HEADDIM, SSD_STATE)
        return y, new_state, new_k, new_v
    return y


def kernel(x_prompt, x_sample, state_ssd, cache_k, cache_v, c, c_ctx, ada_w, ada_b, norm_mix_g, norm_ffn_g, ssd_w_in,
           ssd_conv_w, ssd_conv_b, ssd_dt_bias, ssd_a_log, ssd_d, ssd_norm_g, sc_conv_w, mix0_w_out, na_w_qkv, na_rpb,
           na_w_out, ffn_w_gate, ffn_w_up, ffn_conv_w, ffn_w_down, final_norm_g):
    n_lat = x_sample.shape[0]
    cvec = jnp.concatenate([c_ctx[None, :], c, jnp.zeros((8 - 1 - n_lat, D_MODEL), F32)], axis=0)
    mod = _ada(cvec, ada_w, ada_b)

    w_in = ssd_w_in[0].astype(BF16)
    w_in_tail = jnp.pad(ssd_w_in[0][:, TAIL_COL0:], ((0, 0), (0, LANES - MIX_SHIFT))).astype(BF16)
    w_out0 = mix0_w_out[0]
    w_out0 = jnp.concatenate([w_out0[:SSD_INNER], jnp.roll(w_out0[SSD_INNER:], MIX_SHIFT, axis=0)], axis=0).astype(BF16)
    prm =jnp.stack([_pad_lanes(ssd_dt_bias[0].reshape(-1)), _pad_lanes(ssd_a_log[0].reshape(-1)),
                     _pad_lanes(ssd_d[0, 0]), _pad_lanes(ssd_d[0, 1])] + [jnp.zeros((LANES,), F32)] * 4, axis=0)
    rpb = na_rpb[0]
    rpb_pad = jnp.pad(rpb, ((0, 0), (0, 1), (0, LANES - rpb.shape[2])))
    W = {
        'norm_mix_g': norm_mix_g, 'norm_ffn_g': norm_ffn_g, 'final_norm_g': final_norm_g,
        'w_in': w_in, 'w_in_tail': w_in_tail,
        'ssd_conv_w': ssd_conv_w[0], 'ssd_conv_b': ssd_conv_b[0].reshape(1, SSD_XBC), 'ssd_prm': prm,
        'ssd_norm_g': ssd_norm_g[0].reshape(1, SSD_INNER), 'sc_conv_w_rot': jnp.roll(sc_conv_w[0], MIX_SHIFT, axis=1),
        'mix0_w_out': w_out0,
        'na_w_qkv': na_w_qkv[0].astype(BF16), 'rpb_pad': rpb_pad, 'na_w_out': na_w_out[0].astype(BF16),
        'ffn': (ffn_w_gate.astype(BF16), ffn_w_up.astype(BF16), ffn_conv_w, ffn_w_down.astype(BF16)),
    }
    y_prompt, new_state, new_k, new_v = _run_stream(x_prompt, mod, 0, W, is_ctx=True)
    y_sample = _run_stream(x_sample, mod, 1, W, is_ctx=False, state_ssd=state_ssd, cache_k=cache_k, cache_v=cache_v)
    return (y_prompt, y_sample, new_state, new_k, new_v)
```
